```python
import math
import jax, jax.numpy as jnp
from jax import lax
import numpy as np

D_MODEL = 1024
BATCH = 8
SEQ = 8192
DEPTH = 2

CHUNK = 64
MEM_LEN = 256
Q_BLOCK = 128
N_BRANCH = 4
BRANCH_W = D_MODEL // 2
N_IN_PIECES = 12
IN_COLS = N_IN_PIECES * BRANCH_W + N_BRANCH * D_MODEL
SSM_GROUP = 16
SSM_GROUPS = BRANCH_W // SSM_GROUP
SSM_STATE = 64
DT_MIN = 1e-3
DT_MAX = 1e-1
RET_HEADS = 4
RET_DK = BRANCH_W // RET_HEADS
RET_DV = BRANCH_W // RET_HEADS
DIFF_HEADS = 4
DIFF_DH = BRANCH_W // (2 * DIFF_HEADS)
DIFF_DV = 2 * DIFF_DH
MEM_HEADS = 4
MEM_DH = BRANCH_W // MEM_HEADS
ROPE_THETA = 10000.0
EPS = 1e-6
NEG_INF = -1e30

kernel_name = "hybrid_s5_retention_diffattn_memory_trunk"


def rms_norm(x, g=None):
    x32 = x.astype(jnp.float32)
    y = x32 * lax.rsqrt(jnp.mean(x32 * x32, axis=-1, keepdims=True) + EPS)
    if g is not None:
        y = y * g.astype(jnp.float32)
    return y.astype(x.dtype)


def rope_tables(seq, inv_freq):
    pos = jnp.arange(seq, dtype=jnp.float32)
    ang = pos[:, None] * inv_freq[None, :]
    return jnp.cos(ang), jnp.sin(ang)


def rope(x, cos, sin):
    shape = (x.shape[1],) + (1,) * (x.ndim - 3) + (cos.shape[-1],)
    c = cos.reshape(shape)
    s = sin.reshape(shape)
    x1, x2 = jnp.split(x, 2, axis=-1)
    return jnp.concatenate([x1 * c - x2 * s, x2 * c + x1 * s], axis=-1).astype(x.dtype)


def _ssm_combine(left, right):
    ar_l, ai_l, br_l, bi_l = left
    ar_r, ai_r, br_r, bi_r = right
    ar = ar_r * ar_l - ai_r * ai_l
    ai = ar_r * ai_l + ai_r * ar_l
    br = ar_r * br_l - ai_r * bi_l + br_r
    bi = ar_r * bi_l + ai_r * br_l + bi_r
    return (ar, ai, br, bi)


def s5_branch(u, lam_re, lam_im, log_dt, b_re, b_im, c_re, c_im, d_skip, w_glu, b_glu):
    f32 = jnp.float32
    bsz, seq, _ = u.shape
    u32 = u.astype(f32)
    ug = u32.reshape(bsz, seq, SSM_GROUPS, SSM_GROUP)
    lr = jnp.minimum(lam_re.astype(f32), -1e-4)
    li = lam_im.astype(f32)
    dt = jnp.exp(log_dt.astype(f32))[:, None]
    mag = jnp.exp(lr * dt)
    ab_re = mag * jnp.cos(li * dt)
    ab_im = mag * jnp.sin(li * dt)
    nr = ab_re - 1.0
    ni = ab_im
    den = lr * lr + li * li
    f_re = (nr * lr + ni * li) / den
    f_im = (ni * lr - nr * li) / den
    br_ = b_re.astype(f32)
    bi_ = b_im.astype(f32)
    bb_re = f_re[:, :, None] * br_ - f_im[:, :, None] * bi_
    bb_im = f_re[:, :, None] * bi_ + f_im[:, :, None] * br_
    bu_re = jnp.einsum('bsgh,gph->bsgp', ug, bb_re)
    bu_im = jnp.einsum('bsgh,gph->bsgp', ug, bb_im)
    a_re = jnp.broadcast_to(ab_re, bu_re.shape)
    a_im = jnp.broadcast_to(ab_im, bu_im.shape)
    _, _, xr, xi = lax.associative_scan(_ssm_combine, (a_re, a_im, bu_re, bu_im), axis=1)
    y = (jnp.einsum('bsgp,ghp->bsgh', xr, c_re.astype(f32))
         - jnp.einsum('bsgp,ghp->bsgh', xi, c_im.astype(f32)))
    y = y.reshape(bsz, seq, BRANCH_W) + d_skip.astype(f32) * u32
    z = jax.nn.gelu(y)
    out = z * jax.nn.sigmoid(jnp.einsum('bsc,ce->bse', z, w_glu.astype(f32)) + b_glu.astype(f32))
    return out.astype(u.dtype)


def retention_branch(q, k, v, cos, sin):
    f32 = jnp.float32
    bsz, seq = q.shape[0], q.shape[1]
    n_chunks = seq // CHUNK
    qr = rope(q.astype(f32), cos, sin)
    kr = rope(k.astype(f32), cos, sin) * (RET_DK ** -0.5)
    v32 = v.astype(f32)
    log_gamma = jnp.log1p(-(2.0 ** (-5.0 - jnp.arange(RET_HEADS, dtype=f32))))
    idx = jnp.arange(CHUNK, dtype=f32)
    intra = jnp.exp(jnp.abs(idx[:, None] - idx[None, :])[None] * log_gamma[:, None, None])
    q_dec = jnp.exp((idx + 1.0)[None, :] * log_gamma[:, None])
    k_dec = jnp.exp((CHUNK - 1.0 - idx)[None, :] * log_gamma[:, None])
    chunk_dec = jnp.exp(CHUNK * log_gamma)

    def to_chunks(t):
        return t.reshape(bsz, n_chunks, CHUNK, RET_HEADS, t.shape[-1]).transpose(1, 0, 3, 2, 4)

    def step(state, xs):
        qc, kc, vc = xs
        scores = jnp.einsum('bhid,bhjd->bhij', qc, kc) * intra[None]
        o = (jnp.einsum('bhij,bhjv->bhiv', scores, vc)
             + jnp.einsum('bhid,bhdv->bhiv', qc, state) * q_dec[None, :, :, None])
        state = (state * chunk_dec[None, :, None, None]
                 + jnp.einsum('bhjd,bhjv->bhdv', kc * k_dec[None, :, :, None], vc))
        return state, o

    state0 = jnp.zeros((bsz, RET_HEADS, RET_DK, RET_DV), f32)
    _, o = lax.scan(step, state0, (to_chunks(qr), to_chunks(kr), to_chunks(v32)))
    o = o.transpose(1, 0, 3, 2, 4).reshape(bsz, seq, RET_HEADS, RET_DV)
    o = rms_norm(o)
    return o.reshape(bsz, seq, RET_HEADS * RET_DV).astype(q.dtype)


def diff_attn_branch(q, k, v, cos, sin, qn_g, kn_g, lq1, lk1, lq2, lk2, hn_g, lambda_init):
    f32 = jnp.float32
    bsz, seq = q.shape[0], q.shape[1]
    n_blocks = seq // Q_BLOCK
    qr = rope(rms_norm(q, qn_g), cos, sin) * (DIFF_DH ** -0.5)
    kr = rope(rms_norm(k, kn_g), cos, sin)
    lam = (jnp.exp(jnp.sum(lq1.astype(f32) * lk1.astype(f32)))
           - jnp.exp(jnp.sum(lq2.astype(f32) * lk2.astype(f32))) + lambda_init)
    key_chunk = jnp.arange(seq) // CHUNK
    qb = qr.reshape(bsz, n_blocks, Q_BLOCK, DIFF_HEADS, 2, DIFF_DH).transpose(1, 0, 2, 3, 4, 5)

    def attend(args):
        q_blk, blk = args
        q_chunk = (blk * Q_BLOCK + jnp.arange(Q_BLOCK)) // CHUNK
        mask = key_chunk[None, :] <= q_chunk[:, None]
        s = jnp.einsum('bqhmd,bkhmd->bhmqk', q_blk, kr).astype(f32)
        s = jnp.where(mask, s, NEG_INF)
        p = jax.nn.softmax(s, axis=-1)
        a = p[:, :, 0] - lam * p[:, :, 1]
        return jnp.einsum('bhqk,bkhv->bqhv', a.astype(v.dtype), v)

    o = lax.map(attend, (qb, jnp.arange(n_blocks)))
    o = o.transpose(1, 0, 2, 3, 4).reshape(bsz, seq, DIFF_HEADS, DIFF_DV)
    o = rms_norm(o, hn_g) * (1.0 - lambda_init)
    return o.reshape(bsz, seq, DIFF_HEADS * DIFF_DV).astype(q.dtype)


def memory_branch(q, mem_n, w_mem_kv, qn_g, kn_g):
    bsz, seq = q.shape[0], q.shape[1]
    m = mem_n.shape[1]
    kv = jnp.einsum('bmd,dc->bmc', mem_n, w_mem_kv)
    mk, mv = jnp.split(kv, 2, axis=-1)
    mk = rms_norm(mk.reshape(bsz, m, MEM_HEADS, MEM_DH), kn_g)
    mv = mv.reshape(bsz, m, MEM_HEADS, MEM_DH)
    qh = rms_norm(q.reshape(bsz, seq, MEM_HEADS, MEM_DH), qn_g) * (MEM_DH ** -0.5)
    s = jnp.einsum('bshd,bmhd->bhsm', qh, mk).astype(jnp.float32)
    p = jax.nn.softmax(s, axis=-1)
    o = jnp.einsum('bhsm,bmhd->bshd', p.astype(mv.dtype), mv)
    return o.reshape(bsz, seq, BRANCH_W).astype(q.dtype)


def setup_inputs(seed: int = 0) -> dict:
    key = jax.random.key(seed)
    ks = jax.random.split(key, 32)
    f32 = jnp.float32
    nrm = lambda k, shape, scale: jax.random.normal(k, shape, f32) * scale
    gain = lambda k, shape: 1.0 + 0.02 * jax.random.normal(k, shape, f32)
    lam_im_base = math.pi * jnp.arange(SSM_STATE, dtype=f32)
    return {
        "x": nrm(ks[0], (BATCH, SEQ, D_MODEL), 1.0),
        "mem": nrm(ks[1], (BATCH, MEM_LEN, D_MODEL), 1.0),
        "norm_g": gain(ks[2], (DEPTH, D_MODEL)),
        "w_in": nrm(ks[3], (DEPTH, D_MODEL, IN_COLS), D_MODEL ** -0.5),
        "ssm_lambda_re": -0.5 + 0.01 * jax.random.normal(ks[4], (DEPTH, SSM_GROUPS, SSM_STATE), f32),
        "ssm_lambda_im": lam_im_base[None, None, :] + 0.01 * jax.random.normal(ks[5], (DEPTH, SSM_GROUPS, SSM_STATE), f32),
        "ssm_log_dt": jax.random.uniform(ks[6], (DEPTH, SSM_GROUPS), f32, minval=math.log(DT_MIN), maxval=math.log(DT_MAX)),
        "ssm_b_re": nrm(ks[7], (DEPTH, SSM_GROUPS, SSM_STATE, SSM_GROUP), (2 * SSM_GROUP) ** -0.5),
        "ssm_b_im": nrm(ks[8], (DEPTH, SSM_GROUPS, SSM_STATE, SSM_GROUP), (2 * SSM_GROUP) ** -0.5),
        "ssm_c_re": nrm(ks[9], (DEPTH, SSM_GROUPS, SSM_GROUP, SSM_STATE), SSM_STATE ** -0.5),
        "ssm_c_im": nrm(ks[10], (DEPTH, SSM_GROUPS, SSM_GROUP, SSM_STATE), SSM_STATE ** -0.5),
        "ssm_d": nrm(ks[11], (DEPTH, BRANCH_W), 1.0),
        "ssm_w_glu": nrm(ks[12], (DEPTH, BRANCH_W, BRANCH_W), BRANCH_W ** -0.5),
        "ssm_b_glu": nrm(ks[13], (DEPTH, BRANCH_W), 0.01),
        "diff_q_norm_g": gain(ks[14], (DEPTH, DIFF_DH)),
        "diff_k_norm_g": gain(ks[15], (DEPTH, DIFF_DH)),
        "diff_lambda_q1": nrm(ks[16], (DEPTH, DIFF_DH), 0.1),
        "diff_lambda_k1": nrm(ks[17], (DEPTH, DIFF_DH), 0.1),
        "diff_lambda_q2": nrm(ks[18], (DEPTH, DIFF_DH), 0.1),
        "diff_lambda_k2": nrm(ks[19], (DEPTH, DIFF_DH), 0.1),
        "diff_head_norm_g": gain(ks[20], (DEPTH, DIFF_DV)),
        "mem_norm_g": gain(ks[21], (DEPTH, D_MODEL)),
        "w_mem_kv": nrm(ks[22], (DEPTH, D_MODEL, 2 * BRANCH_W), D_MODEL ** -0.5),
        "mem_q_norm_g": gain(ks[23], (DEPTH, MEM_DH)),
        "mem_k_norm_g": gain(ks[24], (DEPTH, MEM_DH)),
        "w_branch": nrm(ks[25], (DEPTH, N_BRANCH, BRANCH_W, D_MODEL), BRANCH_W ** -0.5),
        "b_merge": nrm(ks[26], (DEPTH, N_BRANCH, D_MODEL), 0.01),
        "w_out": nrm(ks[27], (DEPTH, D_MODEL, D_MODEL), D_MODEL ** -0.5),
    }


def reference(x, mem, norm_g, w_in, ssm_lambda_re, ssm_lambda_im, ssm_log_dt, ssm_b_re, ssm_b_im,
              ssm_c_re, ssm_c_im, ssm_d, ssm_w_glu, ssm_b_glu, diff_q_norm_g, diff_k_norm_g,
              diff_lambda_q1, diff_lambda_k1, diff_lambda_q2, diff_lambda_k2, diff_head_norm_g,
              mem_norm_g, w_mem_kv, mem_q_norm_g, mem_k_norm_g, w_branch, b_merge, w_out):
    f32 = jnp.float32
    bsz, seq, _ = x.shape
    diff_inv = ROPE_THETA ** (-jnp.arange(0, DIFF_DH, 2, dtype=f32) / DIFF_DH)
    d_cos, d_sin = rope_tables(seq, diff_inv)
    ret_inv = 1.0 / (ROPE_THETA ** jnp.linspace(0.0, 1.0, RET_DK // 2, dtype=f32))
    r_cos, r_sin = rope_tables(seq, ret_inv)
    split_at = [BRANCH_W * i for i in range(1, N_IN_PIECES + 1)]

    for l in range(DEPTH):
        lambda_init = 0.8 - 0.6 * math.exp(-0.3 * l)
        h = rms_norm(x, norm_g[l])
        proj = jnp.einsum('bsd,dc->bsc', h, w_in[l])
        (ssm_u, ssm_gate, ret_q, ret_k, ret_v, ret_gate, diff_q, diff_k, diff_v, diff_gate,
         mem_q, mem_gate, merge) = jnp.split(proj, split_at, axis=-1)

        a_out = s5_branch(ssm_u, ssm_lambda_re[l], ssm_lambda_im[l], ssm_log_dt[l], ssm_b_re[l],
                          ssm_b_im[l], ssm_c_re[l], ssm_c_im[l], ssm_d[l], ssm_w_glu[l], ssm_b_glu[l])
        a_out = a_out * jax.nn.silu(ssm_gate)

        b_out = retention_branch(ret_q.reshape(bsz, seq, RET_HEADS, RET_DK),
                                 ret_k.reshape(bsz, seq, RET_HEADS, RET_DK),
                                 ret_v.reshape(bsz, seq, RET_HEADS, RET_DV), r_cos, r_sin)
        b_out = b_out * jax.nn.silu(ret_gate)

        c_out = diff_attn_branch(diff_q.reshape(bsz, seq, DIFF_HEADS, 2, DIFF_DH),
                                 diff_k.reshape(bsz, seq, DIFF_HEADS, 2, DIFF_DH),
                                 diff_v.reshape(bsz, seq, DIFF_HEADS, DIFF_DV), d_cos, d_sin,
                                 diff_q_norm_g[l], diff_k_norm_g[l], diff_lambda_q1[l], diff_lambda_k1[l],
                                 diff_lambda_q2[l], diff_lambda_k2[l], diff_head_norm_g[l], lambda_init)
        c_out = c_out * jax.nn.silu(diff_gate)

        mem_n = rms_norm(mem, mem_norm_g[l])
        m_out = memory_branch(mem_q, mem_n, w_mem_kv[l], mem_q_norm_g[l], mem_k_norm_g[l])
        m_out = m_out * jax.nn.silu(mem_gate)

        branches = jnp.stack([a_out, b_out, c_out, m_out], axis=2)
        br = jnp.einsum('bsnc,ncd->bsnd', branches, w_branch[l])
        gates = jax.nn.sigmoid(merge.reshape(bsz, seq, N_BRANCH, D_MODEL) + b_merge[l])
        merged = jnp.sum(gates * br, axis=2)
        x = (x + jnp.einsum('bsd,de->bse', merged, w_out[l])).astype(x.dtype)
    return x
```

```python
import functools
import math

import numpy as np
import jax
import jax.numpy as jnp
from jax import lax
from jax.experimental import pallas as pl
from jax.experimental.pallas import tpu as pltpu

F32 = jnp.float32
BF16 = jnp.bfloat16

D_MODEL = 1024
BRANCH_W = D_MODEL // 2
N_BRANCH = 4
N_IN_PIECES = 12
CHUNK = 64
SSM_GROUP = 16
SSM_GROUPS = BRANCH_W // SSM_GROUP
SSM_STATE = 64
HEADS = 4
HEAD_W = BRANCH_W // HEADS
DIFF_DH = HEAD_W // 2
ROPE_THETA = 10000.0
EPS = 1e-6
NEG_INF = -1e30

LANES = 128
S5_CHUNK = LANES
S5_FLAT = SSM_GROUP * S5_CHUNK
PROJ_COLS = (N_IN_PIECES - 2) * BRANCH_W + N_BRANCH * D_MODEL
P_RET_Q, P_RET_K, P_RET_V, P_RET_G, P_DIF_Q, P_DIF_K, P_DIF_V, P_DIF_G, P_MEM_Q, P_MEM_G, P_MERGE = range(11)

VMEM_LIMIT = 56 * 1024 * 1024

NT_DIMS = (((1,), (1,)), ((), ()))
TN_DIMS = (((0,), (0,)), ((), ()))


def _params(*sem):
    return pltpu.CompilerParams(dimension_semantics=sem, vmem_limit_bytes=VMEM_LIMIT)


def _rms(x, eps=EPS):
    return x * lax.rsqrt(jnp.mean(x * x, axis=-1, keepdims=True) + eps)


def _silu(x):
    return x * jax.nn.sigmoid(x)


def _in_proj_kernel(x_ref, g_ref, w_ref, o_ref, h_ref):
    @pl.when(pl.program_id(1) == 0)
    def _():
        h_ref[...] = (_rms(x_ref[...]) * g_ref[...]).astype(BF16)

    o_ref[...] = jnp.dot(h_ref[...], w_ref[...], preferred_element_type=F32).astype(o_ref.dtype)


def _in_proj(x2, g, w_bf, tm, tn):
    n, d = x2.shape
    cols = w_bf.shape[1]
    return pl.pallas_call(
        _in_proj_kernel,
        grid=(n // tm, cols // tn),
        in_specs=[pl.BlockSpec((tm, d), lambda i, j: (i, 0)),
                  pl.BlockSpec((1, d), lambda i, j: (0, 0)),
                  pl.BlockSpec((d, tn), lambda i, j: (0, j))],
        out_specs=pl.BlockSpec((tm, tn), lambda i, j: (i, j)),
        out_shape=jax.ShapeDtypeStruct((n, cols), BF16),
        scratch_shapes=[pltpu.VMEM((tm, d), BF16)],
        compiler_params=_params("parallel", "arbitrary"),
        name="in_proj",
    )(x2, g, w_bf)


def _in_proj_t_kernel(x_ref, g_ref, wt_ref, o_ref):
    h = (_rms(x_ref[...]) * g_ref[...]).astype(BF16)
    o_ref[...] = lax.dot_general(wt_ref[...], h, NT_DIMS, preferred_element_type=F32).astype(o_ref.dtype)


def _in_proj_t(x2, g, wt_bf, tm):
    n, d = x2.shape
    rows = wt_bf.shape[0]
    return pl.pallas_call(
        _in_proj_t_kernel,
        grid=(n // tm,),
        in_specs=[pl.BlockSpec((tm, d), lambda i: (i, 0)),
                  pl.BlockSpec((1, d), lambda i: (0, 0)),
                  pl.BlockSpec((rows, d), lambda i: (0, 0))],
        out_specs=pl.BlockSpec((rows, tm), lambda i: (0, i)),
        out_shape=jax.ShapeDtypeStruct((rows, n), BF16),
        compiler_params=_params("parallel"),
        name="in_proj_t",
    )(x2, g, wt_bf)


def _rope_table_kernel(inv_ref, sgn_ref, cos_ref, sin_ref):
    ts = cos_ref.shape[0]
    pos = (lax.broadcasted_iota(jnp.int32, (ts, LANES), 0) + pl.program_id(1) * ts).astype(F32)
    ang = pos * inv_ref[...]
    cos_ref[...] = jnp.cos(ang)
    sin_ref[...] = jnp.sin(ang) * sgn_ref[...]


def _rope_tables(seq, inv, sgn, ts):
    spec_in = pl.BlockSpec((None, 1, LANES), lambda t, i: (t, 0, 0))
    spec_out = pl.BlockSpec((None, ts, LANES), lambda t, i: (t, i, 0))
    return pl.pallas_call(
        _rope_table_kernel,
        grid=(2, seq // ts),
        in_specs=[spec_in, spec_in],
        out_specs=[spec_out, spec_out],
        out_shape=[jax.ShapeDtypeStruct((2, seq, LANES), F32)] * 2,
        compiler_params=_params("parallel", "parallel"),
        name="rope_tables",
    )(inv, sgn)


def _qk_prep_kernel(rq_ref, rk_ref, dq_ref, dk_ref, cos_ref, sin_ref, qg_ref, kg_ref,
                    orq_ref, ork_ref, odq_ref, odk_ref):
    cos_d, sin_d = cos_ref[0], sin_ref[0]
    cos_r, sin_r = cos_ref[1], sin_ref[1]
    ts = cos_d.shape[0]
    lane = lax.broadcasted_iota(jnp.int32, (ts, LANES), 1)
    low_map = lane < DIFF_DH
    first_half = (lane % DIFF_DH) < (DIFF_DH // 2)

    def seg_norm(x):
        x2 = x * x
        lo = jnp.sum(jnp.where(low_map, x2, 0.0), axis=-1, keepdims=True)
        hi = jnp.sum(jnp.where(low_map, 0.0, x2), axis=-1, keepdims=True)
        ms = jnp.where(low_map, lo, hi) * (1.0 / DIFF_DH)
        return x * lax.rsqrt(ms + EPS)

    def rope_d(x):
        partner = jnp.where(first_half, pltpu.roll(x, LANES - DIFF_DH // 2, 1),
                            pltpu.roll(x, DIFF_DH // 2, 1))
        return x * cos_d + partner * sin_d

    def rope_r(x):
        return x * cos_r + pltpu.roll(x, HEAD_W // 2, 1) * sin_r

    for h in range(HEADS):
        sl = slice(h * HEAD_W, (h + 1) * HEAD_W)
        orq_ref[:, sl] = rope_r(rq_ref[:, sl].astype(F32)).astype(BF16)
        ork_ref[:, sl] = (rope_r(rk_ref[:, sl].astype(F32)) * (HEAD_W ** -0.5)).astype(BF16)
        q = seg_norm(dq_ref[:, sl].astype(F32)) * qg_ref[...]
        odq_ref[:, sl] = (rope_d(q) * (DIFF_DH ** -0.5)).astype(BF16)
        k = seg_norm(dk_ref[:, sl].astype(F32)) * kg_ref[...]
        odk_ref[:, sl] = rope_d(k).astype(BF16)


def _qk_prep(proj, cos_t, sin_t, qg, kg, seq, ts):
    n = proj.shape[0]
    nsb = seq // ts

    def piece(p):
        return pl.BlockSpec((ts, BRANCH_W), lambda i: (i, p))

    tab = pl.BlockSpec((2, ts, LANES), lambda i: (0, i % nsb, 0))
    vec = pl.BlockSpec((1, LANES), lambda i: (0, 0))
    out = pl.BlockSpec((ts, BRANCH_W), lambda i: (i, 0))
    return pl.pallas_call(
        _qk_prep_kernel,
        grid=(n // ts,),
        in_specs=[piece(P_RET_Q), piece(P_RET_K), piece(P_DIF_Q), piece(P_DIF_K), tab, tab, vec, vec],
        out_specs=[out] * 4,
        out_shape=[jax.ShapeDtypeStruct((n, BRANCH_W), BF16)] * 4,
        compiler_params=_params("parallel"),
        name="qk_prep",
    )(proj, proj, proj, proj, cos_t, sin_t, qg, kg)


def _s5_prep_kernel(lam_c_ref, lam_r_ref, b_ref, bt_ref, c_ref, ct_ref,
                    m_ref, p_ref, q_ref, dec_ref, kern_ref):
    T = S5_CHUNK

    def disc(lam_re, lam_im, log_dt):
        lr = jnp.minimum(lam_re, -1e-4)
        li = lam_im
        dt = jnp.exp(log_dt)
        mag = jnp.exp(lr * dt)
        ab_re = mag * jnp.cos(li * dt)
        ab_im = mag * jnp.sin(li * dt)
        nr = ab_re - 1.0
        ni = ab_im
        den = lr * lr + li * li
        f_re = (nr * lr + ni * li) / den
        f_im = (ni * lr - nr * li) / den
        return lr * dt, li * dt, f_re, f_im

    def powers(ldt_re, ldt_im, k):
        mag = jnp.exp(k * ldt_re)
        return mag * jnp.cos(k * ldt_im), mag * jnp.sin(k * ldt_im)

    ldr, ldi, f_re, f_im = disc(lam_c_ref[0], lam_c_ref[1], lam_c_ref[2])
    b_re, b_im = b_ref[0], b_ref[1]
    bb_re = f_re * b_re - f_im * b_im
    bb_im = f_re * b_im + f_im * b_re
    k_row = lax.broadcasted_iota(jnp.int32, (1, T), 1).astype(F32)
    pw_re, pw_im = powers(ldr, ldi, k_row)
    pw1_re, pw1_im = powers(ldr, ldi, k_row + 1.0)
    c_re, c_im = c_ref[0], c_ref[1]
    ct_re, ct_im = ct_ref[0], ct_ref[1]

    for c in range(SSM_GROUP):
        w_re = pw_re * bb_re[:, c:c + 1] - pw_im * bb_im[:, c:c + 1]
        w_im = pw_re * bb_im[:, c:c + 1] + pw_im * bb_re[:, c:c + 1]
        kern = (jnp.dot(c_re, w_re, preferred_element_type=F32, precision=lax.Precision.HIGHEST)
                - jnp.dot(c_im, w_im, preferred_element_type=F32, precision=lax.Precision.HIGHEST))
        for co in range(SSM_GROUP):
            kern_ref[c * SSM_GROUP + co] = kern[co:co + 1, :]
        q_ref[0, :, c * T:(c + 1) * T] = (ct_re[:, c:c + 1] * pw1_re - ct_im[:, c:c + 1] * pw1_im).astype(q_ref.dtype)
        q_ref[1, :, c * T:(c + 1) * T] = (-ct_re[:, c:c + 1] * pw1_im - ct_im[:, c:c + 1] * pw1_re).astype(q_ref.dtype)

    jj = lax.broadcasted_iota(jnp.int32, (T, T), 0)
    tt = lax.broadcasted_iota(jnp.int32, (T, T), 1)
    causal = tt >= jj

    def toeplitz_block(i, carry):
        c, co = i // SSM_GROUP, i % SSM_GROUP
        rows = pl.ds(pl.multiple_of(c * T, T), T)
        cols = pl.ds(pl.multiple_of(co * T, T), T)
        row = jnp.broadcast_to(kern_ref[i], (T, T))
        toep = pltpu.roll(row, 0, 1, stride=1, stride_axis=0)
        m_ref[rows, cols] = jnp.where(causal, toep, 0.0).astype(m_ref.dtype)
        return carry

    lax.fori_loop(0, SSM_GROUP * SSM_GROUP, toeplitz_block, 0)

    ldr_r, ldi_r, fr_r, fi_r = disc(lam_r_ref[0], lam_r_ref[1], lam_r_ref[2])
    bt_re, bt_im = bt_ref[0], bt_ref[1]
    bbt_re = fr_r * bt_re - fi_r * bt_im
    bbt_im = fr_r * bt_im + fi_r * bt_re
    k_col = (T - 1.0) - lax.broadcasted_iota(jnp.int32, (T, 1), 0).astype(F32)
    pr_re, pr_im = powers(ldr_r, ldi_r, k_col)
    for c in range(SSM_GROUP):
        p_ref[0, c * T:(c + 1) * T, :] = (pr_re * bbt_re[c:c + 1, :] - pr_im * bbt_im[c:c + 1, :]).astype(p_ref.dtype)
        p_ref[1, c * T:(c + 1) * T, :] = (pr_re * bbt_im[c:c + 1, :] + pr_im * bbt_re[c:c + 1, :]).astype(p_ref.dtype)
    d_re, d_im = powers(ldr_r, ldi_r, float(T))
    dec_ref[0] = d_re
    dec_ref[1] = d_im


def _s5_prep(lam_re, lam_im, log_dt, b_re, b_im, c_re, c_im):
    g, p = lam_re.shape
    ldt = jnp.broadcast_to(log_dt[:, None], (g, p))
    lam = jnp.stack([lam_re, lam_im, ldt], axis=1)
    lam_c = lam[..., None]
    lam_r = lam[:, :, None, :]
    b = jnp.stack([b_re, b_im], axis=1)
    bt = jnp.swapaxes(b, 2, 3)
    c = jnp.stack([c_re, c_im], axis=1)
    ct = jnp.swapaxes(c, 2, 3)

    def spec(shape):
        return pl.BlockSpec((None,) + shape, lambda i: (i,) + (0,) * len(shape))

    return pl.pallas_call(
        _s5_prep_kernel,
        grid=(g,),
        in_specs=[spec((3, p, 1)), spec((3, 1, p)), spec((2, p, SSM_GROUP)), spec((2, SSM_GROUP, p)),
                  spec((2, SSM_GROUP, p)), spec((2, p, SSM_GROUP))],
        out_specs=[spec((S5_FLAT, S5_FLAT)), spec((2, S5_FLAT, p)), spec((2, p, S5_FLAT)), spec((2, 1, p))],
        out_shape=[jax.ShapeDtypeStruct((g, S5_FLAT, S5_FLAT), BF16),
                   jax.ShapeDtypeStruct((g, 2, S5_FLAT, p), BF16),
                   jax.ShapeDtypeStruct((g, 2, p, S5_FLAT), BF16),
                   jax.ShapeDtypeStruct((g, 2, 1, p), F32)],
        scratch_shapes=[pltpu.VMEM((SSM_GROUP * SSM_GROUP, 1, S5_CHUNK), F32)],
        compiler_params=_params("parallel"),
        name="s5_prep",
    )(lam_c, lam_r, b, bt, c, ct)


def _s5_conv_kernel(d_ref, u_ref, m_ref, p_ref, q_ref, dec_ref, y_ref, v_ref, s_ref, *, batch, n_chunks):
    g = pl.program_id(0)
    T = S5_CHUNK
    u_flat = jnp.concatenate([u_ref[c] for c in range(SSM_GROUP)], axis=-1)
    rows = batch * n_chunks
    for ri in range(2):
        v_ref[ri] = jnp.dot(u_flat, p_ref[ri], preferred_element_type=F32).reshape(batch, n_chunks, SSM_STATE)
    dr, di = dec_ref[0], dec_ref[1]

    def body(ch, carry):
        sr, si = carry
        s_ref[0, :, pl.ds(ch, 1), :] = sr
        s_ref[1, :, pl.ds(ch, 1), :] = si
        vr = v_ref[0, :, pl.ds(ch, 1), :]
        vi = v_ref[1, :, pl.ds(ch, 1), :]
        return dr * sr - di * si + vr, dr * si + di * sr + vi

    zero = jnp.zeros((batch, 1, SSM_STATE), F32)
    lax.fori_loop(0, n_chunks, body, (zero, zero))
    y = jnp.dot(u_flat, m_ref[...], preferred_element_type=F32)
    for ri in range(2):
        s_start = s_ref[ri].reshape(rows, SSM_STATE).astype(BF16)
        y += jnp.dot(s_start, q_ref[ri], preferred_element_type=F32)
    for c in range(SSM_GROUP):
        y_ref[c] = y[:, c * T:(c + 1) * T] + d_ref[g * SSM_GROUP + c] * u_ref[c].astype(F32)


def _s5_conv(u_t, d_skip, m, p, q, dec, batch, seq):
    n = u_t.shape[1]
    r = n // S5_CHUNK
    u3 = u_t.reshape(u_t.shape[0], r, S5_CHUNK)
    g = SSM_GROUPS

    def spec(shape):
        return pl.BlockSpec((None,) + shape, lambda i: (i,) + (0,) * len(shape))

    y3 = pl.pallas_call(
        functools.partial(_s5_conv_kernel, batch=batch, n_chunks=seq // S5_CHUNK),
        grid=(g,),
        in_specs=[pl.BlockSpec(memory_space=pltpu.SMEM),
                  pl.BlockSpec((SSM_GROUP, r, S5_CHUNK), lambda i: (i, 0, 0)),
                  spec((S5_FLAT, S5_FLAT)), spec((2, S5_FLAT, SSM_STATE)), spec((2, SSM_STATE, S5_FLAT)),
                  spec((2, 1, SSM_STATE))],
        out_specs=pl.BlockSpec((SSM_GROUP, r, S5_CHUNK), lambda i: (i, 0, 0)),
        out_shape=jax.ShapeDtypeStruct((BRANCH_W, r, S5_CHUNK), F32),
        scratch_shapes=[pltpu.VMEM((2, batch, seq // S5_CHUNK, SSM_STATE), F32)] * 2,
        compiler_params=_params("parallel"),
        name="s5_conv",
    )(d_skip, u3, m, p, q, dec)
    return y3.reshape(BRANCH_W, n)


def _retention_kernel(q_ref, k_ref, v_ref, mask_ref, qd_ref, kd_ref, cd_ref, o_ref, st_ref):
    @pl.when(pl.program_id(2) == 0)
    def _():
        st_ref[...] = jnp.zeros_like(st_ref)

    q, k, v = q_ref[...], k_ref[...], v_ref[...]
    scores = lax.dot_general(q, k, NT_DIMS, preferred_element_type=F32) * mask_ref[...]
    st = st_ref[...]
    o = (jnp.dot(scores.astype(BF16), v, preferred_element_type=F32)
         + jnp.dot(q, st.astype(BF16), preferred_element_type=F32) * qd_ref[...])
    kd = (k.astype(F32) * kd_ref[...]).astype(BF16)
    st_ref[...] = st * cd_ref[...] + lax.dot_general(kd, v, TN_DIMS, preferred_element_type=F32)
    o_ref[...] = _rms(o).astype(o_ref.dtype)


def _retention_consts(blk):
    log_gamma = np.log1p(-(2.0 ** (-5.0 - np.arange(HEADS, dtype=np.float64))))
    idx = np.arange(blk, dtype=np.float64)
    diff = idx[:, None] - idx[None, :]
    same = (np.arange(blk)[:, None] // CHUNK) == (np.arange(blk)[None, :] // CHUNK)
    past = (np.arange(blk)[None, :] // CHUNK) < (np.arange(blk)[:, None] // CHUNK)
    expo = np.where(same, np.abs(diff), diff)
    mask = np.where(same | past, np.exp(expo[None] * log_gamma[:, None, None]), 0.0)
    q_dec = np.exp((idx + 1.0)[None, :] * log_gamma[:, None])
    k_dec = np.exp((blk - 1.0 - idx)[None, :] * log_gamma[:, None])
    c_dec = np.exp(blk * log_gamma)
    bc = lambda a: np.broadcast_to(a[:, :, None], (HEADS, blk, HEAD_W))
    return (jnp.asarray(mask, F32), jnp.asarray(bc(q_dec), F32), jnp.asarray(bc(k_dec), F32),
            jnp.asarray(np.broadcast_to(c_dec[:, None, None], (HEADS, 1, HEAD_W)), F32))


def _retention(rq, rk, proj, batch, seq, blk):
    n = rq.shape[0]
    nb = seq // blk
    mask, q_dec, k_dec, c_dec = _retention_consts(blk)
    tok = lambda b, h, l: (b * nb + l, h)
    return pl.pallas_call(
        _retention_kernel,
        grid=(batch, HEADS, nb),
        in_specs=[pl.BlockSpec((blk, HEAD_W), tok),
                  pl.BlockSpec((blk, HEAD_W), tok),
                  pl.BlockSpec((blk, HEAD_W), lambda b, h, l: (b * nb + l, P_RET_V * HEADS + h)),
                  pl.BlockSpec((None, blk, blk), lambda b, h, l: (h, 0, 0)),
                  pl.BlockSpec((None, blk, HEAD_W), lambda b, h, l: (h, 0, 0)),
                  pl.BlockSpec((None, blk, HEAD_W), lambda b, h, l: (h, 0, 0)),
                  pl.BlockSpec((None, 1, HEAD_W), lambda b, h, l: (h, 0, 0))],
        out_specs=pl.BlockSpec((blk, HEAD_W), tok),
        out_shape=jax.ShapeDtypeStruct((n, BRANCH_W), BF16),
        scratch_shapes=[pltpu.VMEM((HEAD_W, HEAD_W), F32)],
        compiler_params=_params("parallel", "parallel", "arbitrary"),
        name="retention",
    )(rq, rk, proj, mask, q_dec, k_dec, c_dec)


def _diff_attn_kernel(q_ref, k_ref, v_ref, lq1_ref, lk1_ref, lq2_ref, lk2_ref, hg_ref, o_ref,
                      m_ref, l_ref, acc_ref, *, tq, lambda_init):
    i = pl.program_id(2)
    q = q_ref[...]
    lane = lax.broadcasted_iota(jnp.int32, (tq, HEAD_W), 1)
    qs = (jnp.where(lane < DIFF_DH, q, jnp.zeros_like(q)), jnp.where(lane < DIFF_DH, jnp.zeros_like(q), q))
    m_ref[...] = jnp.full_like(m_ref, NEG_INF)
    l_ref[...] = jnp.zeros_like(l_ref)
    acc_ref[...] = jnp.zeros_like(acc_ref)
    qc = lax.broadcasted_iota(jnp.int32, (tq, tq), 0) // CHUNK
    kc = lax.broadcasted_iota(jnp.int32, (tq, tq), 1) // CHUNK
    visible = kc <= qc

    def tile(j, masked):
        kt = k_ref[pl.ds(pl.multiple_of(j * tq, tq), tq), :]
        vt = v_ref[pl.ds(pl.multiple_of(j * tq, tq), tq), :]
        for mp in range(2):
            s = lax.dot_general(qs[mp], kt, NT_DIMS, preferred_element_type=F32)
            if masked:
                s = jnp.where(visible, s, NEG_INF)
            m_prev = m_ref[mp]
            m_new = jnp.maximum(m_prev, jnp.max(s, axis=-1, keepdims=True))
            alpha = jnp.exp(m_prev - m_new)
            p = jnp.exp(s - m_new)
            l_ref[mp] = alpha * l_ref[mp] + jnp.sum(p, axis=-1, keepdims=True)
            acc_ref[mp] = alpha * acc_ref[mp] + jnp.dot(p.astype(BF16), vt, preferred_element_type=F32)
            m_ref[mp] = m_new

    def body(j, carry):
        tile(j, False)
        return carry

    lax.fori_loop(0, i, body, 0)
    tile(i, True)

    lam = (jnp.exp(jnp.sum(lq1_ref[...] * lk1_ref[...], axis=-1, keepdims=True))
           - jnp.exp(jnp.sum(lq2_ref[...] * lk2_ref[...], axis=-1, keepdims=True)) + lambda_init)
    o = acc_ref[0] / l_ref[0] - lam * (acc_ref[1] / l_ref[1])
    o_ref[...] = (_rms(o) * hg_ref[...] * (1.0 - lambda_init)).astype(o_ref.dtype)


def _diff_attn(dq, dk, proj, lq1, lk1, lq2, lk2, hg, batch, seq, tq, lambda_init):
    n = dq.shape[0]
    nq = seq // tq
    vec64 = pl.BlockSpec((1, DIFF_DH), lambda b, h, i: (0, 0))
    return pl.pallas_call(
        functools.partial(_diff_attn_kernel, tq=tq, lambda_init=lambda_init),
        grid=(batch, HEADS, nq),
        in_specs=[pl.BlockSpec((tq, HEAD_W), lambda b, h, i: (b * nq + i, h)),
                  pl.BlockSpec((seq, HEAD_W), lambda b, h, i: (b, h)),
                  pl.BlockSpec((seq, HEAD_W), lambda b, h, i: (b, P_DIF_V * HEADS + h)),
                  vec64, vec64, vec64, vec64,
                  pl.BlockSpec((1, HEAD_W), lambda b, h, i: (0, 0))],
        out_specs=pl.BlockSpec((tq, HEAD_W), lambda b, h, i: (b * nq + i, h)),
        out_shape=jax.ShapeDtypeStruct((n, BRANCH_W), BF16),
        scratch_shapes=[pltpu.VMEM((2, tq, 1), F32), pltpu.VMEM((2, tq, 1), F32),
                        pltpu.VMEM((2, tq, HEAD_W), F32)],
        compiler_params=_params("parallel", "parallel", "arbitrary"),
        name="diff_attn",
    )(dq, dk, proj, lq1, lk1, lq2, lk2, hg)


def _mem_kv_kernel(mem_ref, g_ref, w_ref, kg_ref, k_ref, v_ref):
    h = (_rms(mem_ref[...]) * g_ref[...]).astype(BF16)
    kv = jnp.dot(h, w_ref[...], preferred_element_type=F32)
    for hd in range(HEADS):
        sl = slice(hd * HEAD_W, (hd + 1) * HEAD_W)
        k_ref[:, sl] = (_rms(kv[:, sl]) * kg_ref[...]).astype(BF16)
    v_ref[...] = kv[:, BRANCH_W:].astype(BF16)


def _mem_kv(mem2, g, w_bf, kg, mem_len):
    n, d = mem2.shape
    out = pl.BlockSpec((mem_len, BRANCH_W), lambda i: (i, 0))
    return pl.pallas_call(
        _mem_kv_kernel,
        grid=(n // mem_len,),
        in_specs=[pl.BlockSpec((mem_len, d), lambda i: (i, 0)),
                  pl.BlockSpec((1, d), lambda i: (0, 0)),
                  pl.BlockSpec((d, 2 * BRANCH_W), lambda i: (0, 0)),
                  pl.BlockSpec((1, HEAD_W), lambda i: (0, 0))],
        out_specs=[out, out],
        out_shape=[jax.ShapeDtypeStruct((n, BRANCH_W), BF16)] * 2,
        compiler_params=_params("parallel"),
        name="mem_kv",
    )(mem2, g, w_bf, kg)


def _merge_kernel(x_ref, y_ref, sg_ref, ret_ref, rg_ref, dif_ref, dg_ref, mq_ref, mg_ref, mk_ref, mv_ref,
                  mrg0_ref, mrg1_ref, mrg2_ref, mrg3_ref, wglu_ref, bglu_ref, mqg_ref, wbr_ref, bm_ref, wout_ref,
                  o_ref):
    z = jax.nn.gelu(y_ref[...])
    glu = jnp.dot(wglu_ref[...], z.astype(BF16), preferred_element_type=F32) + bglu_ref[...]
    a_t = z * jax.nn.sigmoid(glu) * _silu(sg_ref[...].astype(F32))
    branches = [lax.dot_general(a_t.astype(BF16), wbr_ref[0], TN_DIMS, preferred_element_type=F32)]
    b_out = ret_ref[...].astype(F32) * _silu(rg_ref[...].astype(F32))
    branches.append(jnp.dot(b_out.astype(BF16), wbr_ref[1], preferred_element_type=F32))
    c_out = dif_ref[...].astype(F32) * _silu(dg_ref[...].astype(F32))
    branches.append(jnp.dot(c_out.astype(BF16), wbr_ref[2], preferred_element_type=F32))
    heads = []
    for hd in range(HEADS):
        sl = slice(hd * HEAD_W, (hd + 1) * HEAD_W)
        qh = (_rms(mq_ref[:, sl].astype(F32)) * mqg_ref[...] * (HEAD_W ** -0.5)).astype(BF16)
        s = lax.dot_general(qh, mk_ref[:, sl], NT_DIMS, preferred_element_type=F32)
        p = jnp.exp(s - jnp.max(s, axis=-1, keepdims=True))
        o = jnp.dot(p.astype(BF16), mv_ref[:, sl], preferred_element_type=F32)
        heads.append(o / jnp.sum(p, axis=-1, keepdims=True))
    m_out = jnp.concatenate(heads, axis=-1) * _silu(mg_ref[...].astype(F32))
    branches.append(jnp.dot(m_out.astype(BF16), wbr_ref[3], preferred_element_type=F32))
    merged = None
    for nb, mrg_ref in enumerate((mrg0_ref, mrg1_ref, mrg2_ref, mrg3_ref)):
        gate = jax.nn.sigmoid(mrg_ref[...].astype(F32) + bm_ref[nb:nb + 1, :])
        merged = gate * branches[nb] if merged is None else merged + gate * branches[nb]
    o_ref[...] = x_ref[...] + jnp.dot(merged.astype(BF16), wout_ref[...], preferred_element_type=F32)


def _merge(x2, y_t, u_t, proj, ret_o, dif_o, mk, mv, wglu_t, bglu, mqg, wbr, bm, wout, seq, mem_len, tm):
    n, d = x2.shape

    def piece(p):
        return pl.BlockSpec((tm, BRANCH_W), lambda i: (i, p))

    def gate_piece(nb):
        return pl.BlockSpec((tm, D_MODEL), lambda i: (i, P_MERGE * BRANCH_W // D_MODEL + nb))

    tok = pl.BlockSpec((tm, BRANCH_W), lambda i: (i, 0))
    mem = pl.BlockSpec((mem_len, BRANCH_W), lambda i: ((i * tm) // seq, 0))
    full = lambda a: pl.BlockSpec(a.shape, lambda i: (0,) * a.ndim)
    return pl.pallas_call(
        _merge_kernel,
        grid=(n // tm,),
        in_specs=[pl.BlockSpec((tm, d), lambda i: (i, 0)),
                  pl.BlockSpec((BRANCH_W, tm), lambda i: (0, i)),
                  pl.BlockSpec((BRANCH_W, tm), lambda i: (1, i)),
                  tok, piece(P_RET_G), tok, piece(P_DIF_G), piece(P_MEM_Q), piece(P_MEM_G), mem, mem,
                  gate_piece(0), gate_piece(1), gate_piece(2), gate_piece(3),
                  full(wglu_t), full(bglu), full(mqg), full(wbr), full(bm), full(wout)],
        out_specs=pl.BlockSpec((tm, d), lambda i: (i, 0)),
        out_shape=jax.ShapeDtypeStruct((n, d), F32),
        compiler_params=_params("parallel"),
        name="merge",
    )(x2, y_t, u_t, ret_o, proj, dif_o, proj, proj, proj, mk, mv, proj, proj, proj, proj,
      wglu_t, bglu, mqg, wbr, bm, wout)


def _tile(n, pref):
    return pref if n % pref == 0 else n


def kernel(x, mem, norm_g, w_in, ssm_lambda_re, ssm_lambda_im, ssm_log_dt, ssm_b_re, ssm_b_im, ssm_c_re,
           ssm_c_im, ssm_d, ssm_w_glu, ssm_b_glu, diff_q_norm_g, diff_k_norm_g, diff_lambda_q1,
           diff_lambda_k1, diff_lambda_q2, diff_lambda_k2, diff_head_norm_g, mem_norm_g, w_mem_kv,
           mem_q_norm_g, mem_k_norm_g, w_branch, b_merge, w_out):
    batch, seq, d = x.shape
    mem_len = mem.shape[1]
    depth = w_in.shape[0]
    n = batch * seq
    assert d == D_MODEL and seq % S5_CHUNK == 0

    diff_inv = ROPE_THETA ** (-jnp.arange(0, DIFF_DH, 2, dtype=F32) / DIFF_DH)
    ret_inv = 1.0 / (ROPE_THETA ** jnp.linspace(0.0, 1.0, HEAD_W // 2, dtype=F32))
    inv = jnp.stack([jnp.tile(diff_inv, LANES // diff_inv.shape[0]),
                     jnp.tile(ret_inv, LANES // ret_inv.shape[0])])[:, None, :]
    lane = np.arange(LANES)
    sgn = np.stack([np.where(lane % DIFF_DH < DIFF_DH // 2, -1.0, 1.0),
                    np.where(lane < HEAD_W // 2, -1.0, 1.0)])[:, None, :]
    cos_t, sin_t = _rope_tables(seq, inv, jnp.asarray(sgn, F32), _tile(seq, 1024))

    x2 = x.reshape(n, d)
    mem2 = mem.reshape(batch * mem_len, d)
    tile2 = lambda v: jnp.tile(v, LANES // v.shape[0])[None, :]
    for l in range(depth):
        lambda_init = 0.8 - 0.6 * math.exp(-0.3 * l)
        g = norm_g[l][None, :]
        w = w_in[l]
        proj = _in_proj(x2, g, w[:, 2 * BRANCH_W:].astype(BF16), _tile(n, 1024), 1024)
        u_t = _in_proj_t(x2, g, w[:, :2 * BRANCH_W].T.astype(BF16), _tile(n, 1024))

        m_op, p_op, q_op, dec = _s5_prep(ssm_lambda_re[l], ssm_lambda_im[l], ssm_log_dt[l], ssm_b_re[l],
                                         ssm_b_im[l], ssm_c_re[l], ssm_c_im[l])
        y_t = _s5_conv(u_t, ssm_d[l], m_op, p_op, q_op, dec, batch, seq)

        rq, rk, dq, dk = _qk_prep(proj, cos_t, sin_t, tile2(diff_q_norm_g[l]), tile2(diff_k_norm_g[l]),
                                  seq, _tile(seq, 1024))
        ret_o = _retention(rq, rk, proj, batch, seq, _tile(seq, 256))
        dif_o = _diff_attn(dq, dk, proj, diff_lambda_q1[l][None], diff_lambda_k1[l][None],
                           diff_lambda_q2[l][None], diff_lambda_k2[l][None], diff_head_norm_g[l][None],
                           batch, seq, _tile(seq, 512), lambda_init)
        mk, mv = _mem_kv(mem2, mem_norm_g[l][None, :], w_mem_kv[l].astype(BF16), mem_k_norm_g[l][None], mem_len)
        x2 = _merge(x2, y_t, u_t, proj, ret_o, dif_o, mk, mv, ssm_w_glu[l].T.astype(BF16),
                    ssm_b_glu[l][:, None], mem_q_norm_g[l][None], w_branch[l].astype(BF16), b_merge[l],
                    w_out[l].astype(BF16), seq, mem_len, _tile(seq, 512))
    return x2.reshape(batch, seq, d)
```

```python
import functools
import math

import numpy as np
import jax
import jax.numpy as jnp
from jax import lax
from jax.experimental import pallas as pl
from jax.experimental.pallas import tpu as pltpu

F32 = jnp.float32
BF16 = jnp.bfloat16

D_MODEL = 1024
BRANCH_W = D_MODEL // 2
N_BRANCH = 4
N_IN_PIECES = 12
CHUNK = 64
SSM_GROUP = 16
SSM_GROUPS = BRANCH_W // SSM_GROUP
SSM_STATE = 64
HEADS = 4
HEAD_W = BRANCH_W // HEADS
DIFF_DH = HEAD_W // 2
ROPE_THETA = 10000.0
EPS = 1e-6
NEG_INF = -1e30
LOG2_E = math.log2(math.e)
MAX_SCORE_RANGE_LOG2 = 100.0

LANES = 128
S5_CHUNK = LANES
S5_FLAT = SSM_GROUP * S5_CHUNK
PROJ_COLS = (N_IN_PIECES - 2) * BRANCH_W + N_BRANCH * D_MODEL
P_RET_Q, P_RET_K, P_RET_V, P_RET_G, P_DIF_Q, P_DIF_K, P_DIF_V, P_DIF_G, P_MEM_Q, P_MEM_G, P_MERGE = range(11)

VMEM_LIMIT = 56 * 1024 * 1024

NT_DIMS = (((1,), (1,)), ((), ()))
TN_DIMS = (((0,), (0,)), ((), ()))


def _params(*sem):
    return pltpu.CompilerParams(dimension_semantics=sem, vmem_limit_bytes=VMEM_LIMIT)


def _rms(x, eps=EPS):
    return x * lax.rsqrt(jnp.mean(x * x, axis=-1, keepdims=True) + eps)


def _silu(x):
    return x * jax.nn.sigmoid(x)


def _in_proj_kernel(x_ref, g_ref, w_ref, o_ref, h_ref):
    @pl.when(pl.program_id(1) == 0)
    def _():
        h_ref[...] = (_rms(x_ref[...]) * g_ref[...]).astype(BF16)

    o_ref[...] = jnp.dot(h_ref[...], w_ref[...], preferred_element_type=F32).astype(o_ref.dtype)


def _in_proj(x2, g, w_bf, tm, tn):
    n, d = x2.shape
    cols = w_bf.shape[1]
    return pl.pallas_call(
        _in_proj_kernel,
        grid=(n // tm, cols // tn),
        in_specs=[pl.BlockSpec((tm, d), lambda i, j: (i, 0)),
                  pl.BlockSpec((1, d), lambda i, j: (0, 0)),
                  pl.BlockSpec((d, tn), lambda i, j: (0, j))],
        out_specs=pl.BlockSpec((tm, tn), lambda i, j: (i, j)),
        out_shape=jax.ShapeDtypeStruct((n, cols), BF16),
        scratch_shapes=[pltpu.VMEM((tm, d), BF16)],
        compiler_params=_params("parallel", "arbitrary"),
        name="in_proj",
    )(x2, g, w_bf)


def _in_proj_t_kernel(x_ref, g_ref, wt_ref, o_ref):
    h = (_rms(x_ref[...]) * g_ref[...]).astype(BF16)
    o_ref[...] = lax.dot_general(wt_ref[...], h, NT_DIMS, preferred_element_type=F32).astype(o_ref.dtype)


def _in_proj_t(x2, g, wt_bf, tm):
    n, d = x2.shape
    rows = wt_bf.shape[0]
    return pl.pallas_call(
        _in_proj_t_kernel,
        grid=(n // tm,),
        in_specs=[pl.BlockSpec((tm, d), lambda i: (i, 0)),
                  pl.BlockSpec((1, d), lambda i: (0, 0)),
                  pl.BlockSpec((rows, d), lambda i: (0, 0))],
        out_specs=pl.BlockSpec((rows, tm), lambda i: (0, i)),
        out_shape=jax.ShapeDtypeStruct((rows, n), BF16),
        compiler_params=_params("parallel"),
        name="in_proj_t",
    )(x2, g, wt_bf)


def _rope_table_kernel(inv_ref, sgn_ref, cos_ref, sin_ref):
    ts = cos_ref.shape[0]
    pos = (lax.broadcasted_iota(jnp.int32, (ts, LANES), 0) + pl.program_id(1) * ts).astype(F32)
    ang = pos * inv_ref[...]
    cos_ref[...] = jnp.cos(ang)
    sin_ref[...] = jnp.sin(ang) * sgn_ref[...]


def _rope_tables(seq, inv, sgn, ts):
    spec_in = pl.BlockSpec((None, 1, LANES), lambda t, i: (t, 0, 0))
    spec_out = pl.BlockSpec((None, ts, LANES), lambda t, i: (t, i, 0))
    return pl.pallas_call(
        _rope_table_kernel,
        grid=(2, seq // ts),
        in_specs=[spec_in, spec_in],
        out_specs=[spec_out, spec_out],
        out_shape=[jax.ShapeDtypeStruct((2, seq, LANES), F32)] * 2,
        compiler_params=_params("parallel", "parallel"),
        name="rope_tables",
    )(inv, sgn)


def _qk_prep_kernel(rq_ref, rk_ref, dq_ref, dk_ref, cos_ref, sin_ref, qg_ref, kg_ref,
                    orq_ref, ork_ref, odq_ref, odk_ref, stat_ref):
    cos_d, sin_d = cos_ref[0], sin_ref[0]
    cos_r, sin_r = cos_ref[1], sin_ref[1]
    ts = cos_d.shape[0]
    lane = lax.broadcasted_iota(jnp.int32, (ts, LANES), 1)
    low_map = lane < DIFF_DH
    first_half = (lane % DIFF_DH) < (DIFF_DH // 2)

    def seg_norm(x):
        x2 = x * x
        lo = jnp.sum(jnp.where(low_map, x2, 0.0), axis=-1, keepdims=True)
        hi = jnp.sum(jnp.where(low_map, 0.0, x2), axis=-1, keepdims=True)
        ms = jnp.where(low_map, lo, hi) * (1.0 / DIFF_DH)
        return x * lax.rsqrt(ms + EPS)

    def rope_d(x):
        partner = jnp.where(first_half, pltpu.roll(x, LANES - DIFF_DH // 2, 1),
                            pltpu.roll(x, DIFF_DH // 2, 1))
        return x * cos_d + partner * sin_d

    def rope_r(x):
        return x * cos_r + pltpu.roll(x, HEAD_W // 2, 1) * sin_r

    stat_row = lax.broadcasted_iota(jnp.int32, stat_ref.shape, 0)
    stats = jnp.zeros(stat_ref.shape, F32)

    def max_sq_norms(y):
        y2 = jnp.square(y.astype(F32))
        lo = jnp.sum(jnp.where(low_map, y2, 0.0), axis=-1, keepdims=True)
        hi = jnp.sum(jnp.where(low_map, 0.0, y2), axis=-1, keepdims=True)
        return jnp.max(lo, axis=0, keepdims=True), jnp.max(hi, axis=0, keepdims=True)

    for h in range(HEADS):
        sl = slice(h * HEAD_W, (h + 1) * HEAD_W)
        orq_ref[:, sl] = rope_r(rq_ref[:, sl].astype(F32)).astype(BF16)
        ork_ref[:, sl] = (rope_r(rk_ref[:, sl].astype(F32)) * (HEAD_W ** -0.5)).astype(BF16)
        q = seg_norm(dq_ref[:, sl].astype(F32)) * qg_ref[...]
        q = (rope_d(q) * (DIFF_DH ** -0.5 * LOG2_E)).astype(BF16)
        odq_ref[:, sl] = q
        k = seg_norm(dk_ref[:, sl].astype(F32)) * kg_ref[...]
        k = rope_d(k).astype(BF16)
        odk_ref[:, sl] = k
        for qk, y in enumerate((q, k)):
            for mp, val in enumerate(max_sq_norms(y)):
                stats = jnp.where(stat_row == qk * 8 + h * 2 + mp, val, stats)
    stat_ref[...] = stats


def _qk_prep(proj, cos_t, sin_t, qg, kg, seq, ts):
    n = proj.shape[0]
    nsb = seq // ts

    def piece(p):
        return pl.BlockSpec((ts, BRANCH_W), lambda i: (i, p))

    tab = pl.BlockSpec((2, ts, LANES), lambda i: (0, i % nsb, 0))
    vec = pl.BlockSpec((1, LANES), lambda i: (0, 0))
    out = pl.BlockSpec((ts, BRANCH_W), lambda i: (i, 0))
    return pl.pallas_call(
        _qk_prep_kernel,
        grid=(n // ts,),
        in_specs=[piece(P_RET_Q), piece(P_RET_K), piece(P_DIF_Q), piece(P_DIF_K), tab, tab, vec, vec],
        out_specs=[out] * 4 + [pl.BlockSpec((None, 16, LANES), lambda i: (i, 0, 0))],
        out_shape=[jax.ShapeDtypeStruct((n, BRANCH_W), BF16)] * 4 + [jax.ShapeDtypeStruct((n // ts, 16, LANES), F32)],
        compiler_params=_params("parallel"),
        name="qk_prep",
    )(proj, proj, proj, proj, cos_t, sin_t, qg, kg)


def _s5_prep_kernel(lam_c_ref, lam_r_ref, b_ref, bt_ref, c_ref, ct_ref,
                    m_ref, p_ref, q_ref, dec_ref, kern_ref):
    T = S5_CHUNK

    def disc(lam_re, lam_im, log_dt):
        lr = jnp.minimum(lam_re, -1e-4)
        li = lam_im
        dt = jnp.exp(log_dt)
        mag = jnp.exp(lr * dt)
        ab_re = mag * jnp.cos(li * dt)
        ab_im = mag * jnp.sin(li * dt)
        nr = ab_re - 1.0
        ni = ab_im
        den = lr * lr + li * li
        f_re = (nr * lr + ni * li) / den
        f_im = (ni * lr - nr * li) / den
        return lr * dt, li * dt, f_re, f_im

    def powers(ldt_re, ldt_im, k):
        mag = jnp.exp(k * ldt_re)
        return mag * jnp.cos(k * ldt_im), mag * jnp.sin(k * ldt_im)

    ldr, ldi, f_re, f_im = disc(lam_c_ref[0], lam_c_ref[1], lam_c_ref[2])
    b_re, b_im = b_ref[0], b_ref[1]
    bb_re = f_re * b_re - f_im * b_im
    bb_im = f_re * b_im + f_im * b_re
    k_row = lax.broadcasted_iota(jnp.int32, (1, T), 1).astype(F32)
    pw_re, pw_im = powers(ldr, ldi, k_row)
    pw1_re, pw1_im = powers(ldr, ldi, k_row + 1.0)
    c_re, c_im = c_ref[0], c_ref[1]
    ct_re, ct_im = ct_ref[0], ct_ref[1]

    for c in range(SSM_GROUP):
        w_re = pw_re * bb_re[:, c:c + 1] - pw_im * bb_im[:, c:c + 1]
        w_im = pw_re * bb_im[:, c:c + 1] + pw_im * bb_re[:, c:c + 1]
        kern = (jnp.dot(c_re, w_re, preferred_element_type=F32, precision=lax.Precision.HIGHEST)
                - jnp.dot(c_im, w_im, preferred_element_type=F32, precision=lax.Precision.HIGHEST))
        for co in range(SSM_GROUP):
            kern_ref[c * SSM_GROUP + co] = kern[co:co + 1, :]
        q_ref[0, :, c * T:(c + 1) * T] = (ct_re[:, c:c + 1] * pw1_re - ct_im[:, c:c + 1] * pw1_im).astype(q_ref.dtype)
        q_ref[1, :, c * T:(c + 1) * T] = (-ct_re[:, c:c + 1] * pw1_im - ct_im[:, c:c + 1] * pw1_re).astype(q_ref.dtype)

    jj = lax.broadcasted_iota(jnp.int32, (T, T), 0)
    tt = lax.broadcasted_iota(jnp.int32, (T, T), 1)
    causal = tt >= jj

    def toeplitz_block(i, carry):
        c, co = i // SSM_GROUP, i % SSM_GROUP
        rows = pl.ds(pl.multiple_of(c * T, T), T)
        cols = pl.ds(pl.multiple_of(co * T, T), T)
        row = jnp.broadcast_to(kern_ref[i], (T, T))
        toep = pltpu.roll(row, 0, 1, stride=1, stride_axis=0)
        m_ref[rows, cols] = jnp.where(causal, toep, 0.0).astype(m_ref.dtype)
        return carry

    lax.fori_loop(0, SSM_GROUP * SSM_GROUP, toeplitz_block, 0)

    ldr_r, ldi_r, fr_r, fi_r = disc(lam_r_ref[0], lam_r_ref[1], lam_r_ref[2])
    bt_re, bt_im = bt_ref[0], bt_ref[1]
    bbt_re = fr_r * bt_re - fi_r * bt_im
    bbt_im = fr_r * bt_im + fi_r * bt_re
    k_col = (T - 1.0) - lax.broadcasted_iota(jnp.int32, (T, 1), 0).astype(F32)
    pr_re, pr_im = powers(ldr_r, ldi_r, k_col)
    for c in range(SSM_GROUP):
        p_ref[0, c * T:(c + 1) * T, :] = (pr_re * bbt_re[c:c + 1, :] - pr_im * bbt_im[c:c + 1, :]).astype(p_ref.dtype)
        p_ref[1, c * T:(c + 1) * T, :] = (pr_re * bbt_im[c:c + 1, :] + pr_im * bbt_re[c:c + 1, :]).astype(p_ref.dtype)
    d_re, d_im = powers(ldr_r, ldi_r, float(T))
    dec_ref[0] = d_re
    dec_ref[1] = d_im


def _s5_prep(lam_re, lam_im, log_dt, b_re, b_im, c_re, c_im):
    g, p = lam_re.shape
    ldt = jnp.broadcast_to(log_dt[:, None], (g, p))
    lam = jnp.stack([lam_re, lam_im, ldt], axis=1)
    lam_c = lam[..., None]
    lam_r = lam[:, :, None, :]
    b = jnp.stack([b_re, b_im], axis=1)
    bt = jnp.swapaxes(b, 2, 3)
    c = jnp.stack([c_re, c_im], axis=1)
    ct = jnp.swapaxes(c, 2, 3)

    def spec(shape):
        return pl.BlockSpec((None,) + shape, lambda i: (i,) + (0,) * len(shape))

    return pl.pallas_call(
        _s5_prep_kernel,
        grid=(g,),
        in_specs=[spec((3, p, 1)), spec((3, 1, p)), spec((2, p, SSM_GROUP)), spec((2, SSM_GROUP, p)),
                  spec((2, SSM_GROUP, p)), spec((2, p, SSM_GROUP))],
        out_specs=[spec((S5_FLAT, S5_FLAT)), spec((2, S5_FLAT, p)), spec((2, p, S5_FLAT)), spec((2, 1, p))],
        out_shape=[jax.ShapeDtypeStruct((g, S5_FLAT, S5_FLAT), BF16),
                   jax.ShapeDtypeStruct((g, 2, S5_FLAT, p), BF16),
                   jax.ShapeDtypeStruct((g, 2, p, S5_FLAT), BF16),
                   jax.ShapeDtypeStruct((g, 2, 1, p), F32)],
        scratch_shapes=[pltpu.VMEM((SSM_GROUP * SSM_GROUP, 1, S5_CHUNK), F32)],
        compiler_params=_params("parallel"),
        name="s5_prep",
    )(lam_c, lam_r, b, bt, c, ct)


def _s5_conv_kernel(d_ref, u_ref, m_ref, p_ref, q_ref, dec_ref, y_ref, v_ref, s_ref, *, batch, n_chunks):
    g = pl.program_id(0)
    T = S5_CHUNK
    u_flat = jnp.concatenate([u_ref[c] for c in range(SSM_GROUP)], axis=-1)
    rows = batch * n_chunks
    for ri in range(2):
        v_ref[ri] = jnp.dot(u_flat, p_ref[ri], preferred_element_type=F32).reshape(batch, n_chunks, SSM_STATE)
    dr, di = dec_ref[0], dec_ref[1]

    def body(ch, carry):
        sr, si = carry
        s_ref[0, :, pl.ds(ch, 1), :] = sr
        s_ref[1, :, pl.ds(ch, 1), :] = si
        vr = v_ref[0, :, pl.ds(ch, 1), :]
        vi = v_ref[1, :, pl.ds(ch, 1), :]
        return dr * sr - di * si + vr, dr * si + di * sr + vi

    zero = jnp.zeros((batch, 1, SSM_STATE), F32)
    lax.fori_loop(0, n_chunks, body, (zero, zero))
    y = jnp.dot(u_flat, m_ref[...], preferred_element_type=F32)
    for ri in range(2):
        s_start = s_ref[ri].reshape(rows, SSM_STATE).astype(BF16)
        y += jnp.dot(s_start, q_ref[ri], preferred_element_type=F32)
    for c in range(SSM_GROUP):
        y_ref[c] = y[:, c * T:(c + 1) * T] + d_ref[g * SSM_GROUP + c] * u_ref[c].astype(F32)


def _s5_conv(u_t, d_skip, m, p, q, dec, batch, seq):
    n = u_t.shape[1]
    r = n // S5_CHUNK
    u3 = u_t.reshape(u_t.shape[0], r, S5_CHUNK)
    g = SSM_GROUPS

    def spec(shape):
        return pl.BlockSpec((None,) + shape, lambda i: (i,) + (0,) * len(shape))

    y3 = pl.pallas_call(
        functools.partial(_s5_conv_kernel, batch=batch, n_chunks=seq // S5_CHUNK),
        grid=(g,),
        in_specs=[pl.BlockSpec(memory_space=pltpu.SMEM),
                  pl.BlockSpec((SSM_GROUP, r, S5_CHUNK), lambda i: (i, 0, 0)),
                  spec((S5_FLAT, S5_FLAT)), spec((2, S5_FLAT, SSM_STATE)), spec((2, SSM_STATE, S5_FLAT)),
                  spec((2, 1, SSM_STATE))],
        out_specs=pl.BlockSpec((SSM_GROUP, r, S5_CHUNK), lambda i: (i, 0, 0)),
        out_shape=jax.ShapeDtypeStruct((BRANCH_W, r, S5_CHUNK), F32),
        scratch_shapes=[pltpu.VMEM((2, batch, seq // S5_CHUNK, SSM_STATE), F32)] * 2,
        compiler_params=_params("parallel"),
        name="s5_conv",
    )(d_skip, u3, m, p, q, dec)
    return y3.reshape(BRANCH_W, n)


def _retention_kernel(q_ref, k_ref, v_ref, mask_ref, qd_ref, kd_ref, cd_ref, o_ref, st_ref):
    @pl.when(pl.program_id(2) == 0)
    def _():
        st_ref[...] = jnp.zeros_like(st_ref)

    q, k, v = q_ref[...], k_ref[...], v_ref[...]
    scores = lax.dot_general(q, k, NT_DIMS, preferred_element_type=F32) * mask_ref[...]
    st = st_ref[...]
    o = (jnp.dot(scores.astype(BF16), v, preferred_element_type=F32)
         + jnp.dot(q, st.astype(BF16), preferred_element_type=F32) * qd_ref[...])
    kd = (k.astype(F32) * kd_ref[...]).astype(BF16)
    st_ref[...] = st * cd_ref[...] + lax.dot_general(kd, v, TN_DIMS, preferred_element_type=F32)
    o_ref[...] = _rms(o).astype(o_ref.dtype)


def _retention_consts(blk):
    log_gamma = np.log1p(-(2.0 ** (-5.0 - np.arange(HEADS, dtype=np.float64))))
    idx = np.arange(blk, dtype=np.float64)
    diff = idx[:, None] - idx[None, :]
    same = (np.arange(blk)[:, None] // CHUNK) == (np.arange(blk)[None, :] // CHUNK)
    past = (np.arange(blk)[None, :] // CHUNK) < (np.arange(blk)[:, None] // CHUNK)
    expo = np.where(same, np.abs(diff), diff)
    mask = np.where(same | past, np.exp(expo[None] * log_gamma[:, None, None]), 0.0)
    q_dec = np.exp((idx + 1.0)[None, :] * log_gamma[:, None])
    k_dec = np.exp((blk - 1.0 - idx)[None, :] * log_gamma[:, None])
    c_dec = np.exp(blk * log_gamma)
    bc = lambda a: np.broadcast_to(a[:, :, None], (HEADS, blk, HEAD_W))
    return (jnp.asarray(mask, F32), jnp.asarray(bc(q_dec), F32), jnp.asarray(bc(k_dec), F32),
            jnp.asarray(np.broadcast_to(c_dec[:, None, None], (HEADS, 1, HEAD_W)), F32))


def _retention(rq, rk, proj, batch, seq, blk):
    n = rq.shape[0]
    nb = seq // blk
    mask, q_dec, k_dec, c_dec = _retention_consts(blk)
    tok = lambda b, h, l: (b * nb + l, h)
    return pl.pallas_call(
        _retention_kernel,
        grid=(batch, HEADS, nb),
        in_specs=[pl.BlockSpec((blk, HEAD_W), tok),
                  pl.BlockSpec((blk, HEAD_W), tok),
                  pl.BlockSpec((blk, HEAD_W), lambda b, h, l: (b * nb + l, P_RET_V * HEADS + h)),
                  pl.BlockSpec((None, blk, blk), lambda b, h, l: (h, 0, 0)),
                  pl.BlockSpec((None, blk, HEAD_W), lambda b, h, l: (h, 0, 0)),
                  pl.BlockSpec((None, blk, HEAD_W), lambda b, h, l: (h, 0, 0)),
                  pl.BlockSpec((None, 1, HEAD_W), lambda b, h, l: (h, 0, 0))],
        out_specs=pl.BlockSpec((blk, HEAD_W), tok),
        out_shape=jax.ShapeDtypeStruct((n, BRANCH_W), BF16),
        scratch_shapes=[pltpu.VMEM((HEAD_W, HEAD_W), F32)],
        compiler_params=_params("parallel", "parallel", "arbitrary"),
        name="retention",
    )(rq, rk, proj, mask, q_dec, k_dec, c_dec)


def _diff_attn_kernel(bound_ref, q_ref, k_ref, v_ref, lq1_ref, lk1_ref, lq2_ref, lk2_ref, hg_ref, o_ref,
                      acc_ref, *stat_refs, tq, lambda_init, fixed_stabiliser):
    b, h, i = pl.program_id(0), pl.program_id(1), pl.program_id(2)
    q = q_ref[...]
    lane = lax.broadcasted_iota(jnp.int32, (tq, HEAD_W), 1)
    qs = (jnp.where(lane < DIFF_DH, q, jnp.zeros_like(q)), jnp.where(lane < DIFF_DH, jnp.zeros_like(q), q))
    acc_ref[...] = jnp.zeros_like(acc_ref)
    if not fixed_stabiliser:
        m_ref, = stat_refs
        m_ref[...] = jnp.full_like(m_ref, NEG_INF)
    qc = lax.broadcasted_iota(jnp.int32, (tq, tq), 0) // CHUNK
    kc = lax.broadcasted_iota(jnp.int32, (tq, tq), 1) // CHUNK
    visible = kc <= qc

    def tile(j, masked):
        rows = pl.ds(pl.multiple_of(j * tq, tq), tq)
        kt = k_ref[rows, :]
        vt = v_ref[rows, :]
        v_ones = jnp.concatenate([vt, jnp.ones_like(vt)], axis=1)
        for mp in range(2):
            s = lax.dot_general(qs[mp], kt, NT_DIMS, preferred_element_type=F32)
            if fixed_stabiliser:
                p = jnp.exp2(s - bound_ref[b, 2 * h + mp])
                if masked:
                    p = jnp.where(visible, p, 0.0)
                acc_ref[mp] += jnp.dot(p.astype(BF16), v_ones, preferred_element_type=F32)
            else:
                if masked:
                    s = jnp.where(visible, s, NEG_INF)
                m_prev = m_ref[mp]
                m_new = jnp.maximum(m_prev, jnp.max(s, axis=-1, keepdims=True))
                p = jnp.exp2(s - m_new)
                acc_ref[mp] = (jnp.exp2(m_prev - m_new) * acc_ref[mp]
                               + jnp.dot(p.astype(BF16), v_ones, preferred_element_type=F32))
                m_ref[mp] = m_new

    def body(j, carry):
        tile(j, False)
        return carry

    lax.fori_loop(0, i, body, 0)
    tile(i, True)

    lam = (jnp.exp(jnp.sum(lq1_ref[...] * lk1_ref[...], axis=-1, keepdims=True))
           - jnp.exp(jnp.sum(lq2_ref[...] * lk2_ref[...], axis=-1, keepdims=True)) + lambda_init)
    o = (acc_ref[0, :, :HEAD_W] / acc_ref[0, :, HEAD_W:]
         - lam * (acc_ref[1, :, :HEAD_W] / acc_ref[1, :, HEAD_W:]))
    o_ref[...] = (_rms(o) * hg_ref[...] * (1.0 - lambda_init)).astype(o_ref.dtype)


def _diff_attn(dq, dk, proj, stats, lq1, lk1, lq2, lk2, hg, batch, seq, tq, lambda_init):
    n = dq.shape[0]
    nq = seq // tq
    norms = jnp.sqrt(jnp.max(stats[:, :, 0].reshape(batch, -1, 2, 2 * HEADS), axis=1))
    bound = 1.01 * norms[:, 0] * norms[:, 1]
    vec64 = pl.BlockSpec((1, DIFF_DH), lambda b, h, i: (0, 0))

    def call(fixed_stabiliser):
        stat_scratch = [] if fixed_stabiliser else [pltpu.VMEM((2, tq, 1), F32)]
        return pl.pallas_call(
            functools.partial(_diff_attn_kernel, tq=tq, lambda_init=lambda_init, fixed_stabiliser=fixed_stabiliser),
            grid=(batch, HEADS, nq),
            in_specs=[pl.BlockSpec(memory_space=pltpu.SMEM),
                      pl.BlockSpec((tq, HEAD_W), lambda b, h, i: (b * nq + i, h)),
                      pl.BlockSpec((seq, HEAD_W), lambda b, h, i: (b, h)),
                      pl.BlockSpec((seq, HEAD_W), lambda b, h, i: (b, P_DIF_V * HEADS + h)),
                      vec64, vec64, vec64, vec64,
                      pl.BlockSpec((1, HEAD_W), lambda b, h, i: (0, 0))],
            out_specs=pl.BlockSpec((tq, HEAD_W), lambda b, h, i: (b * nq + i, h)),
            out_shape=jax.ShapeDtypeStruct((n, BRANCH_W), BF16),
            scratch_shapes=[pltpu.VMEM((2, tq, 2 * HEAD_W), F32)] + stat_scratch,
            compiler_params=_params("parallel", "parallel", "arbitrary"),
            name="diff_attn_fixed" if fixed_stabiliser else "diff_attn_online",
        )(bound, dq, dk, proj, lq1, lk1, lq2, lk2, hg)

    return lax.cond(jnp.max(bound) * 2.0 <= MAX_SCORE_RANGE_LOG2, lambda: call(True), lambda: call(False))


def _mem_kv_kernel(mem_ref, g_ref, w_ref, kg_ref, k_ref, v_ref):
    h = (_rms(mem_ref[...]) * g_ref[...]).astype(BF16)
    kv = jnp.dot(h, w_ref[...], preferred_element_type=F32)
    for hd in range(HEADS):
        sl = slice(hd * HEAD_W, (hd + 1) * HEAD_W)
        k_ref[:, sl] = (_rms(kv[:, sl]) * kg_ref[...]).astype(BF16)
    v_ref[...] = kv[:, BRANCH_W:].astype(BF16)


def _mem_kv(mem2, g, w_bf, kg, mem_len):
    n, d = mem2.shape
    out = pl.BlockSpec((mem_len, BRANCH_W), lambda i: (i, 0))
    return pl.pallas_call(
        _mem_kv_kernel,
        grid=(n // mem_len,),
        in_specs=[pl.BlockSpec((mem_len, d), lambda i: (i, 0)),
                  pl.BlockSpec((1, d), lambda i: (0, 0)),
                  pl.BlockSpec((d, 2 * BRANCH_W), lambda i: (0, 0)),
                  pl.BlockSpec((1, HEAD_W), lambda i: (0, 0))],
        out_specs=[out, out],
        out_shape=[jax.ShapeDtypeStruct((n, BRANCH_W), BF16)] * 2,
        compiler_params=_params("parallel"),
        name="mem_kv",
    )(mem2, g, w_bf, kg)


def _merge_kernel(x_ref, y_ref, sg_ref, ret_ref, rg_ref, dif_ref, dg_ref, mq_ref, mg_ref, mk_ref, mv_ref,
                  mrg0_ref, mrg1_ref, mrg2_ref, mrg3_ref, wglu_ref, bglu_ref, mqg_ref, wbr_ref, bm_ref, wout_ref,
                  o_ref):
    z = jax.nn.gelu(y_ref[...])
    glu = jnp.dot(wglu_ref[...], z.astype(BF16), preferred_element_type=F32) + bglu_ref[...]
    a_t = z * jax.nn.sigmoid(glu) * _silu(sg_ref[...].astype(F32))
    branches = [lax.dot_general(a_t.astype(BF16), wbr_ref[0], TN_DIMS, preferred_element_type=F32)]
    b_out = ret_ref[...].astype(F32) * _silu(rg_ref[...].astype(F32))
    branches.append(jnp.dot(b_out.astype(BF16), wbr_ref[1], preferred_element_type=F32))
    c_out = dif_ref[...].astype(F32) * _silu(dg_ref[...].astype(F32))
    branches.append(jnp.dot(c_out.astype(BF16), wbr_ref[2], preferred_element_type=F32))
    heads = []
    for hd in range(HEADS):
        sl = slice(hd * HEAD_W, (hd + 1) * HEAD_W)
        qh = (_rms(mq_ref[:, sl].astype(F32)) * mqg_ref[...] * (HEAD_W ** -0.5)).astype(BF16)
        s = lax.dot_general(qh, mk_ref[:, sl], NT_DIMS, preferred_element_type=F32)
        p = jnp.exp(s - jnp.max(s, axis=-1, keepdims=True))
        o = jnp.dot(p.astype(BF16), mv_ref[:, sl], preferred_element_type=F32)
        heads.append(o / jnp.sum(p, axis=-1, keepdims=True))
    m_out = jnp.concatenate(heads, axis=-1) * _silu(mg_ref[...].astype(F32))
    branches.append(jnp.dot(m_out.astype(BF16), wbr_ref[3], preferred_element_type=F32))
    merged = None
    for nb, mrg_ref in enumerate((mrg0_ref, mrg1_ref, mrg2_ref, mrg3_ref)):
        gate = jax.nn.sigmoid(mrg_ref[...].astype(F32) + bm_ref[nb:nb + 1, :])
        merged = gate * branches[nb] if merged is None else merged + gate * branches[nb]
    o_ref[...] = x_ref[...] + jnp.dot(merged.astype(BF16), wout_ref[...], preferred_element_type=F32)


def _merge(x2, y_t, u_t, proj, ret_o, dif_o, mk, mv, wglu_t, bglu, mqg, wbr, bm, wout, seq, mem_len, tm):
    n, d = x2.shape

    def piece(p):
        return pl.BlockSpec((tm, BRANCH_W), lambda i: (i, p))

    def gate_piece(nb):
        return pl.BlockSpec((tm, D_MODEL), lambda i: (i, P_MERGE * BRANCH_W // D_MODEL + nb))

    tok = pl.BlockSpec((tm, BRANCH_W), lambda i: (i, 0))
    mem = pl.BlockSpec((mem_len, BRANCH_W), lambda i: ((i * tm) // seq, 0))
    full = lambda a: pl.BlockSpec(a.shape, lambda i: (0,) * a.ndim)
    return pl.pallas_call(
        _merge_kernel,
        grid=(n // tm,),
        in_specs=[pl.BlockSpec((tm, d), lambda i: (i, 0)),
                  pl.BlockSpec((BRANCH_W, tm), lambda i: (0, i)),
                  pl.BlockSpec((BRANCH_W, tm), lambda i: (1, i)),
                  tok, piece(P_RET_G), tok, piece(P_DIF_G), piece(P_MEM_Q), piece(P_MEM_G), mem, mem,
                  gate_piece(0), gate_piece(1), gate_piece(2), gate_piece(3),
                  full(wglu_t), full(bglu), full(mqg), full(wbr), full(bm), full(wout)],
        out_specs=pl.BlockSpec((tm, d), lambda i: (i, 0)),
        out_shape=jax.ShapeDtypeStruct((n, d), F32),
        compiler_params=_params("parallel"),
        name="merge",
    )(x2, y_t, u_t, ret_o, proj, dif_o, proj, proj, proj, mk, mv, proj, proj, proj, proj,
      wglu_t, bglu, mqg, wbr, bm, wout)


def _tile(n, pref):
    return pref if n % pref == 0 else n


def kernel(x, mem, norm_g, w_in, ssm_lambda_re, ssm_lambda_im, ssm_log_dt, ssm_b_re, ssm_b_im, ssm_c_re,
           ssm_c_im, ssm_d, ssm_w_glu, ssm_b_glu, diff_q_norm_g, diff_k_norm_g, diff_lambda_q1,
           diff_lambda_k1, diff_lambda_q2, diff_lambda_k2, diff_head_norm_g, mem_norm_g, w_mem_kv,
           mem_q_norm_g, mem_k_norm_g, w_branch, b_merge, w_out):
    batch, seq, d = x.shape
    mem_len = mem.shape[1]
    depth = w_in.shape[0]
    n = batch * seq
    assert d == D_MODEL and seq % S5_CHUNK == 0

    diff_inv = ROPE_THETA ** (-jnp.arange(0, DIFF_DH, 2, dtype=F32) / DIFF_DH)
    ret_inv = 1.0 / (ROPE_THETA ** jnp.linspace(0.0, 1.0, HEAD_W // 2, dtype=F32))
    inv = jnp.stack([jnp.tile(diff_inv, LANES // diff_inv.shape[0]),
                     jnp.tile(ret_inv, LANES // ret_inv.shape[0])])[:, None, :]
    lane = np.arange(LANES)
    sgn = np.stack([np.where(lane % DIFF_DH < DIFF_DH // 2, -1.0, 1.0),
                    np.where(lane < HEAD_W // 2, -1.0, 1.0)])[:, None, :]
    cos_t, sin_t = _rope_tables(seq, inv, jnp.asarray(sgn, F32), _tile(seq, 1024))

    x2 = x.reshape(n, d)
    mem2 = mem.reshape(batch * mem_len, d)
    tile2 = lambda v: jnp.tile(v, LANES // v.shape[0])[None, :]
    for l in range(depth):
        lambda_init = 0.8 - 0.6 * math.exp(-0.3 * l)
        g = norm_g[l][None, :]
        w = w_in[l]
        proj = _in_proj(x2, g, w[:, 2 * BRANCH_W:].astype(BF16), _tile(n, 1024), 1024)
        u_t = _in_proj_t(x2, g, w[:, :2 * BRANCH_W].T.astype(BF16), _tile(n, 1024))

        m_op, p_op, q_op, dec = _s5_prep(ssm_lambda_re[l], ssm_lambda_im[l], ssm_log_dt[l], ssm_b_re[l],
                                         ssm_b_im[l], ssm_c_re[l], ssm_c_im[l])
        y_t = _s5_conv(u_t, ssm_d[l], m_op, p_op, q_op, dec, batch, seq)

        rq, rk, dq, dk, qk_stats = _qk_prep(proj, cos_t, sin_t, tile2(diff_q_norm_g[l]), tile2(diff_k_norm_g[l]),
                                            seq, _tile(seq, 1024))
        ret_o = _retention(rq, rk, proj, batch, seq, _tile(seq, 256))
        dif_o = _diff_attn(dq, dk, proj, qk_stats, diff_lambda_q1[l][None], diff_lambda_k1[l][None],
                           diff_lambda_q2[l][None], diff_lambda_k2[l][None], diff_head_norm_g[l][None],
                           batch, seq, _tile(seq, 512), lambda_init)
        mk, mv = _mem_kv(mem2, mem_norm_g[l][None, :], w_mem_kv[l].astype(BF16), mem_k_norm_g[l][None], mem_len)
        x2 = _merge(x2, y_t, u_t, proj, ret_o, dif_o, mk, mv, ssm_w_glu[l].T.astype(BF16),
                    ssm_b_glu[l][:, None], mem_q_norm_g[l][None], w_branch[l].astype(BF16), b_merge[l],
                    w_out[l].astype(BF16), seq, mem_len, _tile(seq, 512))
    return x2.reshape(batch, seq, d)
```

```python
import functools
import math

import numpy as np
import jax
import jax.numpy as jnp
from jax import lax
from jax.experimental import pallas as pl
from jax.experimental.pallas import tpu as pltpu

F32 = jnp.float32
BF16 = jnp.bfloat16

D_MODEL = 1024
BRANCH_W = D_MODEL // 2
N_BRANCH = 4
N_IN_PIECES = 12
CHUNK = 64
SSM_GROUP = 16
SSM_GROUPS = BRANCH_W // SSM_GROUP
SSM_STATE = 64
HEADS = 4
HEAD_W = BRANCH_W // HEADS
DIFF_DH = HEAD_W // 2
ROPE_THETA = 10000.0
EPS = 1e-6
NEG_INF = -1e30
LOG2_E = math.log2(math.e)
DIFF_Q_SCALE = DIFF_DH ** -0.5 * LOG2_E
MAX_SCORE_RANGE_LOG2 = 100.0

LANES = 128
S5_CHUNK = LANES
S5_FLAT = SSM_GROUP * S5_CHUNK
PROJ_COLS = (N_IN_PIECES - 2) * BRANCH_W + N_BRANCH * D_MODEL
P_RET_Q, P_RET_K, P_RET_V, P_RET_G, P_DIF_Q, P_DIF_K, P_DIF_V, P_DIF_G, P_MEM_Q, P_MEM_G, P_MERGE = range(11)

VMEM_LIMIT = 56 * 1024 * 1024

NT_DIMS = (((1,), (1,)), ((), ()))
TN_DIMS = (((0,), (0,)), ((), ()))


def _params(*sem):
    return pltpu.CompilerParams(dimension_semantics=sem, vmem_limit_bytes=VMEM_LIMIT)


def _rms(x, eps=EPS):
    return x * lax.rsqrt(jnp.mean(x * x, axis=-1, keepdims=True) + eps)


def _sigmoid(x):
    return 0.5 * jnp.tanh(0.5 * x) + 0.5


def _silu(x):
    return x * _sigmoid(x)


def _in_proj_kernel(x_ref, g_ref, w_ref, o_ref, h_ref):
    @pl.when(pl.program_id(1) == 0)
    def _():
        h_ref[...] = (_rms(x_ref[...]) * g_ref[...]).astype(BF16)

    o_ref[...] = jnp.dot(h_ref[...], w_ref[...], preferred_element_type=F32).astype(o_ref.dtype)


def _in_proj(x2, g, w_bf, tm, tn):
    n, d = x2.shape
    cols = w_bf.shape[1]
    return pl.pallas_call(
        _in_proj_kernel,
        grid=(n // tm, cols // tn),
        in_specs=[pl.BlockSpec((tm, d), lambda i, j: (i, 0)),
                  pl.BlockSpec((1, d), lambda i, j: (0, 0)),
                  pl.BlockSpec((d, tn), lambda i, j: (0, j))],
        out_specs=pl.BlockSpec((tm, tn), lambda i, j: (i, j)),
        out_shape=jax.ShapeDtypeStruct((n, cols), BF16),
        scratch_shapes=[pltpu.VMEM((tm, d), BF16)],
        compiler_params=_params("parallel", "arbitrary"),
        name="in_proj",
    )(x2, g, w_bf)


def _in_proj_t_kernel(x_ref, g_ref, wt_ref, o_ref):
    h = (_rms(x_ref[...]) * g_ref[...]).astype(BF16)
    o_ref[...] = lax.dot_general(wt_ref[...], h, NT_DIMS, preferred_element_type=F32).astype(o_ref.dtype)


def _in_proj_t(x2, g, wt_bf, tm):
    n, d = x2.shape
    rows = wt_bf.shape[0]
    return pl.pallas_call(
        _in_proj_t_kernel,
        grid=(n // tm,),
        in_specs=[pl.BlockSpec((tm, d), lambda i: (i, 0)),
                  pl.BlockSpec((1, d), lambda i: (0, 0)),
                  pl.BlockSpec((rows, d), lambda i: (0, 0))],
        out_specs=pl.BlockSpec((rows, tm), lambda i: (0, i)),
        out_shape=jax.ShapeDtypeStruct((rows, n), BF16),
        compiler_params=_params("parallel"),
        name="in_proj_t",
    )(x2, g, wt_bf)


def _rope_table_kernel(inv_ref, sgn_ref, cos_ref, sin_ref):
    ts = cos_ref.shape[0]
    pos = (lax.broadcasted_iota(jnp.int32, (ts, LANES), 0) + pl.program_id(1) * ts).astype(F32)
    ang = pos * inv_ref[...]
    cos_ref[...] = jnp.cos(ang)
    sin_ref[...] = jnp.sin(ang) * sgn_ref[...]


def _rope_tables(seq, inv, sgn, ts):
    spec_in = pl.BlockSpec((None, 1, LANES), lambda t, i: (t, 0, 0))
    spec_out = pl.BlockSpec((None, ts, LANES), lambda t, i: (t, i, 0))
    return pl.pallas_call(
        _rope_table_kernel,
        grid=(2, seq // ts),
        in_specs=[spec_in, spec_in],
        out_specs=[spec_out, spec_out],
        out_shape=[jax.ShapeDtypeStruct((2, seq, LANES), F32)] * 2,
        compiler_params=_params("parallel", "parallel"),
        name="rope_tables",
    )(inv, sgn)


def _qk_prep_kernel(rq_ref, rk_ref, dq_ref, dk_ref, cos_ref, sin_ref, qg_ref, kg_ref,
                    orq_ref, ork_ref, odq_ref, odk_ref):
    cos_d, sin_d = cos_ref[0], sin_ref[0]
    cos_r, sin_r = cos_ref[1], sin_ref[1]
    ts = cos_d.shape[0]
    lane = lax.broadcasted_iota(jnp.int32, (ts, LANES), 1)
    low_map = lane < DIFF_DH
    first_half = (lane % DIFF_DH) < (DIFF_DH // 2)

    def seg_norm(x):
        x2 = x * x
        lo = jnp.sum(jnp.where(low_map, x2, 0.0), axis=-1, keepdims=True)
        hi = jnp.sum(jnp.where(low_map, 0.0, x2), axis=-1, keepdims=True)
        ms = jnp.where(low_map, lo, hi) * (1.0 / DIFF_DH)
        return x * lax.rsqrt(ms + EPS)

    def rope_d(x):
        partner = jnp.where(first_half, pltpu.roll(x, LANES - DIFF_DH // 2, 1),
                            pltpu.roll(x, DIFF_DH // 2, 1))
        return x * cos_d + partner * sin_d

    def rope_r(x):
        return x * cos_r + pltpu.roll(x, HEAD_W // 2, 1) * sin_r

    for h in range(HEADS):
        sl = slice(h * HEAD_W, (h + 1) * HEAD_W)
        orq_ref[:, sl] = rope_r(rq_ref[:, sl].astype(F32)).astype(BF16)
        ork_ref[:, sl] = (rope_r(rk_ref[:, sl].astype(F32)) * (HEAD_W ** -0.5)).astype(BF16)
        q = seg_norm(dq_ref[:, sl].astype(F32)) * qg_ref[...]
        odq_ref[:, sl] = (rope_d(q) * DIFF_Q_SCALE).astype(BF16)
        k = seg_norm(dk_ref[:, sl].astype(F32)) * kg_ref[...]
        odk_ref[:, sl] = rope_d(k).astype(BF16)


def _qk_prep(proj, cos_t, sin_t, qg, kg, seq, ts):
    n = proj.shape[0]
    nsb = seq // ts

    def piece(p):
        return pl.BlockSpec((ts, BRANCH_W), lambda i: (i, p))

    tab = pl.BlockSpec((2, ts, LANES), lambda i: (0, i % nsb, 0))
    vec = pl.BlockSpec((1, LANES), lambda i: (0, 0))
    out = pl.BlockSpec((ts, BRANCH_W), lambda i: (i, 0))
    return pl.pallas_call(
        _qk_prep_kernel,
        grid=(n // ts,),
        in_specs=[piece(P_RET_Q), piece(P_RET_K), piece(P_DIF_Q), piece(P_DIF_K), tab, tab, vec, vec],
        out_specs=[out] * 4,
        out_shape=[jax.ShapeDtypeStruct((n, BRANCH_W), BF16)] * 4,
        compiler_params=_params("parallel"),
        name="qk_prep",
    )(proj, proj, proj, proj, cos_t, sin_t, qg, kg)


def _s5_prep_kernel(lam_c_ref, lam_r_ref, b_ref, bt_ref, c_ref, ct_ref,
                    m_ref, p_ref, q_ref, dec_ref, kern_ref):
    T = S5_CHUNK

    def disc(lam_re, lam_im, log_dt):
        lr = jnp.minimum(lam_re, -1e-4)
        li = lam_im
        dt = jnp.exp(log_dt)
        mag = jnp.exp(lr * dt)
        ab_re = mag * jnp.cos(li * dt)
        ab_im = mag * jnp.sin(li * dt)
        nr = ab_re - 1.0
        ni = ab_im
        den = lr * lr + li * li
        f_re = (nr * lr + ni * li) / den
        f_im = (ni * lr - nr * li) / den
        return lr * dt, li * dt, f_re, f_im

    def powers(ldt_re, ldt_im, k):
        mag = jnp.exp(k * ldt_re)
        return mag * jnp.cos(k * ldt_im), mag * jnp.sin(k * ldt_im)

    ldr, ldi, f_re, f_im = disc(lam_c_ref[0], lam_c_ref[1], lam_c_ref[2])
    b_re, b_im = b_ref[0], b_ref[1]
    bb_re = f_re * b_re - f_im * b_im
    bb_im = f_re * b_im + f_im * b_re
    k_row = lax.broadcasted_iota(jnp.int32, (1, T), 1).astype(F32)
    pw_re, pw_im = powers(ldr, ldi, k_row)
    pw1_re, pw1_im = powers(ldr, ldi, k_row + 1.0)
    c_re, c_im = c_ref[0], c_ref[1]
    ct_re, ct_im = ct_ref[0], ct_ref[1]

    for c in range(SSM_GROUP):
        w_re = pw_re * bb_re[:, c:c + 1] - pw_im * bb_im[:, c:c + 1]
        w_im = pw_re * bb_im[:, c:c + 1] + pw_im * bb_re[:, c:c + 1]
        kern = (jnp.dot(c_re, w_re, preferred_element_type=F32, precision=lax.Precision.HIGHEST)
                - jnp.dot(c_im, w_im, preferred_element_type=F32, precision=lax.Precision.HIGHEST))
        for co in range(SSM_GROUP):
            kern_ref[c, :, co * T:(co + 1) * T] = kern[co:co + 1, :]
        q_ref[0, :, c * T:(c + 1) * T] = (ct_re[:, c:c + 1] * pw1_re - ct_im[:, c:c + 1] * pw1_im).astype(q_ref.dtype)
        q_ref[1, :, c * T:(c + 1) * T] = (-ct_re[:, c:c + 1] * pw1_im - ct_im[:, c:c + 1] * pw1_re).astype(q_ref.dtype)

    jj = lax.broadcasted_iota(jnp.int32, (T, S5_FLAT), 0)
    tt = lax.broadcasted_iota(jnp.int32, (T, S5_FLAT), 1) % T
    causal = tt >= jj

    def toeplitz_rows(c, carry):
        row = jnp.broadcast_to(kern_ref[c], (T, S5_FLAT))
        toep = pltpu.roll(row, 0, 1, stride=1, stride_axis=0)
        m_ref[pl.ds(pl.multiple_of(c * T, T), T), :] = jnp.where(causal, toep, 0.0).astype(m_ref.dtype)
        return carry

    lax.fori_loop(0, SSM_GROUP, toeplitz_rows, 0)

    ldr_r, ldi_r, fr_r, fi_r = disc(lam_r_ref[0], lam_r_ref[1], lam_r_ref[2])
    bt_re, bt_im = bt_ref[0], bt_ref[1]
    bbt_re = fr_r * bt_re - fi_r * bt_im
    bbt_im = fr_r * bt_im + fi_r * bt_re
    k_col = (T - 1.0) - lax.broadcasted_iota(jnp.int32, (T, 1), 0).astype(F32)
    pr_re, pr_im = powers(ldr_r, ldi_r, k_col)
    for c in range(SSM_GROUP):
        p_ref[0, c * T:(c + 1) * T, :] = (pr_re * bbt_re[c:c + 1, :] - pr_im * bbt_im[c:c + 1, :]).astype(p_ref.dtype)
        p_ref[1, c * T:(c + 1) * T, :] = (pr_re * bbt_im[c:c + 1, :] + pr_im * bbt_re[c:c + 1, :]).astype(p_ref.dtype)
    d_re, d_im = powers(ldr_r, ldi_r, float(T))
    dec_ref[0] = d_re
    dec_ref[1] = d_im


def _s5_prep(lam_re, lam_im, log_dt, b_re, b_im, c_re, c_im):
    g, p = lam_re.shape
    ldt = jnp.broadcast_to(log_dt[:, None], (g, p))
    lam = jnp.stack([lam_re, lam_im, ldt], axis=1)
    lam_c = lam[..., None]
    lam_r = lam[:, :, None, :]
    b = jnp.stack([b_re, b_im], axis=1)
    bt = jnp.swapaxes(b, 2, 3)
    c = jnp.stack([c_re, c_im], axis=1)
    ct = jnp.swapaxes(c, 2, 3)

    def spec(shape):
        return pl.BlockSpec((None,) + shape, lambda i: (i,) + (0,) * len(shape))

    return pl.pallas_call(
        _s5_prep_kernel,
        grid=(g,),
        in_specs=[spec((3, p, 1)), spec((3, 1, p)), spec((2, p, SSM_GROUP)), spec((2, SSM_GROUP, p)),
                  spec((2, SSM_GROUP, p)), spec((2, p, SSM_GROUP))],
        out_specs=[spec((S5_FLAT, S5_FLAT)), spec((2, S5_FLAT, p)), spec((2, p, S5_FLAT)), spec((2, 1, p))],
        out_shape=[jax.ShapeDtypeStruct((g, S5_FLAT, S5_FLAT), BF16),
                   jax.ShapeDtypeStruct((g, 2, S5_FLAT, p), BF16),
                   jax.ShapeDtypeStruct((g, 2, p, S5_FLAT), BF16),
                   jax.ShapeDtypeStruct((g, 2, 1, p), F32)],
        scratch_shapes=[pltpu.VMEM((SSM_GROUP, 1, S5_FLAT), F32)],
        compiler_params=_params("parallel"),
        name="s5_prep",
    )(lam_c, lam_r, b, bt, c, ct)


def _s5_conv_kernel(d_ref, u_ref, m_ref, p_ref, q_ref, dec_ref, y_ref, v_ref, s_ref, *, batch, n_chunks):
    g = pl.program_id(0)
    T = S5_CHUNK
    u_flat = jnp.concatenate([u_ref[c] for c in range(SSM_GROUP)], axis=-1)
    rows = batch * n_chunks
    for ri in range(2):
        v_ref[ri] = jnp.dot(u_flat, p_ref[ri], preferred_element_type=F32).reshape(batch, n_chunks, SSM_STATE)
    dr, di = dec_ref[0], dec_ref[1]

    def body(ch, carry):
        sr, si = carry
        s_ref[0, :, pl.ds(ch, 1), :] = sr
        s_ref[1, :, pl.ds(ch, 1), :] = si
        vr = v_ref[0, :, pl.ds(ch, 1), :]
        vi = v_ref[1, :, pl.ds(ch, 1), :]
        return dr * sr - di * si + vr, dr * si + di * sr + vi

    zero = jnp.zeros((batch, 1, SSM_STATE), F32)
    lax.fori_loop(0, n_chunks, body, (zero, zero))
    y = jnp.dot(u_flat, m_ref[...], preferred_element_type=F32)
    for ri in range(2):
        s_start = s_ref[ri].reshape(rows, SSM_STATE).astype(BF16)
        y += jnp.dot(s_start, q_ref[ri], preferred_element_type=F32)
    for c in range(SSM_GROUP):
        y_ref[c] = y[:, c * T:(c + 1) * T] + d_ref[g * SSM_GROUP + c] * u_ref[c].astype(F32)


def _s5_conv(u_t, d_skip, m, p, q, dec, batch, seq):
    n = u_t.shape[1]
    r = n // S5_CHUNK
    u3 = u_t.reshape(u_t.shape[0], r, S5_CHUNK)
    g = SSM_GROUPS

    def spec(shape):
        return pl.BlockSpec((None,) + shape, lambda i: (i,) + (0,) * len(shape))

    y3 = pl.pallas_call(
        functools.partial(_s5_conv_kernel, batch=batch, n_chunks=seq // S5_CHUNK),
        grid=(g,),
        in_specs=[pl.BlockSpec(memory_space=pltpu.SMEM),
                  pl.BlockSpec((SSM_GROUP, r, S5_CHUNK), lambda i: (i, 0, 0)),
                  spec((S5_FLAT, S5_FLAT)), spec((2, S5_FLAT, SSM_STATE)), spec((2, SSM_STATE, S5_FLAT)),
                  spec((2, 1, SSM_STATE))],
        out_specs=pl.BlockSpec((SSM_GROUP, r, S5_CHUNK), lambda i: (i, 0, 0)),
        out_shape=jax.ShapeDtypeStruct((BRANCH_W, r, S5_CHUNK), F32),
        scratch_shapes=[pltpu.VMEM((2, batch, seq // S5_CHUNK, SSM_STATE), F32)] * 2,
        compiler_params=_params("parallel"),
        name="s5_conv",
    )(d_skip, u3, m, p, q, dec)
    return y3.reshape(BRANCH_W, n)


def _retention_kernel(q_ref, k_ref, v_ref, mask_ref, qd_ref, kd_ref, cd_ref, o_ref, st_ref):
    @pl.when(pl.program_id(1) == 0)
    def _():
        st_ref[...] = jnp.zeros_like(st_ref)

    for h in range(HEADS):
        sl = slice(h * HEAD_W, (h + 1) * HEAD_W)
        q, k, v = q_ref[:, sl], k_ref[:, sl], v_ref[:, sl]
        scores = lax.dot_general(q, k, NT_DIMS, preferred_element_type=F32) * mask_ref[h]
        st = st_ref[h]
        o = (jnp.dot(scores.astype(BF16), v, preferred_element_type=F32)
             + jnp.dot(q, st.astype(BF16), preferred_element_type=F32) * qd_ref[h])
        kd = (k.astype(F32) * kd_ref[h]).astype(BF16)
        st_ref[h] = st * cd_ref[h] + lax.dot_general(kd, v, TN_DIMS, preferred_element_type=F32)
        o_ref[:, sl] = _rms(o).astype(o_ref.dtype)


def _retention_consts(blk):
    log_gamma = np.log1p(-(2.0 ** (-5.0 - np.arange(HEADS, dtype=np.float64))))
    idx = np.arange(blk, dtype=np.float64)
    diff = idx[:, None] - idx[None, :]
    same = (np.arange(blk)[:, None] // CHUNK) == (np.arange(blk)[None, :] // CHUNK)
    past = (np.arange(blk)[None, :] // CHUNK) < (np.arange(blk)[:, None] // CHUNK)
    expo = np.where(same, np.abs(diff), diff)
    mask = np.where(same | past, np.exp(expo[None] * log_gamma[:, None, None]), 0.0)
    q_dec = np.exp((idx + 1.0)[None, :] * log_gamma[:, None])
    k_dec = np.exp((blk - 1.0 - idx)[None, :] * log_gamma[:, None])
    c_dec = np.exp(blk * log_gamma)
    bc = lambda a: np.broadcast_to(a[:, :, None], (HEADS, blk, HEAD_W))
    return (jnp.asarray(mask, F32), jnp.asarray(bc(q_dec), F32), jnp.asarray(bc(k_dec), F32),
            jnp.asarray(np.broadcast_to(c_dec[:, None, None], (HEADS, 1, HEAD_W)), F32))


def _retention(rq, rk, proj, batch, seq, blk):
    n = rq.shape[0]
    nb = seq // blk
    mask, q_dec, k_dec, c_dec = _retention_consts(blk)
    tok = pl.BlockSpec((blk, BRANCH_W), lambda b, l: (b * nb + l, 0))
    full = lambda a: pl.BlockSpec(a.shape, lambda b, l: (0,) * a.ndim)
    return pl.pallas_call(
        _retention_kernel,
        grid=(batch, nb),
        in_specs=[tok, tok, pl.BlockSpec((blk, BRANCH_W), lambda b, l: (b * nb + l, P_RET_V)),
                  full(mask), full(q_dec), full(k_dec), full(c_dec)],
        out_specs=tok,
        out_shape=jax.ShapeDtypeStruct((n, BRANCH_W), BF16),
        scratch_shapes=[pltpu.VMEM((HEADS, HEAD_W, HEAD_W), F32)],
        compiler_params=_params("parallel", "arbitrary"),
        name="retention",
    )(rq, rk, proj, mask, q_dec, k_dec, c_dec)


def _diff_attn_kernel(bound_ref, q_ref, k_ref, v_ref, lq1_ref, lk1_ref, lq2_ref, lk2_ref, hg_ref, o_ref,
                      acc_ref, *stat_refs, tq, tk, lambda_init, fixed_stabiliser):
    i = pl.program_id(2)
    n_sub = tq // tk
    q = q_ref[...]
    lane = lax.broadcasted_iota(jnp.int32, (tq, HEAD_W), 1)
    qs = (jnp.where(lane < DIFF_DH, q, jnp.zeros_like(q)), jnp.where(lane < DIFF_DH, jnp.zeros_like(q), q))
    acc_ref[...] = jnp.zeros_like(acc_ref)
    if fixed_stabiliser:
        shift = bound_ref[0, 0]
    else:
        m_ref, = stat_refs
        m_ref[...] = jnp.full_like(m_ref, NEG_INF)
    qc = lax.broadcasted_iota(jnp.int32, (tq, tk), 0) // CHUNK
    kc = lax.broadcasted_iota(jnp.int32, (tq, tk), 1) // CHUNK

    def tile(j, diag):
        rows = pl.ds(pl.multiple_of(j * tk, tk), tk)
        kt = k_ref[rows, :]
        vt = v_ref[rows, :]
        v_ones = jnp.concatenate([vt, jnp.ones_like(vt)], axis=1)
        visible = None if diag is None else (kc + diag * (tk // CHUNK)) <= qc
        for mp in range(2):
            s = lax.dot_general(qs[mp], kt, NT_DIMS, preferred_element_type=F32)
            if fixed_stabiliser:
                p = jnp.exp2(s - shift)
                if visible is not None:
                    p = jnp.where(visible, p, 0.0)
                acc_ref[mp] += jnp.dot(p.astype(BF16), v_ones, preferred_element_type=F32)
            else:
                if visible is not None:
                    s = jnp.where(visible, s, NEG_INF)
                m_prev = m_ref[mp]
                m_new = jnp.maximum(m_prev, jnp.max(s, axis=-1, keepdims=True))
                p = jnp.exp2(s - m_new)
                acc_ref[mp] = (jnp.exp2(m_prev - m_new) * acc_ref[mp]
                               + jnp.dot(p.astype(BF16), v_ones, preferred_element_type=F32))
                m_ref[mp] = m_new

    def body(j, carry):
        tile(j, None)
        return carry

    lax.fori_loop(0, i * n_sub, body, 0)
    for d in range(n_sub):
        tile(i * n_sub + d, d)

    lam = (jnp.exp(jnp.sum(lq1_ref[...] * lk1_ref[...], axis=-1, keepdims=True))
           - jnp.exp(jnp.sum(lq2_ref[...] * lk2_ref[...], axis=-1, keepdims=True)) + lambda_init)
    o = (acc_ref[0, :, :HEAD_W] / acc_ref[0, :, HEAD_W:]
         - lam * (acc_ref[1, :, :HEAD_W] / acc_ref[1, :, HEAD_W:]))
    o_ref[...] = (_rms(o) * hg_ref[...] * (1.0 - lambda_init)).astype(o_ref.dtype)


def _diff_attn(dq, dk, proj, qg, kg, lq1, lk1, lq2, lk2, hg, batch, seq, tq, tk, lambda_init):
    n = dq.shape[0]
    nq = seq // tq
    bound = (1.01 * DIFF_DH * DIFF_Q_SCALE * jnp.max(jnp.abs(qg)) * jnp.max(jnp.abs(kg))).reshape(1, 1)
    vec64 = pl.BlockSpec((1, DIFF_DH), lambda b, h, i: (0, 0))

    def call(fixed_stabiliser):
        stat_scratch = [] if fixed_stabiliser else [pltpu.VMEM((2, tq, 1), F32)]
        return pl.pallas_call(
            functools.partial(_diff_attn_kernel, tq=tq, tk=tk, lambda_init=lambda_init,
                              fixed_stabiliser=fixed_stabiliser),
            grid=(batch, HEADS, nq),
            in_specs=[pl.BlockSpec(memory_space=pltpu.SMEM),
                      pl.BlockSpec((tq, HEAD_W), lambda b, h, i: (b * nq + i, h)),
                      pl.BlockSpec((seq, HEAD_W), lambda b, h, i: (b, h)),
                      pl.BlockSpec((seq, HEAD_W), lambda b, h, i: (b, P_DIF_V * HEADS + h)),
                      vec64, vec64, vec64, vec64,
                      pl.BlockSpec((1, HEAD_W), lambda b, h, i: (0, 0))],
            out_specs=pl.BlockSpec((tq, HEAD_W), lambda b, h, i: (b * nq + i, h)),
            out_shape=jax.ShapeDtypeStruct((n, BRANCH_W), BF16),
            scratch_shapes=[pltpu.VMEM((2, tq, 2 * HEAD_W), F32)] + stat_scratch,
            compiler_params=_params("parallel", "parallel", "arbitrary"),
            name="diff_attn_fixed" if fixed_stabiliser else "diff_attn_online",
        )(bound, dq, dk, proj, lq1, lk1, lq2, lk2, hg)

    return lax.cond(bound[0, 0] * 2.0 <= MAX_SCORE_RANGE_LOG2, lambda: call(True), lambda: call(False))


def _mem_kv_kernel(mem_ref, g_ref, w_ref, kg_ref, k_ref, v_ref):
    h = (_rms(mem_ref[...]) * g_ref[...]).astype(BF16)
    kv = jnp.dot(h, w_ref[...], preferred_element_type=F32)
    for hd in range(HEADS):
        sl = slice(hd * HEAD_W, (hd + 1) * HEAD_W)
        k_ref[:, sl] = (_rms(kv[:, sl]) * kg_ref[...]).astype(BF16)
    v_ref[...] = kv[:, BRANCH_W:].astype(BF16)


def _mem_kv(mem2, g, w_bf, kg, mem_len):
    n, d = mem2.shape
    out = pl.BlockSpec((mem_len, BRANCH_W), lambda i: (i, 0))
    return pl.pallas_call(
        _mem_kv_kernel,
        grid=(n // mem_len,),
        in_specs=[pl.BlockSpec((mem_len, d), lambda i: (i, 0)),
                  pl.BlockSpec((1, d), lambda i: (0, 0)),
                  pl.BlockSpec((d, 2 * BRANCH_W), lambda i: (0, 0)),
                  pl.BlockSpec((1, HEAD_W), lambda i: (0, 0))],
        out_specs=[out, out],
        out_shape=[jax.ShapeDtypeStruct((n, BRANCH_W), BF16)] * 2,
        compiler_params=_params("parallel"),
        name="mem_kv",
    )(mem2, g, w_bf, kg)


def _merge_kernel(x_ref, y_ref, sg_ref, ret_ref, rg_ref, dif_ref, dg_ref, mq_ref, mg_ref, mk_ref, mv_ref,
                  mrg0_ref, mrg1_ref, mrg2_ref, mrg3_ref, wglu_ref, bglu_ref, mqg_ref, wbr_ref, bm_ref, wout_ref,
                  o_ref):
    z = jax.nn.gelu(y_ref[...])
    glu = jnp.dot(wglu_ref[...], z.astype(BF16), preferred_element_type=F32) + bglu_ref[...]
    a_t = z * _sigmoid(glu) * _silu(sg_ref[...].astype(F32))
    branches = [lax.dot_general(a_t.astype(BF16), wbr_ref[0], TN_DIMS, preferred_element_type=F32)]
    b_out = ret_ref[...].astype(F32) * _silu(rg_ref[...].astype(F32))
    branches.append(jnp.dot(b_out.astype(BF16), wbr_ref[1], preferred_element_type=F32))
    c_out = dif_ref[...].astype(F32) * _silu(dg_ref[...].astype(F32))
    branches.append(jnp.dot(c_out.astype(BF16), wbr_ref[2], preferred_element_type=F32))
    heads = []
    for hd in range(HEADS):
        sl = slice(hd * HEAD_W, (hd + 1) * HEAD_W)
        qh = (_rms(mq_ref[:, sl].astype(F32)) * mqg_ref[...] * (HEAD_W ** -0.5)).astype(BF16)
        s = lax.dot_general(qh, mk_ref[:, sl], NT_DIMS, preferred_element_type=F32)
        p = jnp.exp(s - jnp.max(s, axis=-1, keepdims=True))
        o = jnp.dot(p.astype(BF16), mv_ref[:, sl], preferred_element_type=F32)
        heads.append(o / jnp.sum(p, axis=-1, keepdims=True))
    m_out = jnp.concatenate(heads, axis=-1) * _silu(mg_ref[...].astype(F32))
    branches.append(jnp.dot(m_out.astype(BF16), wbr_ref[3], preferred_element_type=F32))
    merged = None
    for nb, mrg_ref in enumerate((mrg0_ref, mrg1_ref, mrg2_ref, mrg3_ref)):
        gate = _sigmoid(mrg_ref[...].astype(F32) + bm_ref[nb:nb + 1, :])
        merged = gate * branches[nb] if merged is None else merged + gate * branches[nb]
    o_ref[...] = x_ref[...] + jnp.dot(merged.astype(BF16), wout_ref[...], preferred_element_type=F32)


def _merge(x2, y_t, u_t, proj, ret_o, dif_o, mk, mv, wglu_t, bglu, mqg, wbr, bm, wout, seq, mem_len, tm):
    n, d = x2.shape

    def piece(p):
        return pl.BlockSpec((tm, BRANCH_W), lambda i: (i, p))

    def gate_piece(nb):
        return pl.BlockSpec((tm, D_MODEL), lambda i: (i, P_MERGE * BRANCH_W // D_MODEL + nb))

    tok = pl.BlockSpec((tm, BRANCH_W), lambda i: (i, 0))
    mem = pl.BlockSpec((mem_len, BRANCH_W), lambda i: ((i * tm) // seq, 0))
    full = lambda a: pl.BlockSpec(a.shape, lambda i: (0,) * a.ndim)
    return pl.pallas_call(
        _merge_kernel,
        grid=(n // tm,),
        in_specs=[pl.BlockSpec((tm, d), lambda i: (i, 0)),
                  pl.BlockSpec((BRANCH_W, tm), lambda i: (0, i)),
                  pl.BlockSpec((BRANCH_W, tm), lambda i: (1, i)),
                  tok, piece(P_RET_G), tok, piece(P_DIF_G), piece(P_MEM_Q), piece(P_MEM_G), mem, mem,
                  gate_piece(0), gate_piece(1), gate_piece(2), gate_piece(3),
                  full(wglu_t), full(bglu), full(mqg), full(wbr), full(bm), full(wout)],
        out_specs=pl.BlockSpec((tm, d), lambda i: (i, 0)),
        out_shape=jax.ShapeDtypeStruct((n, d), F32),
        compiler_params=_params("parallel"),
        name="merge",
    )(x2, y_t, u_t, ret_o, proj, dif_o, proj, proj, proj, mk, mv, proj, proj, proj, proj,
      wglu_t, bglu, mqg, wbr, bm, wout)


def _tile(n, pref):
    return pref if n % pref == 0 else n


def kernel(x, mem, norm_g, w_in, ssm_lambda_re, ssm_lambda_im, ssm_log_dt, ssm_b_re, ssm_b_im, ssm_c_re,
           ssm_c_im, ssm_d, ssm_w_glu, ssm_b_glu, diff_q_norm_g, diff_k_norm_g, diff_lambda_q1,
           diff_lambda_k1, diff_lambda_q2, diff_lambda_k2, diff_head_norm_g, mem_norm_g, w_mem_kv,
           mem_q_norm_g, mem_k_norm_g, w_branch, b_merge, w_out):
    batch, seq, d = x.shape
    mem_len = mem.shape[1]
    depth = w_in.shape[0]
    n = batch * seq
    assert d == D_MODEL and seq % S5_CHUNK == 0

    diff_inv = ROPE_THETA ** (-jnp.arange(0, DIFF_DH, 2, dtype=F32) / DIFF_DH)
    ret_inv = 1.0 / (ROPE_THETA ** jnp.linspace(0.0, 1.0, HEAD_W // 2, dtype=F32))
    inv = jnp.stack([jnp.tile(diff_inv, LANES // diff_inv.shape[0]),
                     jnp.tile(ret_inv, LANES // ret_inv.shape[0])])[:, None, :]
    lane = np.arange(LANES)
    sgn = np.stack([np.where(lane % DIFF_DH < DIFF_DH // 2, -1.0, 1.0),
                    np.where(lane < HEAD_W // 2, -1.0, 1.0)])[:, None, :]
    cos_t, sin_t = _rope_tables(seq, inv, jnp.asarray(sgn, F32), _tile(seq, 1024))

    x2 = x.reshape(n, d)
    mem2 = mem.reshape(batch * mem_len, d)
    tile2 = lambda v: jnp.tile(v, LANES // v.shape[0])[None, :]
    for l in range(depth):
        lambda_init = 0.8 - 0.6 * math.exp(-0.3 * l)
        g = norm_g[l][None, :]
        w = w_in[l]
        proj = _in_proj(x2, g, w[:, 2 * BRANCH_W:].astype(BF16), _tile(n, 2048), 1024)
        u_t = _in_proj_t(x2, g, w[:, :2 * BRANCH_W].T.astype(BF16), _tile(n, 1024))

        m_op, p_op, q_op, dec = _s5_prep(ssm_lambda_re[l], ssm_lambda_im[l], ssm_log_dt[l], ssm_b_re[l],
                                         ssm_b_im[l], ssm_c_re[l], ssm_c_im[l])
        y_t = _s5_conv(u_t, ssm_d[l], m_op, p_op, q_op, dec, batch, seq)

        rq, rk, dq, dk = _qk_prep(proj, cos_t, sin_t, tile2(diff_q_norm_g[l]), tile2(diff_k_norm_g[l]),
                                  seq, _tile(seq, 1024))
        ret_o = _retention(rq, rk, proj, batch, seq, _tile(seq, 256))
        dif_o = _diff_attn(dq, dk, proj, diff_q_norm_g[l], diff_k_norm_g[l], diff_lambda_q1[l][None],
                           diff_lambda_k1[l][None], diff_lambda_q2[l][None], diff_lambda_k2[l][None],
                           diff_head_norm_g[l][None], batch, seq, _tile(seq, 1024), _tile(seq, 512), lambda_init)
        mk, mv = _mem_kv(mem2, mem_norm_g[l][None, :], w_mem_kv[l].astype(BF16), mem_k_norm_g[l][None], mem_len)
        x2 = _merge(x2, y_t, u_t, proj, ret_o, dif_o, mk, mv, ssm_w_glu[l].T.astype(BF16),
                    ssm_b_glu[l][:, None], mem_q_norm_g[l][None], w_branch[l].astype(BF16), b_merge[l],
                    w_out[l].astype(BF16), seq, mem_len, _tile(seq, 512))
    return x2.reshape(batch, seq, d)
```

```python
import functools
import math

import numpy as np
import jax
import jax.numpy as jnp
from jax import lax
from jax.experimental import pallas as pl
from jax.experimental.pallas import tpu as pltpu

F32 = jnp.float32
BF16 = jnp.bfloat16

D_MODEL = 1024
BRANCH_W = D_MODEL // 2
N_BRANCH = 4
N_IN_PIECES = 12
CHUNK = 64
SSM_GROUP = 16
SSM_GROUPS = BRANCH_W // SSM_GROUP
SSM_STATE = 64
HEADS = 4
HEAD_W = BRANCH_W // HEADS
DIFF_DH = HEAD_W // 2
ROPE_THETA = 10000.0
EPS = 1e-6
NEG_INF = -1e30
GATE_FOLD = 0.5
IN_GATE_PIECES = (1, 5, 9, 11)
LOG2_E = math.log2(math.e)
DIFF_Q_SCALE = DIFF_DH ** -0.5 * LOG2_E
MAX_SCORE_RANGE_LOG2 = 100.0

LANES = 128
S5_CHUNK = LANES
S5_FLAT = SSM_GROUP * S5_CHUNK
PROJ_COLS = (N_IN_PIECES - 2) * BRANCH_W + N_BRANCH * D_MODEL
P_RET_Q, P_RET_K, P_RET_V, P_RET_G, P_DIF_Q, P_DIF_K, P_DIF_V, P_DIF_G, P_MEM_Q, P_MEM_G, P_MERGE = range(11)

VMEM_LIMIT = 56 * 1024 * 1024

NT_DIMS = (((1,), (1,)), ((), ()))
TN_DIMS = (((0,), (0,)), ((), ()))


def _params(*sem):
    return pltpu.CompilerParams(dimension_semantics=sem, vmem_limit_bytes=VMEM_LIMIT)


def _rms(x, eps=EPS):
    return x * lax.rsqrt(jnp.mean(x * x, axis=-1, keepdims=True) + eps)


def _sigmoid(x):
    return 0.5 * jnp.tanh(0.5 * x) + 0.5


def _silu_of_half(xh):
    return xh * (1.0 + jnp.tanh(xh))


def _in_proj_kernel(x_ref, g_ref, w_ref, o_ref, h_ref):
    @pl.when(pl.program_id(1) == 0)
    def _():
        h_ref[...] = (_rms(x_ref[...]) * g_ref[...]).astype(BF16)

    o_ref[...] = jnp.dot(h_ref[...], w_ref[...], preferred_element_type=F32).astype(o_ref.dtype)


def _in_proj(x2, g, w_bf, tm, tn):
    n, d = x2.shape
    cols = w_bf.shape[1]
    return pl.pallas_call(
        _in_proj_kernel,
        grid=(n // tm, cols // tn),
        in_specs=[pl.BlockSpec((tm, d), lambda i, j: (i, 0)),
                  pl.BlockSpec((1, d), lambda i, j: (0, 0)),
                  pl.BlockSpec((d, tn), lambda i, j: (0, j))],
        out_specs=pl.BlockSpec((tm, tn), lambda i, j: (i, j)),
        out_shape=jax.ShapeDtypeStruct((n, cols), BF16),
        scratch_shapes=[pltpu.VMEM((tm, d), BF16)],
        compiler_params=_params("parallel", "arbitrary"),
        name="in_proj",
    )(x2, g, w_bf)


def _in_proj_t_kernel(x_ref, g_ref, wt_ref, o_ref):
    h = (_rms(x_ref[...]) * g_ref[...]).astype(BF16)
    o_ref[...] = lax.dot_general(wt_ref[...], h, NT_DIMS, preferred_element_type=F32).astype(o_ref.dtype)


def _in_proj_t(x2, g, wt_bf, tm):
    n, d = x2.shape
    rows = wt_bf.shape[0]
    return pl.pallas_call(
        _in_proj_t_kernel,
        grid=(n // tm,),
        in_specs=[pl.BlockSpec((tm, d), lambda i: (i, 0)),
                  pl.BlockSpec((1, d), lambda i: (0, 0)),
                  pl.BlockSpec((rows, d), lambda i: (0, 0))],
        out_specs=pl.BlockSpec((rows, tm), lambda i: (0, i)),
        out_shape=jax.ShapeDtypeStruct((rows, n), BF16),
        compiler_params=_params("parallel"),
        name="in_proj_t",
    )(x2, g, wt_bf)


def _rope_table_kernel(inv_ref, sgn_ref, cos_ref, sin_ref):
    ts = cos_ref.shape[0]
    pos = (lax.broadcasted_iota(jnp.int32, (ts, LANES), 0) + pl.program_id(1) * ts).astype(F32)
    ang = pos * inv_ref[...]
    cos_ref[...] = jnp.cos(ang)
    sin_ref[...] = jnp.sin(ang) * sgn_ref[...]


def _rope_tables(seq, inv, sgn, ts):
    spec_in = pl.BlockSpec((None, 1, LANES), lambda t, i: (t, 0, 0))
    spec_out = pl.BlockSpec((None, ts, LANES), lambda t, i: (t, i, 0))
    return pl.pallas_call(
        _rope_table_kernel,
        grid=(2, seq // ts),
        in_specs=[spec_in, spec_in],
        out_specs=[spec_out, spec_out],
        out_shape=[jax.ShapeDtypeStruct((2, seq, LANES), F32)] * 2,
        compiler_params=_params("parallel", "parallel"),
        name="rope_tables",
    )(inv, sgn)


def _qk_prep_kernel(rq_ref, rk_ref, dq_ref, dk_ref, cos_ref, sin_ref, qg_ref, kg_ref,
                    orq_ref, ork_ref, odq_ref, odk_ref):
    cos_d, sin_d = cos_ref[0], sin_ref[0]
    cos_r, sin_r = cos_ref[1], sin_ref[1]
    ts = cos_d.shape[0]
    lane = lax.broadcasted_iota(jnp.int32, (ts, LANES), 1)
    low_map = lane < DIFF_DH
    first_half = (lane % DIFF_DH) < (DIFF_DH // 2)

    def seg_norm(x):
        x2 = x * x
        lo = jnp.sum(jnp.where(low_map, x2, 0.0), axis=-1, keepdims=True)
        hi = jnp.sum(jnp.where(low_map, 0.0, x2), axis=-1, keepdims=True)
        ms = jnp.where(low_map, lo, hi) * (1.0 / DIFF_DH)
        return x * lax.rsqrt(ms + EPS)

    def rope_d(x):
        partner = jnp.where(first_half, pltpu.roll(x, LANES - DIFF_DH // 2, 1),
                            pltpu.roll(x, DIFF_DH // 2, 1))
        return x * cos_d + partner * sin_d

    def rope_r(x):
        return x * cos_r + pltpu.roll(x, HEAD_W // 2, 1) * sin_r

    for h in range(HEADS):
        sl = slice(h * HEAD_W, (h + 1) * HEAD_W)
        orq_ref[:, sl] = rope_r(rq_ref[:, sl].astype(F32)).astype(BF16)
        ork_ref[:, sl] = (rope_r(rk_ref[:, sl].astype(F32)) * (HEAD_W ** -0.5)).astype(BF16)
        q = seg_norm(dq_ref[:, sl].astype(F32)) * qg_ref[...]
        odq_ref[:, sl] = (rope_d(q) * DIFF_Q_SCALE).astype(BF16)
        k = seg_norm(dk_ref[:, sl].astype(F32)) * kg_ref[...]
        odk_ref[:, sl] = rope_d(k).astype(BF16)


def _qk_prep(proj, cos_t, sin_t, qg, kg, seq, ts):
    n = proj.shape[0]
    nsb = seq // ts

    def piece(p):
        return pl.BlockSpec((ts, BRANCH_W), lambda i: (i, p))

    tab = pl.BlockSpec((2, ts, LANES), lambda i: (0, i % nsb, 0))
    vec = pl.BlockSpec((1, LANES), lambda i: (0, 0))
    out = pl.BlockSpec((ts, BRANCH_W), lambda i: (i, 0))
    return pl.pallas_call(
        _qk_prep_kernel,
        grid=(n // ts,),
        in_specs=[piece(P_RET_Q), piece(P_RET_K), piece(P_DIF_Q), piece(P_DIF_K), tab, tab, vec, vec],
        out_specs=[out] * 4,
        out_shape=[jax.ShapeDtypeStruct((n, BRANCH_W), BF16)] * 4,
        compiler_params=_params("parallel"),
        name="qk_prep",
    )(proj, proj, proj, proj, cos_t, sin_t, qg, kg)


def _s5_prep_kernel(lam_c_ref, lam_r_ref, b_ref, bt_ref, c_ref, ct_ref,
                    m_ref, p_ref, q_ref, dec_ref, kern_ref):
    T = S5_CHUNK

    def disc(lam_re, lam_im, log_dt):
        lr = jnp.minimum(lam_re, -1e-4)
        li = lam_im
        dt = jnp.exp(log_dt)
        mag = jnp.exp(lr * dt)
        ab_re = mag * jnp.cos(li * dt)
        ab_im = mag * jnp.sin(li * dt)
        nr = ab_re - 1.0
        ni = ab_im
        den = lr * lr + li * li
        f_re = (nr * lr + ni * li) / den
        f_im = (ni * lr - nr * li) / den
        return lr * dt, li * dt, f_re, f_im

    def powers(ldt_re, ldt_im, k):
        mag = jnp.exp(k * ldt_re)
        return mag * jnp.cos(k * ldt_im), mag * jnp.sin(k * ldt_im)

    ldr, ldi, f_re, f_im = disc(lam_c_ref[0], lam_c_ref[1], lam_c_ref[2])
    b_re, b_im = b_ref[0], b_ref[1]
    bb_re = f_re * b_re - f_im * b_im
    bb_im = f_re * b_im + f_im * b_re
    k_row = lax.broadcasted_iota(jnp.int32, (1, T), 1).astype(F32)
    pw_re, pw_im = powers(ldr, ldi, k_row)
    pw1_re, pw1_im = powers(ldr, ldi, k_row + 1.0)
    c_re, c_im = c_ref[0], c_ref[1]
    ct_re, ct_im = ct_ref[0], ct_ref[1]

    for c in range(SSM_GROUP):
        w_re = pw_re * bb_re[:, c:c + 1] - pw_im * bb_im[:, c:c + 1]
        w_im = pw_re * bb_im[:, c:c + 1] + pw_im * bb_re[:, c:c + 1]
        kern = (jnp.dot(c_re, w_re, preferred_element_type=F32, precision=lax.Precision.HIGHEST)
                - jnp.dot(c_im, w_im, preferred_element_type=F32, precision=lax.Precision.HIGHEST))
        for co in range(SSM_GROUP):
            kern_ref[c, :, co * T:(co + 1) * T] = kern[co:co + 1, :]
        q_ref[0, :, c * T:(c + 1) * T] = (ct_re[:, c:c + 1] * pw1_re - ct_im[:, c:c + 1] * pw1_im).astype(q_ref.dtype)
        q_ref[1, :, c * T:(c + 1) * T] = (-ct_re[:, c:c + 1] * pw1_im - ct_im[:, c:c + 1] * pw1_re).astype(q_ref.dtype)

    jj = lax.broadcasted_iota(jnp.int32, (T, S5_FLAT), 0)
    tt = lax.broadcasted_iota(jnp.int32, (T, S5_FLAT), 1) % T
    causal = tt >= jj

    def toeplitz_rows(c, carry):
        row = jnp.broadcast_to(kern_ref[c], (T, S5_FLAT))
        toep = pltpu.roll(row, 0, 1, stride=1, stride_axis=0)
        m_ref[pl.ds(pl.multiple_of(c * T, T), T), :] = jnp.where(causal, toep, 0.0).astype(m_ref.dtype)
        return carry

    lax.fori_loop(0, SSM_GROUP, toeplitz_rows, 0)

    ldr_r, ldi_r, fr_r, fi_r = disc(lam_r_ref[0], lam_r_ref[1], lam_r_ref[2])
    bt_re, bt_im = bt_ref[0], bt_ref[1]
    bbt_re = fr_r * bt_re - fi_r * bt_im
    bbt_im = fr_r * bt_im + fi_r * bt_re
    k_col = (T - 1.0) - lax.broadcasted_iota(jnp.int32, (T, 1), 0).astype(F32)
    pr_re, pr_im = powers(ldr_r, ldi_r, k_col)
    for c in range(SSM_GROUP):
        p_ref[0, c * T:(c + 1) * T, :] = (pr_re * bbt_re[c:c + 1, :] - pr_im * bbt_im[c:c + 1, :]).astype(p_ref.dtype)
        p_ref[1, c * T:(c + 1) * T, :] = (pr_re * bbt_im[c:c + 1, :] + pr_im * bbt_re[c:c + 1, :]).astype(p_ref.dtype)
    d_re, d_im = powers(ldr_r, ldi_r, float(T))
    dec_ref[0] = d_re
    dec_ref[1] = d_im


def _s5_prep(lam_re, lam_im, log_dt, b_re, b_im, c_re, c_im):
    g, p = lam_re.shape
    ldt = jnp.broadcast_to(log_dt[:, None], (g, p))
    lam = jnp.stack([lam_re, lam_im, ldt], axis=1)
    lam_c = lam[..., None]
    lam_r = lam[:, :, None, :]
    b = jnp.stack([b_re, b_im], axis=1)
    bt = jnp.swapaxes(b, 2, 3)
    c = jnp.stack([c_re, c_im], axis=1)
    ct = jnp.swapaxes(c, 2, 3)

    def spec(shape):
        return pl.BlockSpec((None,) + shape, lambda i: (i,) + (0,) * len(shape))

    return pl.pallas_call(
        _s5_prep_kernel,
        grid=(g,),
        in_specs=[spec((3, p, 1)), spec((3, 1, p)), spec((2, p, SSM_GROUP)), spec((2, SSM_GROUP, p)),
                  spec((2, SSM_GROUP, p)), spec((2, p, SSM_GROUP))],
        out_specs=[spec((S5_FLAT, S5_FLAT)), spec((2, S5_FLAT, p)), spec((2, p, S5_FLAT)), spec((2, 1, p))],
        out_shape=[jax.ShapeDtypeStruct((g, S5_FLAT, S5_FLAT), BF16),
                   jax.ShapeDtypeStruct((g, 2, S5_FLAT, p), BF16),
                   jax.ShapeDtypeStruct((g, 2, p, S5_FLAT), BF16),
                   jax.ShapeDtypeStruct((g, 2, 1, p), F32)],
        scratch_shapes=[pltpu.VMEM((SSM_GROUP, 1, S5_FLAT), F32)],
        compiler_params=_params("parallel"),
        name="s5_prep",
    )(lam_c, lam_r, b, bt, c, ct)


def _s5_conv_kernel(d_ref, u_ref, m_ref, p_ref, q_ref, dec_ref, y_ref, v_ref, s_ref, *, batch, n_chunks):
    g = pl.program_id(0)
    T = S5_CHUNK
    u_flat = jnp.concatenate([u_ref[c] for c in range(SSM_GROUP)], axis=-1)
    rows = batch * n_chunks
    for ri in range(2):
        v_ref[ri] = jnp.dot(u_flat, p_ref[ri], preferred_element_type=F32).reshape(batch, n_chunks, SSM_STATE)
    dr, di = dec_ref[0], dec_ref[1]

    def body(ch, carry):
        sr, si = carry
        s_ref[0, :, pl.ds(ch, 1), :] = sr
        s_ref[1, :, pl.ds(ch, 1), :] = si
        vr = v_ref[0, :, pl.ds(ch, 1), :]
        vi = v_ref[1, :, pl.ds(ch, 1), :]
        return dr * sr - di * si + vr, dr * si + di * sr + vi

    zero = jnp.zeros((batch, 1, SSM_STATE), F32)
    lax.fori_loop(0, n_chunks, body, (zero, zero))
    y = jnp.dot(u_flat, m_ref[...], preferred_element_type=F32)
    for ri in range(2):
        s_start = s_ref[ri].reshape(rows, SSM_STATE).astype(BF16)
        y += jnp.dot(s_start, q_ref[ri], preferred_element_type=F32)
    for c in range(SSM_GROUP):
        y_ref[c] = y[:, c * T:(c + 1) * T] + d_ref[g * SSM_GROUP + c] * u_ref[c].astype(F32)


def _s5_conv(u_t, d_skip, m, p, q, dec, batch, seq):
    n = u_t.shape[1]
    r = n // S5_CHUNK
    u3 = u_t.reshape(u_t.shape[0], r, S5_CHUNK)
    g = SSM_GROUPS

    def spec(shape):
        return pl.BlockSpec((None,) + shape, lambda i: (i,) + (0,) * len(shape))

    y3 = pl.pallas_call(
        functools.partial(_s5_conv_kernel, batch=batch, n_chunks=seq // S5_CHUNK),
        grid=(g,),
        in_specs=[pl.BlockSpec(memory_space=pltpu.SMEM),
                  pl.BlockSpec((SSM_GROUP, r, S5_CHUNK), lambda i: (i, 0, 0)),
                  spec((S5_FLAT, S5_FLAT)), spec((2, S5_FLAT, SSM_STATE)), spec((2, SSM_STATE, S5_FLAT)),
                  spec((2, 1, SSM_STATE))],
        out_specs=pl.BlockSpec((SSM_GROUP, r, S5_CHUNK), lambda i: (i, 0, 0)),
        out_shape=jax.ShapeDtypeStruct((BRANCH_W, r, S5_CHUNK), F32),
        scratch_shapes=[pltpu.VMEM((2, batch, seq // S5_CHUNK, SSM_STATE), F32)] * 2,
        compiler_params=_params("parallel"),
        name="s5_conv",
    )(d_skip, u3, m, p, q, dec)
    return y3.reshape(BRANCH_W, n)


def _retention_kernel(q_ref, k_ref, v_ref, mask_ref, qd_ref, kd_ref, cd_ref, o_ref, st_ref):
    @pl.when(pl.program_id(1) == 0)
    def _():
        st_ref[...] = jnp.zeros_like(st_ref)

    for h in range(HEADS):
        sl = slice(h * HEAD_W, (h + 1) * HEAD_W)
        q, k, v = q_ref[:, sl], k_ref[:, sl], v_ref[:, sl]
        scores = lax.dot_general(q, k, NT_DIMS, preferred_element_type=F32) * mask_ref[h]
        st = st_ref[h]
        o = (jnp.dot(scores.astype(BF16), v, preferred_element_type=F32)
             + jnp.dot(q, st.astype(BF16), preferred_element_type=F32) * qd_ref[h])
        kd = (k.astype(F32) * kd_ref[h]).astype(BF16)
        st_ref[h] = st * cd_ref[h] + lax.dot_general(kd, v, TN_DIMS, preferred_element_type=F32)
        o_ref[:, sl] = _rms(o).astype(o_ref.dtype)


def _retention_consts(blk):
    log_gamma = np.log1p(-(2.0 ** (-5.0 - np.arange(HEADS, dtype=np.float64))))
    idx = np.arange(blk, dtype=np.float64)
    diff = idx[:, None] - idx[None, :]
    same = (np.arange(blk)[:, None] // CHUNK) == (np.arange(blk)[None, :] // CHUNK)
    past = (np.arange(blk)[None, :] // CHUNK) < (np.arange(blk)[:, None] // CHUNK)
    expo = np.where(same, np.abs(diff), diff)
    mask = np.where(same | past, np.exp(expo[None] * log_gamma[:, None, None]), 0.0)
    q_dec = np.exp((idx + 1.0)[None, :] * log_gamma[:, None])
    k_dec = np.exp((blk - 1.0 - idx)[None, :] * log_gamma[:, None])
    c_dec = np.exp(blk * log_gamma)
    bc = lambda a: np.broadcast_to(a[:, :, None], (HEADS, blk, HEAD_W))
    return (jnp.asarray(mask, F32), jnp.asarray(bc(q_dec), F32), jnp.asarray(bc(k_dec), F32),
            jnp.asarray(np.broadcast_to(c_dec[:, None, None], (HEADS, 1, HEAD_W)), F32))


def _retention(rq, rk, proj, batch, seq, blk):
    n = rq.shape[0]
    nb = seq // blk
    mask, q_dec, k_dec, c_dec = _retention_consts(blk)
    tok = pl.BlockSpec((blk, BRANCH_W), lambda b, l: (b * nb + l, 0))
    full = lambda a: pl.BlockSpec(a.shape, lambda b, l: (0,) * a.ndim)
    return pl.pallas_call(
        _retention_kernel,
        grid=(batch, nb),
        in_specs=[tok, tok, pl.BlockSpec((blk, BRANCH_W), lambda b, l: (b * nb + l, P_RET_V)),
                  full(mask), full(q_dec), full(k_dec), full(c_dec)],
        out_specs=tok,
        out_shape=jax.ShapeDtypeStruct((n, BRANCH_W), BF16),
        scratch_shapes=[pltpu.VMEM((HEADS, HEAD_W, HEAD_W), F32)],
        compiler_params=_params("parallel", "arbitrary"),
        name="retention",
    )(rq, rk, proj, mask, q_dec, k_dec, c_dec)


def _diff_attn_kernel(bound_ref, q_ref, k_ref, v_ref, lq1_ref, lk1_ref, lq2_ref, lk2_ref, hg_ref, o_ref,
                      acc_ref, *stat_refs, tq, tk, lambda_init, fixed_stabiliser):
    i = pl.program_id(2)
    n_sub = tq // tk
    q = q_ref[...]
    lane = lax.broadcasted_iota(jnp.int32, (tq, HEAD_W), 1)
    qs = (jnp.where(lane < DIFF_DH, q, jnp.zeros_like(q)), jnp.where(lane < DIFF_DH, jnp.zeros_like(q), q))
    acc_ref[...] = jnp.zeros_like(acc_ref)
    if fixed_stabiliser:
        shift = bound_ref[0, 0]
    else:
        m_ref, = stat_refs
        m_ref[...] = jnp.full_like(m_ref, NEG_INF)
    qc = lax.broadcasted_iota(jnp.int32, (tq, tk), 0) // CHUNK
    kc = lax.broadcasted_iota(jnp.int32, (tq, tk), 1) // CHUNK

    def tile(j, diag):
        rows = pl.ds(pl.multiple_of(j * tk, tk), tk)
        kt = k_ref[rows, :]
        vt = v_ref[rows, :]
        v_ones = jnp.concatenate([vt, jnp.ones_like(vt)], axis=1)
        r0 = 0 if diag is None else diag * tk
        visible = None if diag is None else ((kc + diag * (tk // CHUNK)) <= qc)[r0:]
        for mp in range(2):
            s = lax.dot_general(qs[mp][r0:], kt, NT_DIMS, preferred_element_type=F32)
            if fixed_stabiliser:
                p = jnp.exp2(s - shift)
                if visible is not None:
                    p = jnp.where(visible, p, 0.0)
                acc_ref[mp, r0:] += jnp.dot(p.astype(BF16), v_ones, preferred_element_type=F32)
            else:
                if visible is not None:
                    s = jnp.where(visible, s, NEG_INF)
                m_prev = m_ref[mp, r0:]
                m_new = jnp.maximum(m_prev, jnp.max(s, axis=-1, keepdims=True))
                p = jnp.exp2(s - m_new)
                acc_ref[mp, r0:] = (jnp.exp2(m_prev - m_new) * acc_ref[mp, r0:]
                                    + jnp.dot(p.astype(BF16), v_ones, preferred_element_type=F32))
                m_ref[mp, r0:] = m_new

    def body(jj, carry):
        for u in range(n_sub):
            tile(jj * n_sub + u, None)
        return carry

    lax.fori_loop(0, i, body, 0)
    for d in range(n_sub):
        tile(i * n_sub + d, d)

    lam = (jnp.exp(jnp.sum(lq1_ref[...] * lk1_ref[...], axis=-1, keepdims=True))
           - jnp.exp(jnp.sum(lq2_ref[...] * lk2_ref[...], axis=-1, keepdims=True)) + lambda_init)
    o = (acc_ref[0, :, :HEAD_W] / acc_ref[0, :, HEAD_W:]
         - lam * (acc_ref[1, :, :HEAD_W] / acc_ref[1, :, HEAD_W:]))
    o_ref[...] = (_rms(o) * hg_ref[...] * (1.0 - lambda_init)).astype(o_ref.dtype)


def _diff_attn(dq, dk, proj, qg, kg, lq1, lk1, lq2, lk2, hg, batch, seq, tq, tk, lambda_init):
    n = dq.shape[0]
    nq = seq // tq
    bound = (1.01 * DIFF_DH * DIFF_Q_SCALE * jnp.max(jnp.abs(qg)) * jnp.max(jnp.abs(kg))).reshape(1, 1)
    vec64 = pl.BlockSpec((1, DIFF_DH), lambda b, h, i: (0, 0))

    def call(fixed_stabiliser):
        stat_scratch = [] if fixed_stabiliser else [pltpu.VMEM((2, tq, 1), F32)]
        return pl.pallas_call(
            functools.partial(_diff_attn_kernel, tq=tq, tk=tk, lambda_init=lambda_init,
                              fixed_stabiliser=fixed_stabiliser),
            grid=(batch, HEADS, nq),
            in_specs=[pl.BlockSpec(memory_space=pltpu.SMEM),
                      pl.BlockSpec((tq, HEAD_W), lambda b, h, i: (b * nq + i, h)),
                      pl.BlockSpec((seq, HEAD_W), lambda b, h, i: (b, h)),
                      pl.BlockSpec((seq, HEAD_W), lambda b, h, i: (b, P_DIF_V * HEADS + h)),
                      vec64, vec64, vec64, vec64,
                      pl.BlockSpec((1, HEAD_W), lambda b, h, i: (0, 0))],
            out_specs=pl.BlockSpec((tq, HEAD_W), lambda b, h, i: (b * nq + i, h)),
            out_shape=jax.ShapeDtypeStruct((n, BRANCH_W), BF16),
            scratch_shapes=[pltpu.VMEM((2, tq, 2 * HEAD_W), F32)] + stat_scratch,
            compiler_params=_params("parallel", "parallel", "arbitrary"),
            name="diff_attn_fixed" if fixed_stabiliser else "diff_attn_online",
        )(bound, dq, dk, proj, lq1, lk1, lq2, lk2, hg)

    return lax.cond(bound[0, 0] * 2.0 <= MAX_SCORE_RANGE_LOG2, lambda: call(True), lambda: call(False))


def _mem_kv_kernel(mem_ref, g_ref, w_ref, kg_ref, k_ref, v_ref):
    h = (_rms(mem_ref[...]) * g_ref[...]).astype(BF16)
    kv = jnp.dot(h, w_ref[...], preferred_element_type=F32)
    for hd in range(HEADS):
        sl = slice(hd * HEAD_W, (hd + 1) * HEAD_W)
        k_ref[:, sl] = (_rms(kv[:, sl]) * kg_ref[...]).astype(BF16)
    v_ref[...] = kv[:, BRANCH_W:].astype(BF16)


def _mem_kv(mem2, g, w_bf, kg, mem_len):
    n, d = mem2.shape
    out = pl.BlockSpec((mem_len, BRANCH_W), lambda i: (i, 0))
    return pl.pallas_call(
        _mem_kv_kernel,
        grid=(n // mem_len,),
        in_specs=[pl.BlockSpec((mem_len, d), lambda i: (i, 0)),
                  pl.BlockSpec((1, d), lambda i: (0, 0)),
                  pl.BlockSpec((d, 2 * BRANCH_W), lambda i: (0, 0)),
                  pl.BlockSpec((1, HEAD_W), lambda i: (0, 0))],
        out_specs=[out, out],
        out_shape=[jax.ShapeDtypeStruct((n, BRANCH_W), BF16)] * 2,
        compiler_params=_params("parallel"),
        name="mem_kv",
    )(mem2, g, w_bf, kg)


def _merge_kernel(x_ref, y_ref, sg_ref, ret_ref, rg_ref, dif_ref, dg_ref, mq_ref, mg_ref, mk_ref, mv_ref,
                  mrg0_ref, mrg1_ref, mrg2_ref, mrg3_ref, wglu_ref, bglu_ref, mqg_ref, wbr_ref, bm_ref, wout_ref,
                  o_ref):
    z = jax.nn.gelu(y_ref[...])
    glu = jnp.dot(wglu_ref[...], z.astype(BF16), preferred_element_type=F32) + bglu_ref[...]
    a_t = z * _sigmoid(glu) * _silu_of_half(sg_ref[...].astype(F32))
    branches = [lax.dot_general(a_t.astype(BF16), wbr_ref[0], TN_DIMS, preferred_element_type=F32)]
    b_out = ret_ref[...].astype(F32) * _silu_of_half(rg_ref[...].astype(F32))
    branches.append(jnp.dot(b_out.astype(BF16), wbr_ref[1], preferred_element_type=F32))
    c_out = dif_ref[...].astype(F32) * _silu_of_half(dg_ref[...].astype(F32))
    branches.append(jnp.dot(c_out.astype(BF16), wbr_ref[2], preferred_element_type=F32))
    heads = []
    for hd in range(HEADS):
        sl = slice(hd * HEAD_W, (hd + 1) * HEAD_W)
        qh = (_rms(mq_ref[:, sl].astype(F32)) * mqg_ref[...] * (HEAD_W ** -0.5)).astype(BF16)
        s = lax.dot_general(qh, mk_ref[:, sl], NT_DIMS, preferred_element_type=F32)
        p = jnp.exp(s - jnp.max(s, axis=-1, keepdims=True))
        o = jnp.dot(p.astype(BF16), mv_ref[:, sl], preferred_element_type=F32)
        heads.append(o / jnp.sum(p, axis=-1, keepdims=True))
    m_out = jnp.concatenate(heads, axis=-1) * _silu_of_half(mg_ref[...].astype(F32))
    branches.append(jnp.dot(m_out.astype(BF16), wbr_ref[3], preferred_element_type=F32))
    merged = None
    for nb, mrg_ref in enumerate((mrg0_ref, mrg1_ref, mrg2_ref, mrg3_ref)):
        gate = 1.0 + jnp.tanh(mrg_ref[...].astype(F32) + bm_ref[nb:nb + 1, :])
        merged = gate * branches[nb] if merged is None else merged + gate * branches[nb]
    o_ref[...] = x_ref[...] + jnp.dot(merged.astype(BF16), wout_ref[...], preferred_element_type=F32)


def _merge(x2, y_t, u_t, proj, ret_o, dif_o, mk, mv, wglu_t, bglu, mqg, wbr, bm, wout, seq, mem_len, tm):
    n, d = x2.shape

    def piece(p):
        return pl.BlockSpec((tm, BRANCH_W), lambda i: (i, p))

    def gate_piece(nb):
        return pl.BlockSpec((tm, D_MODEL), lambda i: (i, P_MERGE * BRANCH_W // D_MODEL + nb))

    tok = pl.BlockSpec((tm, BRANCH_W), lambda i: (i, 0))
    mem = pl.BlockSpec((mem_len, BRANCH_W), lambda i: ((i * tm) // seq, 0))
    full = lambda a: pl.BlockSpec(a.shape, lambda i: (0,) * a.ndim)
    return pl.pallas_call(
        _merge_kernel,
        grid=(n // tm,),
        in_specs=[pl.BlockSpec((tm, d), lambda i: (i, 0)),
                  pl.BlockSpec((BRANCH_W, tm), lambda i: (0, i)),
                  pl.BlockSpec((BRANCH_W, tm), lambda i: (1, i)),
                  tok, piece(P_RET_G), tok, piece(P_DIF_G), piece(P_MEM_Q), piece(P_MEM_G), mem, mem,
                  gate_piece(0), gate_piece(1), gate_piece(2), gate_piece(3),
                  full(wglu_t), full(bglu), full(mqg), full(wbr), full(bm), full(wout)],
        out_specs=pl.BlockSpec((tm, d), lambda i: (i, 0)),
        out_shape=jax.ShapeDtypeStruct((n, d), F32),
        compiler_params=_params("parallel"),
        name="merge",
    )(x2, y_t, u_t, ret_o, proj, dif_o, proj, proj, proj, mk, mv, proj, proj, proj, proj,
      wglu_t, bglu, mqg, wbr, bm, wout)


def _tile(n, pref):
    return pref if n % pref == 0 else n


def kernel(x, mem, norm_g, w_in, ssm_lambda_re, ssm_lambda_im, ssm_log_dt, ssm_b_re, ssm_b_im, ssm_c_re,
           ssm_c_im, ssm_d, ssm_w_glu, ssm_b_glu, diff_q_norm_g, diff_k_norm_g, diff_lambda_q1,
           diff_lambda_k1, diff_lambda_q2, diff_lambda_k2, diff_head_norm_g, mem_norm_g, w_mem_kv,
           mem_q_norm_g, mem_k_norm_g, w_branch, b_merge, w_out):
    batch, seq, d = x.shape
    mem_len = mem.shape[1]
    depth = w_in.shape[0]
    n = batch * seq
    assert d == D_MODEL and seq % S5_CHUNK == 0

    diff_inv = ROPE_THETA ** (-jnp.arange(0, DIFF_DH, 2, dtype=F32) / DIFF_DH)
    ret_inv = 1.0 / (ROPE_THETA ** jnp.linspace(0.0, 1.0, HEAD_W // 2, dtype=F32))
    inv = jnp.stack([jnp.tile(diff_inv, LANES // diff_inv.shape[0]),
                     jnp.tile(ret_inv, LANES // ret_inv.shape[0])])[:, None, :]
    lane = np.arange(LANES)
    sgn = np.stack([np.where(lane % DIFF_DH < DIFF_DH // 2, -1.0, 1.0),
                    np.where(lane < HEAD_W // 2, -1.0, 1.0)])[:, None, :]
    cos_t, sin_t = _rope_tables(seq, inv, jnp.asarray(sgn, F32), _tile(seq, 1024))

    piece = np.arange(w_in.shape[2]) // BRANCH_W
    halved = (piece >= N_IN_PIECES) | np.isin(piece, IN_GATE_PIECES)
    in_col_scale = jnp.asarray(np.where(halved, GATE_FOLD, 1.0), F32)

    x2 = x.reshape(n, d)
    mem2 = mem.reshape(batch * mem_len, d)
    tile2 = lambda v: jnp.tile(v, LANES // v.shape[0])[None, :]
    for l in range(depth):
        lambda_init = 0.8 - 0.6 * math.exp(-0.3 * l)
        g = norm_g[l][None, :]
        w = w_in[l] * in_col_scale[None, :]
        proj = _in_proj(x2, g, w[:, 2 * BRANCH_W:].astype(BF16), _tile(n, 2048), 1024)
        u_t = _in_proj_t(x2, g, w[:, :2 * BRANCH_W].T.astype(BF16), _tile(n, 1024))

        m_op, p_op, q_op, dec = _s5_prep(ssm_lambda_re[l], ssm_lambda_im[l], ssm_log_dt[l], ssm_b_re[l],
                                         ssm_b_im[l], ssm_c_re[l], ssm_c_im[l])
        y_t = _s5_conv(u_t, ssm_d[l], m_op, p_op, q_op, dec, batch, seq)

        rq, rk, dq, dk = _qk_prep(proj, cos_t, sin_t, tile2(diff_q_norm_g[l]), tile2(diff_k_norm_g[l]),
                                  seq, _tile(seq, 1024))
        ret_o = _retention(rq, rk, proj, batch, seq, _tile(seq, 256))
        dif_o = _diff_attn(dq, dk, proj, diff_q_norm_g[l], diff_k_norm_g[l], diff_lambda_q1[l][None],
                           diff_lambda_k1[l][None], diff_lambda_q2[l][None], diff_lambda_k2[l][None],
                           diff_head_norm_g[l][None], batch, seq, _tile(seq, 1024), _tile(seq, 512), lambda_init)
        mk, mv = _mem_kv(mem2, mem_norm_g[l][None, :], w_mem_kv[l].astype(BF16), mem_k_norm_g[l][None], mem_len)
        x2 = _merge(x2, y_t, u_t, proj, ret_o, dif_o, mk, mv, ssm_w_glu[l].T.astype(BF16),
                    ssm_b_glu[l][:, None], mem_q_norm_g[l][None], w_branch[l].astype(BF16), GATE_FOLD * b_merge[l],
                    (GATE_FOLD * w_out[l]).astype(BF16), seq, mem_len, _tile(seq, 512))
    return x2.reshape(batch, seq, d)
```

```python
import functools
import math

import numpy as np
import jax
import jax.numpy as jnp
from jax import lax
from jax.experimental import pallas as pl
from jax.experimental.pallas import tpu as pltpu

F32 = jnp.float32
BF16 = jnp.bfloat16

D_MODEL = 1024
BRANCH_W = D_MODEL // 2
N_BRANCH = 4
N_IN_PIECES = 12
CHUNK = 64
SSM_GROUP = 16
SSM_GROUPS = BRANCH_W // SSM_GROUP
SSM_STATE = 64
HEADS = 4
HEAD_W = BRANCH_W // HEADS
DIFF_DH = HEAD_W // 2
ROPE_THETA = 10000.0
EPS = 1e-6
NEG_INF = -1e30
GATE_FOLD = 0.5
IN_GATE_PIECES = (1, 5, 9, 11)
LOG2_E = math.log2(math.e)
DIFF_Q_SCALE = DIFF_DH ** -0.5 * LOG2_E
MAX_SCORE_RANGE_LOG2 = 100.0

LANES = 128
S5_CHUNK = LANES
S5_FLAT = SSM_GROUP * S5_CHUNK
PROJ_COLS = (N_IN_PIECES - 2) * BRANCH_W + N_BRANCH * D_MODEL
P_RET_Q, P_RET_K, P_RET_V, P_RET_G, P_DIF_Q, P_DIF_K, P_DIF_V, P_DIF_G, P_MEM_Q, P_MEM_G, P_MERGE = range(11)

VMEM_LIMIT = 56 * 1024 * 1024

NT_DIMS = (((1,), (1,)), ((), ()))
TN_DIMS = (((0,), (0,)), ((), ()))


def _params(*sem):
    return pltpu.CompilerParams(dimension_semantics=sem, vmem_limit_bytes=VMEM_LIMIT)


def _rms(x, eps=EPS):
    return x * lax.rsqrt(jnp.mean(x * x, axis=-1, keepdims=True) + eps)


def _sigmoid(x):
    return 0.5 * jnp.tanh(0.5 * x) + 0.5


def _silu_of_half(xh):
    return xh * (1.0 + jnp.tanh(xh))


def _in_proj_kernel(x_ref, g_ref, w_ref, o_ref, h_ref):
    @pl.when(pl.program_id(1) == 0)
    def _():
        h_ref[...] = (_rms(x_ref[...]) * g_ref[...]).astype(BF16)

    o_ref[...] = jnp.dot(h_ref[...], w_ref[...], preferred_element_type=F32).astype(o_ref.dtype)


def _in_proj(x2, g, w_bf, tm, tn):
    n, d = x2.shape
    cols = w_bf.shape[1]
    return pl.pallas_call(
        _in_proj_kernel,
        grid=(n // tm, cols // tn),
        in_specs=[pl.BlockSpec((tm, d), lambda i, j: (i, 0)),
                  pl.BlockSpec((1, d), lambda i, j: (0, 0)),
                  pl.BlockSpec((d, tn), lambda i, j: (0, j))],
        out_specs=pl.BlockSpec((tm, tn), lambda i, j: (i, j)),
        out_shape=jax.ShapeDtypeStruct((n, cols), BF16),
        scratch_shapes=[pltpu.VMEM((tm, d), BF16)],
        compiler_params=_params("parallel", "arbitrary"),
        name="in_proj",
    )(x2, g, w_bf)


def _in_proj_t_kernel(x_ref, g_ref, wt_ref, o_ref):
    h = (_rms(x_ref[...]) * g_ref[...]).astype(BF16)
    o_ref[...] = lax.dot_general(wt_ref[...], h, NT_DIMS, preferred_element_type=F32).astype(o_ref.dtype)


def _in_proj_t(x2, g, wt_bf, tm):
    n, d = x2.shape
    rows = wt_bf.shape[0]
    return pl.pallas_call(
        _in_proj_t_kernel,
        grid=(n // tm,),
        in_specs=[pl.BlockSpec((tm, d), lambda i: (i, 0)),
                  pl.BlockSpec((1, d), lambda i: (0, 0)),
                  pl.BlockSpec((rows, d), lambda i: (0, 0))],
        out_specs=pl.BlockSpec((rows, tm), lambda i: (0, i)),
        out_shape=jax.ShapeDtypeStruct((rows, n), BF16),
        compiler_params=_params("parallel"),
        name="in_proj_t",
    )(x2, g, wt_bf)


def _rope_table_kernel(inv_ref, sgn_ref, cos_ref, sin_ref):
    ts = cos_ref.shape[0]
    pos = (lax.broadcasted_iota(jnp.int32, (ts, LANES), 0) + pl.program_id(1) * ts).astype(F32)
    ang = pos * inv_ref[...]
    cos_ref[...] = jnp.cos(ang)
    sin_ref[...] = jnp.sin(ang) * sgn_ref[...]


def _rope_tables(seq, inv, sgn, ts):
    spec_in = pl.BlockSpec((None, 1, LANES), lambda t, i: (t, 0, 0))
    spec_out = pl.BlockSpec((None, ts, LANES), lambda t, i: (t, i, 0))
    return pl.pallas_call(
        _rope_table_kernel,
        grid=(2, seq // ts),
        in_specs=[spec_in, spec_in],
        out_specs=[spec_out, spec_out],
        out_shape=[jax.ShapeDtypeStruct((2, seq, LANES), F32)] * 2,
        compiler_params=_params("parallel", "parallel"),
        name="rope_tables",
    )(inv, sgn)


def _qk_prep_kernel(rq_ref, rk_ref, dq_ref, dk_ref, cos_ref, sin_ref, qg_ref, kg_ref,
                    orq_ref, ork_ref, odq_ref, odk_ref):
    cos_d, sin_d = cos_ref[0], sin_ref[0]
    cos_r, sin_r = cos_ref[1], sin_ref[1]
    ts = cos_d.shape[0]
    lane = lax.broadcasted_iota(jnp.int32, (ts, LANES), 1)
    low_map = lane < DIFF_DH
    first_half = (lane % DIFF_DH) < (DIFF_DH // 2)

    def seg_norm(x):
        x2 = x * x
        lo = jnp.sum(jnp.where(low_map, x2, 0.0), axis=-1, keepdims=True)
        hi = jnp.sum(jnp.where(low_map, 0.0, x2), axis=-1, keepdims=True)
        ms = jnp.where(low_map, lo, hi) * (1.0 / DIFF_DH)
        return x * lax.rsqrt(ms + EPS)

    def rope_d(x):
        partner = jnp.where(first_half, pltpu.roll(x, LANES - DIFF_DH // 2, 1),
                            pltpu.roll(x, DIFF_DH // 2, 1))
        return x * cos_d + partner * sin_d

    def rope_r(x):
        return x * cos_r + pltpu.roll(x, HEAD_W // 2, 1) * sin_r

    for h in range(HEADS):
        sl = slice(h * HEAD_W, (h + 1) * HEAD_W)
        orq_ref[:, sl] = rope_r(rq_ref[:, sl].astype(F32)).astype(BF16)
        ork_ref[:, sl] = (rope_r(rk_ref[:, sl].astype(F32)) * (HEAD_W ** -0.5)).astype(BF16)
        q = seg_norm(dq_ref[:, sl].astype(F32)) * qg_ref[...]
        odq_ref[:, sl] = (rope_d(q) * DIFF_Q_SCALE).astype(BF16)
        k = seg_norm(dk_ref[:, sl].astype(F32)) * kg_ref[...]
        odk_ref[:, sl] = rope_d(k).astype(BF16)


def _qk_prep(proj, cos_t, sin_t, qg, kg, seq, ts):
    n = proj.shape[0]
    nsb = seq // ts

    def piece(p):
        return pl.BlockSpec((ts, BRANCH_W), lambda i: (i, p))

    tab = pl.BlockSpec((2, ts, LANES), lambda i: (0, i % nsb, 0))
    vec = pl.BlockSpec((1, LANES), lambda i: (0, 0))
    out = pl.BlockSpec((ts, BRANCH_W), lambda i: (i, 0))
    return pl.pallas_call(
        _qk_prep_kernel,
        grid=(n // ts,),
        in_specs=[piece(P_RET_Q), piece(P_RET_K), piece(P_DIF_Q), piece(P_DIF_K), tab, tab, vec, vec],
        out_specs=[out] * 4,
        out_shape=[jax.ShapeDtypeStruct((n, BRANCH_W), BF16)] * 4,
        compiler_params=_params("parallel"),
        name="qk_prep",
    )(proj, proj, proj, proj, cos_t, sin_t, qg, kg)


def _s5_prep_kernel(lam_c_ref, lam_r_ref, b_ref, bt_ref, c_ref, ct_ref,
                    m_ref, p_ref, q_ref, dec_ref, kern_ref):
    T = S5_CHUNK

    def disc(lam_re, lam_im, log_dt):
        lr = jnp.minimum(lam_re, -1e-4)
        li = lam_im
        dt = jnp.exp(log_dt)
        mag = jnp.exp(lr * dt)
        ab_re = mag * jnp.cos(li * dt)
        ab_im = mag * jnp.sin(li * dt)
        nr = ab_re - 1.0
        ni = ab_im
        den = lr * lr + li * li
        f_re = (nr * lr + ni * li) / den
        f_im = (ni * lr - nr * li) / den
        return lr * dt, li * dt, f_re, f_im

    def powers(ldt_re, ldt_im, k):
        mag = jnp.exp(k * ldt_re)
        return mag * jnp.cos(k * ldt_im), mag * jnp.sin(k * ldt_im)

    ldr, ldi, f_re, f_im = disc(lam_c_ref[0], lam_c_ref[1], lam_c_ref[2])
    b_re, b_im = b_ref[0], b_ref[1]
    bb_re = f_re * b_re - f_im * b_im
    bb_im = f_re * b_im + f_im * b_re
    k_row = lax.broadcasted_iota(jnp.int32, (1, T), 1).astype(F32)
    pw_re, pw_im = powers(ldr, ldi, k_row)
    pw1_re, pw1_im = powers(ldr, ldi, k_row + 1.0)
    c_re, c_im = c_ref[0], c_ref[1]
    ct_re, ct_im = ct_ref[0], ct_ref[1]

    for c in range(SSM_GROUP):
        w_re = pw_re * bb_re[:, c:c + 1] - pw_im * bb_im[:, c:c + 1]
        w_im = pw_re * bb_im[:, c:c + 1] + pw_im * bb_re[:, c:c + 1]
        kern = (jnp.dot(c_re, w_re, preferred_element_type=F32, precision=lax.Precision.HIGHEST)
                - jnp.dot(c_im, w_im, preferred_element_type=F32, precision=lax.Precision.HIGHEST))
        for co in range(SSM_GROUP):
            kern_ref[c, :, co * T:(co + 1) * T] = kern[co:co + 1, :]
        q_ref[0, :, c * T:(c + 1) * T] = (ct_re[:, c:c + 1] * pw1_re - ct_im[:, c:c + 1] * pw1_im).astype(q_ref.dtype)
        q_ref[1, :, c * T:(c + 1) * T] = (-ct_re[:, c:c + 1] * pw1_im - ct_im[:, c:c + 1] * pw1_re).astype(q_ref.dtype)

    jj = lax.broadcasted_iota(jnp.int32, (T, S5_FLAT), 0)
    tt = lax.broadcasted_iota(jnp.int32, (T, S5_FLAT), 1) % T
    causal = tt >= jj

    def toeplitz_rows(c, carry):
        row = jnp.broadcast_to(kern_ref[c], (T, S5_FLAT))
        toep = pltpu.roll(row, 0, 1, stride=1, stride_axis=0)
        m_ref[pl.ds(pl.multiple_of(c * T, T), T), :] = jnp.where(causal, toep, 0.0).astype(m_ref.dtype)
        return carry

    lax.fori_loop(0, SSM_GROUP, toeplitz_rows, 0)

    ldr_r, ldi_r, fr_r, fi_r = disc(lam_r_ref[0], lam_r_ref[1], lam_r_ref[2])
    bt_re, bt_im = bt_ref[0], bt_ref[1]
    bbt_re = fr_r * bt_re - fi_r * bt_im
    bbt_im = fr_r * bt_im + fi_r * bt_re
    k_col = (T - 1.0) - lax.broadcasted_iota(jnp.int32, (T, 1), 0).astype(F32)
    pr_re, pr_im = powers(ldr_r, ldi_r, k_col)
    for c in range(SSM_GROUP):
        p_ref[0, c * T:(c + 1) * T, :] = (pr_re * bbt_re[c:c + 1, :] - pr_im * bbt_im[c:c + 1, :]).astype(p_ref.dtype)
        p_ref[1, c * T:(c + 1) * T, :] = (pr_re * bbt_im[c:c + 1, :] + pr_im * bbt_re[c:c + 1, :]).astype(p_ref.dtype)
    d_re, d_im = powers(ldr_r, ldi_r, float(T))
    dec_ref[0] = d_re
    dec_ref[1] = d_im


def _s5_prep(lam_re, lam_im, log_dt, b_re, b_im, c_re, c_im):
    g, p = lam_re.shape
    ldt = jnp.broadcast_to(log_dt[:, None], (g, p))
    lam = jnp.stack([lam_re, lam_im, ldt], axis=1)
    lam_c = lam[..., None]
    lam_r = lam[:, :, None, :]
    b = jnp.stack([b_re, b_im], axis=1)
    bt = jnp.swapaxes(b, 2, 3)
    c = jnp.stack([c_re, c_im], axis=1)
    ct = jnp.swapaxes(c, 2, 3)

    def spec(shape):
        return pl.BlockSpec((None,) + shape, lambda i: (i,) + (0,) * len(shape))

    return pl.pallas_call(
        _s5_prep_kernel,
        grid=(g,),
        in_specs=[spec((3, p, 1)), spec((3, 1, p)), spec((2, p, SSM_GROUP)), spec((2, SSM_GROUP, p)),
                  spec((2, SSM_GROUP, p)), spec((2, p, SSM_GROUP))],
        out_specs=[spec((S5_FLAT, S5_FLAT)), spec((2, S5_FLAT, p)), spec((2, p, S5_FLAT)), spec((2, 1, p))],
        out_shape=[jax.ShapeDtypeStruct((g, S5_FLAT, S5_FLAT), BF16),
                   jax.ShapeDtypeStruct((g, 2, S5_FLAT, p), BF16),
                   jax.ShapeDtypeStruct((g, 2, p, S5_FLAT), BF16),
                   jax.ShapeDtypeStruct((g, 2, 1, p), F32)],
        scratch_shapes=[pltpu.VMEM((SSM_GROUP, 1, S5_FLAT), F32)],
        compiler_params=_params("parallel"),
        name="s5_prep",
    )(lam_c, lam_r, b, bt, c, ct)


def _s5_conv_kernel(d_ref, u_ref, m_ref, p_ref, q_ref, dec_ref, y_ref, v_ref, s_ref, *, batch, n_chunks):
    g = pl.program_id(0)
    T = S5_CHUNK
    u_flat = jnp.concatenate([u_ref[c] for c in range(SSM_GROUP)], axis=-1)
    rows = batch * n_chunks
    for ri in range(2):
        v_ref[ri] = jnp.dot(u_flat, p_ref[ri], preferred_element_type=F32).reshape(batch, n_chunks, SSM_STATE)
    dr, di = dec_ref[0], dec_ref[1]

    def body(ch, carry):
        sr, si = carry
        s_ref[0, :, pl.ds(ch, 1), :] = sr
        s_ref[1, :, pl.ds(ch, 1), :] = si
        vr = v_ref[0, :, pl.ds(ch, 1), :]
        vi = v_ref[1, :, pl.ds(ch, 1), :]
        return dr * sr - di * si + vr, dr * si + di * sr + vi

    zero = jnp.zeros((batch, 1, SSM_STATE), F32)
    lax.fori_loop(0, n_chunks, body, (zero, zero))
    y = jnp.dot(u_flat, m_ref[...], preferred_element_type=F32)
    for ri in range(2):
        s_start = s_ref[ri].reshape(rows, SSM_STATE).astype(BF16)
        y += jnp.dot(s_start, q_ref[ri], preferred_element_type=F32)
    for c in range(SSM_GROUP):
        y_ref[c] = y[:, c * T:(c + 1) * T] + d_ref[g * SSM_GROUP + c] * u_ref[c].astype(F32)


def _s5_conv(u_t, d_skip, m, p, q, dec, batch, seq):
    n = u_t.shape[1]
    r = n // S5_CHUNK
    u3 = u_t.reshape(u_t.shape[0], r, S5_CHUNK)
    g = SSM_GROUPS

    def spec(shape):
        return pl.BlockSpec((None,) + shape, lambda i: (i,) + (0,) * len(shape))

    y3 = pl.pallas_call(
        functools.partial(_s5_conv_kernel, batch=batch, n_chunks=seq // S5_CHUNK),
        grid=(g,),
        in_specs=[pl.BlockSpec(memory_space=pltpu.SMEM),
                  pl.BlockSpec((SSM_GROUP, r, S5_CHUNK), lambda i: (i, 0, 0)),
                  spec((S5_FLAT, S5_FLAT)), spec((2, S5_FLAT, SSM_STATE)), spec((2, SSM_STATE, S5_FLAT)),
                  spec((2, 1, SSM_STATE))],
        out_specs=pl.BlockSpec((SSM_GROUP, r, S5_CHUNK), lambda i: (i, 0, 0)),
        out_shape=jax.ShapeDtypeStruct((BRANCH_W, r, S5_CHUNK), F32),
        scratch_shapes=[pltpu.VMEM((2, batch, seq // S5_CHUNK, SSM_STATE), F32)] * 2,
        compiler_params=_params("parallel"),
        name="s5_conv",
    )(d_skip, u3, m, p, q, dec)
    return y3.reshape(BRANCH_W, n)


def _retention_kernel(q_ref, k_ref, v_ref, mask_ref, qd_ref, kd_ref, cd_ref, o_ref, st_ref):
    @pl.when(pl.program_id(1) == 0)
    def _():
        st_ref[...] = jnp.zeros_like(st_ref)

    for h in range(HEADS):
        sl = slice(h * HEAD_W, (h + 1) * HEAD_W)
        q, k, v = q_ref[:, sl], k_ref[:, sl], v_ref[:, sl]
        scores = lax.dot_general(q, k, NT_DIMS, preferred_element_type=F32) * mask_ref[h]
        st = st_ref[h]
        o = (jnp.dot(scores.astype(BF16), v, preferred_element_type=F32)
             + jnp.dot(q, st.astype(BF16), preferred_element_type=F32) * qd_ref[h])
        kd = (k.astype(F32) * kd_ref[h]).astype(BF16)
        st_ref[h] = st * cd_ref[h] + lax.dot_general(kd, v, TN_DIMS, preferred_element_type=F32)
        o_ref[:, sl] = _rms(o).astype(o_ref.dtype)


def _retention_consts(blk):
    log_gamma = np.log1p(-(2.0 ** (-5.0 - np.arange(HEADS, dtype=np.float64))))
    idx = np.arange(blk, dtype=np.float64)
    diff = idx[:, None] - idx[None, :]
    same = (np.arange(blk)[:, None] // CHUNK) == (np.arange(blk)[None, :] // CHUNK)
    past = (np.arange(blk)[None, :] // CHUNK) < (np.arange(blk)[:, None] // CHUNK)
    expo = np.where(same, np.abs(diff), diff)
    mask = np.where(same | past, np.exp(expo[None] * log_gamma[:, None, None]), 0.0)
    q_dec = np.exp((idx + 1.0)[None, :] * log_gamma[:, None])
    k_dec = np.exp((blk - 1.0 - idx)[None, :] * log_gamma[:, None])
    c_dec = np.exp(blk * log_gamma)
    bc = lambda a: np.broadcast_to(a[:, :, None], (HEADS, blk, HEAD_W))
    return (jnp.asarray(mask, F32), jnp.asarray(bc(q_dec), F32), jnp.asarray(bc(k_dec), F32),
            jnp.asarray(np.broadcast_to(c_dec[:, None, None], (HEADS, 1, HEAD_W)), F32))


def _retention(rq, rk, proj, batch, seq, blk):
    n = rq.shape[0]
    nb = seq // blk
    mask, q_dec, k_dec, c_dec = _retention_consts(blk)
    tok = pl.BlockSpec((blk, BRANCH_W), lambda b, l: (b * nb + l, 0))
    full = lambda a: pl.BlockSpec(a.shape, lambda b, l: (0,) * a.ndim)
    return pl.pallas_call(
        _retention_kernel,
        grid=(batch, nb),
        in_specs=[tok, tok, pl.BlockSpec((blk, BRANCH_W), lambda b, l: (b * nb + l, P_RET_V)),
                  full(mask), full(q_dec), full(k_dec), full(c_dec)],
        out_specs=tok,
        out_shape=jax.ShapeDtypeStruct((n, BRANCH_W), BF16),
        scratch_shapes=[pltpu.VMEM((HEADS, HEAD_W, HEAD_W), F32)],
        compiler_params=_params("parallel", "arbitrary"),
        name="retention",
    )(rq, rk, proj, mask, q_dec, k_dec, c_dec)


def _diff_attn_kernel(bound_ref, q_ref, k_ref, v_ref, lq1_ref, lk1_ref, lq2_ref, lk2_ref, hg_ref, o_ref,
                      acc_ref, *stat_refs, tq, tk, lambda_init, fixed_stabiliser):
    i = pl.program_id(2)
    n_sub = tq // tk
    q = q_ref[...]
    lane = lax.broadcasted_iota(jnp.int32, (tq, HEAD_W), 1)
    qs = (jnp.where(lane < DIFF_DH, q, jnp.zeros_like(q)), jnp.where(lane < DIFF_DH, jnp.zeros_like(q), q))
    acc_ref[...] = jnp.zeros_like(acc_ref)
    if fixed_stabiliser:
        shift = bound_ref[0, 0]
    else:
        m_ref, = stat_refs
        m_ref[...] = jnp.full_like(m_ref, NEG_INF)
    qc = lax.broadcasted_iota(jnp.int32, (tq, tk), 0) // CHUNK
    kc = lax.broadcasted_iota(jnp.int32, (tq, tk), 1) // CHUNK

    def tile(j, diag):
        rows = pl.ds(pl.multiple_of(j * tk, tk), tk)
        kt = k_ref[rows, :]
        vt = v_ref[rows, :]
        v_ones = jnp.concatenate([vt, jnp.ones_like(vt)], axis=1)
        r0 = 0 if diag is None else diag * tk
        visible = None if diag is None else ((kc + diag * (tk // CHUNK)) <= qc)[r0:]
        for mp in range(2):
            s = lax.dot_general(qs[mp][r0:], kt, NT_DIMS, preferred_element_type=F32)
            if fixed_stabiliser:
                p = jnp.exp2(s - shift)
                if visible is not None:
                    p = jnp.where(visible, p, 0.0)
                acc_ref[mp, r0:] += jnp.dot(p.astype(BF16), v_ones, preferred_element_type=F32)
            else:
                if visible is not None:
                    s = jnp.where(visible, s, NEG_INF)
                m_prev = m_ref[mp, r0:]
                m_new = jnp.maximum(m_prev, jnp.max(s, axis=-1, keepdims=True))
                p = jnp.exp2(s - m_new)
                acc_ref[mp, r0:] = (jnp.exp2(m_prev - m_new) * acc_ref[mp, r0:]
                                    + jnp.dot(p.astype(BF16), v_ones, preferred_element_type=F32))
                m_ref[mp, r0:] = m_new

    def body(jj, carry):
        for u in range(n_sub):
            tile(jj * n_sub + u, None)
        return carry

    lax.fori_loop(0, i, body, 0)
    for d in range(n_sub):
        tile(i * n_sub + d, d)

    lam = (jnp.exp(jnp.sum(lq1_ref[...] * lk1_ref[...], axis=-1, keepdims=True))
           - jnp.exp(jnp.sum(lq2_ref[...] * lk2_ref[...], axis=-1, keepdims=True)) + lambda_init)
    o = (acc_ref[0, :, :HEAD_W] / acc_ref[0, :, HEAD_W:]
         - lam * (acc_ref[1, :, :HEAD_W] / acc_ref[1, :, HEAD_W:]))
    o_ref[...] = (_rms(o) * hg_ref[...] * (1.0 - lambda_init)).astype(o_ref.dtype)


def _diff_attn(dq, dk, proj, qg, kg, lq1, lk1, lq2, lk2, hg, batch, seq, tq, tk, lambda_init):
    n = dq.shape[0]
    nq = seq // tq
    bound = (1.01 * DIFF_DH * DIFF_Q_SCALE * jnp.max(jnp.abs(qg)) * jnp.max(jnp.abs(kg))).reshape(1, 1)
    vec64 = pl.BlockSpec((1, DIFF_DH), lambda b, h, i: (0, 0))

    def call(fixed_stabiliser):
        stat_scratch = [] if fixed_stabiliser else [pltpu.VMEM((2, tq, 1), F32)]
        return pl.pallas_call(
            functools.partial(_diff_attn_kernel, tq=tq, tk=tk, lambda_init=lambda_init,
                              fixed_stabiliser=fixed_stabiliser),
            grid=(batch, HEADS, nq),
            in_specs=[pl.BlockSpec(memory_space=pltpu.SMEM),
                      pl.BlockSpec((tq, HEAD_W), lambda b, h, i: (b * nq + i, h)),
                      pl.BlockSpec((seq, HEAD_W), lambda b, h, i: (b, h)),
                      pl.BlockSpec((seq, HEAD_W), lambda b, h, i: (b, P_DIF_V * HEADS + h)),
                      vec64, vec64, vec64, vec64,
                      pl.BlockSpec((1, HEAD_W), lambda b, h, i: (0, 0))],
            out_specs=pl.BlockSpec((tq, HEAD_W), lambda b, h, i: (b * nq + i, h)),
            out_shape=jax.ShapeDtypeStruct((n, BRANCH_W), BF16),
            scratch_shapes=[pltpu.VMEM((2, tq, 2 * HEAD_W), F32)] + stat_scratch,
            compiler_params=_params("parallel", "parallel", "arbitrary"),
            name="diff_attn_fixed" if fixed_stabiliser else "diff_attn_online",
        )(bound, dq, dk, proj, lq1, lk1, lq2, lk2, hg)

    return lax.cond(bound[0, 0] * 2.0 <= MAX_SCORE_RANGE_LOG2, lambda: call(True), lambda: call(False))


def _mem_kv_kernel(mem_ref, g_ref, w_ref, kg_ref, k_ref, v_ref):
    h = (_rms(mem_ref[...]) * g_ref[...]).astype(BF16)
    kv = jnp.dot(h, w_ref[...], preferred_element_type=F32)
    for hd in range(HEADS):
        sl = slice(hd * HEAD_W, (hd + 1) * HEAD_W)
        k_ref[:, sl] = (_rms(kv[:, sl]) * kg_ref[...]).astype(BF16)
    v_ref[...] = kv[:, BRANCH_W:].astype(BF16)


def _mem_kv(mem2, g, w_bf, kg, mem_len):
    n, d = mem2.shape
    out = pl.BlockSpec((mem_len, BRANCH_W), lambda i: (i, 0))
    return pl.pallas_call(
        _mem_kv_kernel,
        grid=(n // mem_len,),
        in_specs=[pl.BlockSpec((mem_len, d), lambda i: (i, 0)),
                  pl.BlockSpec((1, d), lambda i: (0, 0)),
                  pl.BlockSpec((d, 2 * BRANCH_W), lambda i: (0, 0)),
                  pl.BlockSpec((1, HEAD_W), lambda i: (0, 0))],
        out_specs=[out, out],
        out_shape=[jax.ShapeDtypeStruct((n, BRANCH_W), BF16)] * 2,
        compiler_params=_params("parallel"),
        name="mem_kv",
    )(mem2, g, w_bf, kg)


def _merge_kernel(x_ref, y_ref, sg_ref, ret_ref, rg_ref, dif_ref, dg_ref, mq_ref, mg_ref, mk_ref, mv_ref,
                  mrg0_ref, mrg1_ref, mrg2_ref, mrg3_ref, wglu_ref, bglu_ref, mqg_ref, wbr_ref, bm_ref, wout_ref,
                  o_ref):
    z = jax.nn.gelu(y_ref[...])
    glu = jnp.dot(wglu_ref[...], z.astype(BF16), preferred_element_type=F32) + bglu_ref[...]
    a_t = (z * _sigmoid(glu)).astype(BF16) * _silu_of_half(sg_ref[...])
    branches = [lax.dot_general(a_t, wbr_ref[0], TN_DIMS, preferred_element_type=F32)]
    b_out = ret_ref[...] * _silu_of_half(rg_ref[...])
    branches.append(jnp.dot(b_out, wbr_ref[1], preferred_element_type=F32))
    c_out = dif_ref[...] * _silu_of_half(dg_ref[...])
    branches.append(jnp.dot(c_out, wbr_ref[2], preferred_element_type=F32))
    heads = []
    for hd in range(HEADS):
        sl = slice(hd * HEAD_W, (hd + 1) * HEAD_W)
        qh = (_rms(mq_ref[:, sl].astype(F32)) * mqg_ref[...] * (HEAD_W ** -0.5)).astype(BF16)
        s = lax.dot_general(qh, mk_ref[:, sl], NT_DIMS, preferred_element_type=F32)
        p = jnp.exp(s - jnp.max(s, axis=-1, keepdims=True))
        o = jnp.dot(p.astype(BF16), mv_ref[:, sl], preferred_element_type=F32)
        heads.append(o / jnp.sum(p, axis=-1, keepdims=True))
    m_out = jnp.concatenate(heads, axis=-1).astype(BF16) * _silu_of_half(mg_ref[...])
    branches.append(jnp.dot(m_out, wbr_ref[3], preferred_element_type=F32))
    merged = None
    for nb, mrg_ref in enumerate((mrg0_ref, mrg1_ref, mrg2_ref, mrg3_ref)):
        gate = 1.0 + jnp.tanh(mrg_ref[...] + bm_ref[nb:nb + 1, :].astype(BF16))
        term = gate * branches[nb].astype(BF16)
        merged = term if merged is None else merged + term
    o_ref[...] = x_ref[...] + jnp.dot(merged, wout_ref[...], preferred_element_type=F32)


def _merge(x2, y_t, u_t, proj, ret_o, dif_o, mk, mv, wglu_t, bglu, mqg, wbr, bm, wout, seq, mem_len, tm):
    n, d = x2.shape

    def piece(p):
        return pl.BlockSpec((tm, BRANCH_W), lambda i: (i, p))

    def gate_piece(nb):
        return pl.BlockSpec((tm, D_MODEL), lambda i: (i, P_MERGE * BRANCH_W // D_MODEL + nb))

    tok = pl.BlockSpec((tm, BRANCH_W), lambda i: (i, 0))
    mem = pl.BlockSpec((mem_len, BRANCH_W), lambda i: ((i * tm) // seq, 0))
    full = lambda a: pl.BlockSpec(a.shape, lambda i: (0,) * a.ndim)
    return pl.pallas_call(
        _merge_kernel,
        grid=(n // tm,),
        in_specs=[pl.BlockSpec((tm, d), lambda i: (i, 0)),
                  pl.BlockSpec((BRANCH_W, tm), lambda i: (0, i)),
                  pl.BlockSpec((BRANCH_W, tm), lambda i: (1, i)),
                  tok, piece(P_RET_G), tok, piece(P_DIF_G), piece(P_MEM_Q), piece(P_MEM_G), mem, mem,
                  gate_piece(0), gate_piece(1), gate_piece(2), gate_piece(3),
                  full(wglu_t), full(bglu), full(mqg), full(wbr), full(bm), full(wout)],
        out_specs=pl.BlockSpec((tm, d), lambda i: (i, 0)),
        out_shape=jax.ShapeDtypeStruct((n, d), F32),
        compiler_params=_params("parallel"),
        name="merge",
    )(x2, y_t, u_t, ret_o, proj, dif_o, proj, proj, proj, mk, mv, proj, proj, proj, proj,
      wglu_t, bglu, mqg, wbr, bm, wout)


def _tile(n, pref):
    return pref if n % pref == 0 else n


def kernel(x, mem, norm_g, w_in, ssm_lambda_re, ssm_lambda_im, ssm_log_dt, ssm_b_re, ssm_b_im, ssm_c_re,
           ssm_c_im, ssm_d, ssm_w_glu, ssm_b_glu, diff_q_norm_g, diff_k_norm_g, diff_lambda_q1,
           diff_lambda_k1, diff_lambda_q2, diff_lambda_k2, diff_head_norm_g, mem_norm_g, w_mem_kv,
           mem_q_norm_g, mem_k_norm_g, w_branch, b_merge, w_out):
    batch, seq, d = x.shape
    mem_len = mem.shape[1]
    depth = w_in.shape[0]
    n = batch * seq
    assert d == D_MODEL and seq % S5_CHUNK == 0

    diff_inv = ROPE_THETA ** (-jnp.arange(0, DIFF_DH, 2, dtype=F32) / DIFF_DH)
    ret_inv = 1.0 / (ROPE_THETA ** jnp.linspace(0.0, 1.0, HEAD_W // 2, dtype=F32))
    inv = jnp.stack([jnp.tile(diff_inv, LANES // diff_inv.shape[0]),
                     jnp.tile(ret_inv, LANES // ret_inv.shape[0])])[:, None, :]
    lane = np.arange(LANES)
    sgn = np.stack([np.where(lane % DIFF_DH < DIFF_DH // 2, -1.0, 1.0),
                    np.where(lane < HEAD_W // 2, -1.0, 1.0)])[:, None, :]
    cos_t, sin_t = _rope_tables(seq, inv, jnp.asarray(sgn, F32), _tile(seq, 1024))

    piece = np.arange(w_in.shape[2]) // BRANCH_W
    halved = (piece >= N_IN_PIECES) | np.isin(piece, IN_GATE_PIECES)
    in_col_scale = jnp.asarray(np.where(halved, GATE_FOLD, 1.0), F32)

    x2 = x.reshape(n, d)
    mem2 = mem.reshape(batch * mem_len, d)
    tile2 = lambda v: jnp.tile(v, LANES // v.shape[0])[None, :]
    for l in range(depth):
        lambda_init = 0.8 - 0.6 * math.exp(-0.3 * l)
        g = norm_g[l][None, :]
        w = w_in[l] * in_col_scale[None, :]
        proj = _in_proj(x2, g, w[:, 2 * BRANCH_W:].astype(BF16), _tile(n, 2048), 1024)
        u_t = _in_proj_t(x2, g, w[:, :2 * BRANCH_W].T.astype(BF16), _tile(n, 1024))

        m_op, p_op, q_op, dec = _s5_prep(ssm_lambda_re[l], ssm_lambda_im[l], ssm_log_dt[l], ssm_b_re[l],
                                         ssm_b_im[l], ssm_c_re[l], ssm_c_im[l])
        y_t = _s5_conv(u_t, ssm_d[l], m_op, p_op, q_op, dec, batch, seq)

        rq, rk, dq, dk = _qk_prep(proj, cos_t, sin_t, tile2(diff_q_norm_g[l]), tile2(diff_k_norm_g[l]),
                                  seq, _tile(seq, 1024))
        ret_o = _retention(rq, rk, proj, batch, seq, _tile(seq, 256))
        dif_o = _diff_attn(dq, dk, proj, diff_q_norm_g[l], diff_k_norm_g[l], diff_lambda_q1[l][None],
                           diff_lambda_k1[l][None], diff_lambda_q2[l][None], diff_lambda_k2[l][None],
                           diff_head_norm_g[l][None], batch, seq, _tile(seq, 2048), _tile(seq, 512), lambda_init)
        mk, mv = _mem_kv(mem2, mem_norm_g[l][None, :], w_mem_kv[l].astype(BF16), mem_k_norm_g[l][None], mem_len)
        x2 = _merge(x2, y_t, u_t, proj, ret_o, dif_o, mk, mv, ssm_w_glu[l].T.astype(BF16),
                    ssm_b_glu[l][:, None], mem_q_norm_g[l][None], w_branch[l].astype(BF16), GATE_FOLD * b_merge[l],
                    (GATE_FOLD * w_out[l]).astype(BF16), seq, mem_len, _tile(seq, 512))
    return x2.reshape(batch, seq, d)
```

```python
import functools
import math

import numpy as np
import jax
import jax.numpy as jnp
from jax import lax
from jax.experimental import pallas as pl
from jax.experimental.pallas import tpu as pltpu

F32 = jnp.float32
BF16 = jnp.bfloat16

D_MODEL = 1024
BRANCH_W = D_MODEL // 2
N_BRANCH = 4
N_IN_PIECES = 12
CHUNK = 64
SSM_GROUP = 16
SSM_GROUPS = BRANCH_W // SSM_GROUP
SSM_STATE = 64
HEADS = 4
HEAD_W = BRANCH_W // HEADS
DIFF_DH = HEAD_W // 2
ROPE_THETA = 10000.0
EPS = 1e-6
NEG_INF = -1e30
GATE_FOLD = 0.5
IN_GATE_PIECES = (1, 5, 9, 11)
LOG2_E = math.log2(math.e)
DIFF_Q_SCALE = DIFF_DH ** -0.5 * LOG2_E
MAX_SCORE_RANGE_LOG2 = 100.0

LANES = 128
S5_CHUNK = LANES
S5_FLAT = SSM_GROUP * S5_CHUNK
PROJ_COLS = (N_IN_PIECES - 2) * BRANCH_W + N_BRANCH * D_MODEL
P_RET_Q, P_RET_K, P_RET_V, P_RET_G, P_DIF_Q, P_DIF_K, P_DIF_V, P_DIF_G, P_MEM_Q, P_MEM_G, P_MERGE = range(11)

VMEM_LIMIT = 56 * 1024 * 1024

NT_DIMS = (((1,), (1,)), ((), ()))
TN_DIMS = (((0,), (0,)), ((), ()))


def _params(*sem):
    return pltpu.CompilerParams(dimension_semantics=sem, vmem_limit_bytes=VMEM_LIMIT)


def _rms(x, eps=EPS):
    return x * lax.rsqrt(jnp.mean(x * x, axis=-1, keepdims=True) + eps)


def _sigmoid(x):
    return 0.5 * jnp.tanh(0.5 * x) + 0.5


def _silu_of_half(xh):
    return xh * (1.0 + jnp.tanh(xh))


def _in_proj_kernel(x_ref, g_ref, w_ref, o_ref, h_ref):
    @pl.when(pl.program_id(1) == 0)
    def _():
        h_ref[...] = (_rms(x_ref[...]) * g_ref[...]).astype(BF16)

    o_ref[...] = jnp.dot(h_ref[...], w_ref[...], preferred_element_type=F32).astype(o_ref.dtype)


def _in_proj(x2, g, w_bf, tm, tn):
    n, d = x2.shape
    cols = w_bf.shape[1]
    return pl.pallas_call(
        _in_proj_kernel,
        grid=(n // tm, cols // tn),
        in_specs=[pl.BlockSpec((tm, d), lambda i, j: (i, 0)),
                  pl.BlockSpec((1, d), lambda i, j: (0, 0)),
                  pl.BlockSpec((d, tn), lambda i, j: (0, j))],
        out_specs=pl.BlockSpec((tm, tn), lambda i, j: (i, j)),
        out_shape=jax.ShapeDtypeStruct((n, cols), BF16),
        scratch_shapes=[pltpu.VMEM((tm, d), BF16)],
        compiler_params=_params("parallel", "arbitrary"),
        name="in_proj",
    )(x2, g, w_bf)


def _in_proj_t_kernel(x_ref, g_ref, wt_ref, o_ref):
    h = (_rms(x_ref[...]) * g_ref[...]).astype(BF16)
    o_ref[...] = lax.dot_general(wt_ref[...], h, NT_DIMS, preferred_element_type=F32).astype(o_ref.dtype)


def _in_proj_t(x2, g, wt_bf, tm):
    n, d = x2.shape
    rows = wt_bf.shape[0]
    return pl.pallas_call(
        _in_proj_t_kernel,
        grid=(n // tm,),
        in_specs=[pl.BlockSpec((tm, d), lambda i: (i, 0)),
                  pl.BlockSpec((1, d), lambda i: (0, 0)),
                  pl.BlockSpec((rows, d), lambda i: (0, 0))],
        out_specs=pl.BlockSpec((rows, tm), lambda i: (0, i)),
        out_shape=jax.ShapeDtypeStruct((rows, n), BF16),
        compiler_params=_params("parallel"),
        name="in_proj_t",
    )(x2, g, wt_bf)


def _rope_table_kernel(inv_ref, sgn_ref, cos_ref, sin_ref):
    ts = cos_ref.shape[0]
    pos = (lax.broadcasted_iota(jnp.int32, (ts, LANES), 0) + pl.program_id(1) * ts).astype(F32)
    ang = pos * inv_ref[...]
    cos_ref[...] = jnp.cos(ang)
    sin_ref[...] = jnp.sin(ang) * sgn_ref[...]


def _rope_tables(seq, inv, sgn, ts):
    spec_in = pl.BlockSpec((None, 1, LANES), lambda t, i: (t, 0, 0))
    spec_out = pl.BlockSpec((None, ts, LANES), lambda t, i: (t, i, 0))
    return pl.pallas_call(
        _rope_table_kernel,
        grid=(2, seq // ts),
        in_specs=[spec_in, spec_in],
        out_specs=[spec_out, spec_out],
        out_shape=[jax.ShapeDtypeStruct((2, seq, LANES), F32)] * 2,
        compiler_params=_params("parallel", "parallel"),
        name="rope_tables",
    )(inv, sgn)


def _qk_prep_kernel(rq_ref, rk_ref, dq_ref, dk_ref, cos_ref, sin_ref, qg_ref, kg_ref, perm_ref,
                    orq_ref, ork_ref, odq_ref, odk_ref):
    two = lambda t: jnp.concatenate([t, t], axis=1)
    cos_d, sin_d, cos_r, sin_r = two(cos_ref[0]), two(sin_ref[0]), two(cos_ref[1]), two(sin_ref[1])
    k_scale = HEAD_W ** -0.5
    cos_rk, sin_rk = cos_r * k_scale, sin_r * k_scale
    q_cos, q_sin = cos_d * (two(qg_ref[0]) * DIFF_Q_SCALE), sin_d * (two(qg_ref[1]) * DIFF_Q_SCALE)
    k_cos, k_sin = cos_d * two(kg_ref[0]), sin_d * two(kg_ref[1])

    def shuffled(x, which):
        return jnp.dot(x, perm_ref[which], preferred_element_type=F32)

    def rope_r(x, c, s):
        return (x.astype(F32) * c + shuffled(x, 0) * s).astype(BF16)

    def norm_rope_d(x, c, s):
        xf = x.astype(F32)
        ms = shuffled((xf * xf).astype(BF16), 2) * (1.0 / DIFF_DH)
        return (lax.rsqrt(ms + EPS) * (xf * c + shuffled(x, 1) * s)).astype(BF16)

    for hp in range(HEADS // 2):
        sl = slice(hp * 2 * HEAD_W, (hp + 1) * 2 * HEAD_W)
        orq_ref[:, sl] = rope_r(rq_ref[:, sl], cos_r, sin_r)
        ork_ref[:, sl] = rope_r(rk_ref[:, sl], cos_rk, sin_rk)
        odq_ref[:, sl] = norm_rope_d(dq_ref[:, sl], q_cos, q_sin)
        odk_ref[:, sl] = norm_rope_d(dk_ref[:, sl], k_cos, k_sin)


def _qk_prep_consts():
    lane = np.arange(LANES)
    half_rot = (lane + HEAD_W // 2) % LANES
    partner = np.where(lane % DIFF_DH < DIFF_DH // 2, lane + DIFF_DH // 2, lane - DIFF_DH // 2)
    perms = np.zeros((3, LANES, LANES), np.float32)
    perms[0, half_rot, lane] = 1.0
    perms[1, partner, lane] = 1.0
    perms[2] = (lane[:, None] // DIFF_DH) == (lane[None, :] // DIFF_DH)
    two_heads = np.zeros((3, 2 * LANES, 2 * LANES), np.float32)
    two_heads[:, :LANES, :LANES] = perms
    two_heads[:, LANES:, LANES:] = perms
    return jnp.asarray(two_heads, BF16), partner


def _qk_prep(proj, cos_t, sin_t, qg, kg, seq, ts):
    n = proj.shape[0]
    nsb = seq // ts
    perms, partner = _qk_prep_consts()
    with_partner = lambda g: jnp.stack([jnp.tile(g, 2), jnp.tile(g, 2)[partner]])[:, None, :]

    def piece(p):
        return pl.BlockSpec((ts, BRANCH_W), lambda i: (i, p))

    tab = pl.BlockSpec((2, ts, LANES), lambda i: (0, i % nsb, 0))
    vec = pl.BlockSpec((2, 1, LANES), lambda i: (0, 0, 0))
    out = pl.BlockSpec((ts, BRANCH_W), lambda i: (i, 0))
    qg, kg = with_partner(qg), with_partner(kg)
    return pl.pallas_call(
        _qk_prep_kernel,
        grid=(n // ts,),
        in_specs=[piece(P_RET_Q), piece(P_RET_K), piece(P_DIF_Q), piece(P_DIF_K), tab, tab, vec, vec,
                  pl.BlockSpec(perms.shape, lambda i: (0, 0, 0))],
        out_specs=[out] * 4,
        out_shape=[jax.ShapeDtypeStruct((n, BRANCH_W), BF16)] * 4,
        compiler_params=_params("parallel"),
        name="qk_prep",
    )(proj, proj, proj, proj, cos_t, sin_t, qg, kg, perms)


def _s5_prep_kernel(lam_c_ref, lam_r_ref, b_ref, bt_ref, c_ref, ct_ref,
                    m_ref, p_ref, q_ref, dec_ref, kern_ref):
    T = S5_CHUNK

    def disc(lam_re, lam_im, log_dt):
        lr = jnp.minimum(lam_re, -1e-4)
        li = lam_im
        dt = jnp.exp(log_dt)
        mag = jnp.exp(lr * dt)
        ab_re = mag * jnp.cos(li * dt)
        ab_im = mag * jnp.sin(li * dt)
        nr = ab_re - 1.0
        ni = ab_im
        den = lr * lr + li * li
        f_re = (nr * lr + ni * li) / den
        f_im = (ni * lr - nr * li) / den
        return lr * dt, li * dt, f_re, f_im

    def powers(ldt_re, ldt_im, k):
        mag = jnp.exp(k * ldt_re)
        return mag * jnp.cos(k * ldt_im), mag * jnp.sin(k * ldt_im)

    ldr, ldi, f_re, f_im = disc(lam_c_ref[0], lam_c_ref[1], lam_c_ref[2])
    b_re, b_im = b_ref[0], b_ref[1]
    bb_re = f_re * b_re - f_im * b_im
    bb_im = f_re * b_im + f_im * b_re
    k_row = lax.broadcasted_iota(jnp.int32, (1, T), 1).astype(F32)
    pw_re, pw_im = powers(ldr, ldi, k_row)
    pw1_re, pw1_im = powers(ldr, ldi, k_row + 1.0)
    c_re, c_im = c_ref[0], c_ref[1]
    ct_re, ct_im = ct_ref[0], ct_ref[1]

    for c in range(SSM_GROUP):
        w_re = pw_re * bb_re[:, c:c + 1] - pw_im * bb_im[:, c:c + 1]
        w_im = pw_re * bb_im[:, c:c + 1] + pw_im * bb_re[:, c:c + 1]
        kern = (jnp.dot(c_re, w_re, preferred_element_type=F32, precision=lax.Precision.HIGHEST)
                - jnp.dot(c_im, w_im, preferred_element_type=F32, precision=lax.Precision.HIGHEST))
        for co in range(SSM_GROUP):
            kern_ref[c, :, co * T:(co + 1) * T] = kern[co:co + 1, :]
        q_ref[0, :, c * T:(c + 1) * T] = (ct_re[:, c:c + 1] * pw1_re - ct_im[:, c:c + 1] * pw1_im).astype(q_ref.dtype)
        q_ref[1, :, c * T:(c + 1) * T] = (-ct_re[:, c:c + 1] * pw1_im - ct_im[:, c:c + 1] * pw1_re).astype(q_ref.dtype)

    jj = lax.broadcasted_iota(jnp.int32, (T, S5_FLAT), 0)
    tt = lax.broadcasted_iota(jnp.int32, (T, S5_FLAT), 1) % T
    causal = tt >= jj

    def toeplitz_rows(c, carry):
        row = jnp.broadcast_to(kern_ref[c], (T, S5_FLAT))
        toep = pltpu.roll(row, 0, 1, stride=1, stride_axis=0)
        m_ref[pl.ds(pl.multiple_of(c * T, T), T), :] = jnp.where(causal, toep, 0.0).astype(m_ref.dtype)
        return carry

    lax.fori_loop(0, SSM_GROUP, toeplitz_rows, 0)

    ldr_r, ldi_r, fr_r, fi_r = disc(lam_r_ref[0], lam_r_ref[1], lam_r_ref[2])
    bt_re, bt_im = bt_ref[0], bt_ref[1]
    bbt_re = fr_r * bt_re - fi_r * bt_im
    bbt_im = fr_r * bt_im + fi_r * bt_re
    k_col = (T - 1.0) - lax.broadcasted_iota(jnp.int32, (T, 1), 0).astype(F32)
    pr_re, pr_im = powers(ldr_r, ldi_r, k_col)
    for c in range(SSM_GROUP):
        p_ref[0, c * T:(c + 1) * T, :] = (pr_re * bbt_re[c:c + 1, :] - pr_im * bbt_im[c:c + 1, :]).astype(p_ref.dtype)
        p_ref[1, c * T:(c + 1) * T, :] = (pr_re * bbt_im[c:c + 1, :] + pr_im * bbt_re[c:c + 1, :]).astype(p_ref.dtype)
    d_re, d_im = powers(ldr_r, ldi_r, float(T))
    dec_ref[0] = d_re
    dec_ref[1] = d_im


def _s5_prep(lam_re, lam_im, log_dt, b_re, b_im, c_re, c_im):
    g, p = lam_re.shape
    ldt = jnp.broadcast_to(log_dt[:, None], (g, p))
    lam = jnp.stack([lam_re, lam_im, ldt], axis=1)
    lam_c = lam[..., None]
    lam_r = lam[:, :, None, :]
    b = jnp.stack([b_re, b_im], axis=1)
    bt = jnp.swapaxes(b, 2, 3)
    c = jnp.stack([c_re, c_im], axis=1)
    ct = jnp.swapaxes(c, 2, 3)

    def spec(shape):
        return pl.BlockSpec((None,) + shape, lambda i: (i,) + (0,) * len(shape))

    return pl.pallas_call(
        _s5_prep_kernel,
        grid=(g,),
        in_specs=[spec((3, p, 1)), spec((3, 1, p)), spec((2, p, SSM_GROUP)), spec((2, SSM_GROUP, p)),
                  spec((2, SSM_GROUP, p)), spec((2, p, SSM_GROUP))],
        out_specs=[spec((S5_FLAT, S5_FLAT)), spec((2, S5_FLAT, p)), spec((2, p, S5_FLAT)), spec((2, 1, p))],
        out_shape=[jax.ShapeDtypeStruct((g, S5_FLAT, S5_FLAT), BF16),
                   jax.ShapeDtypeStruct((g, 2, S5_FLAT, p), BF16),
                   jax.ShapeDtypeStruct((g, 2, p, S5_FLAT), BF16),
                   jax.ShapeDtypeStruct((g, 2, 1, p), F32)],
        scratch_shapes=[pltpu.VMEM((SSM_GROUP, 1, S5_FLAT), F32)],
        compiler_params=_params("parallel"),
        name="s5_prep",
    )(lam_c, lam_r, b, bt, c, ct)


def _s5_conv_kernel(d_ref, u_ref, m_ref, p_ref, q_ref, dec_ref, y_ref, v_ref, s_ref, *, batch, n_chunks):
    g = pl.program_id(0)
    T = S5_CHUNK
    u_flat = jnp.concatenate([u_ref[c] for c in range(SSM_GROUP)], axis=-1)
    rows = batch * n_chunks
    for ri in range(2):
        v_ref[ri] = jnp.dot(u_flat, p_ref[ri], preferred_element_type=F32).reshape(batch, n_chunks, SSM_STATE)
    dr, di = dec_ref[0], dec_ref[1]

    def body(ch, carry):
        sr, si = carry
        s_ref[0, :, pl.ds(ch, 1), :] = sr
        s_ref[1, :, pl.ds(ch, 1), :] = si
        vr = v_ref[0, :, pl.ds(ch, 1), :]
        vi = v_ref[1, :, pl.ds(ch, 1), :]
        return dr * sr - di * si + vr, dr * si + di * sr + vi

    zero = jnp.zeros((batch, 1, SSM_STATE), F32)
    lax.fori_loop(0, n_chunks, body, (zero, zero))
    y = jnp.dot(u_flat, m_ref[...], preferred_element_type=F32)
    for ri in range(2):
        s_start = s_ref[ri].reshape(rows, SSM_STATE).astype(BF16)
        y += jnp.dot(s_start, q_ref[ri], preferred_element_type=F32)
    for c in range(SSM_GROUP):
        y_ref[c] = y[:, c * T:(c + 1) * T] + d_ref[g * SSM_GROUP + c] * u_ref[c].astype(F32)


def _s5_conv(u_t, d_skip, m, p, q, dec, batch, seq):
    n = u_t.shape[1]
    r = n // S5_CHUNK
    u3 = u_t.reshape(u_t.shape[0], r, S5_CHUNK)
    g = SSM_GROUPS

    def spec(shape):
        return pl.BlockSpec((None,) + shape, lambda i: (i,) + (0,) * len(shape))

    y3 = pl.pallas_call(
        functools.partial(_s5_conv_kernel, batch=batch, n_chunks=seq // S5_CHUNK),
        grid=(g,),
        in_specs=[pl.BlockSpec(memory_space=pltpu.SMEM),
                  pl.BlockSpec((SSM_GROUP, r, S5_CHUNK), lambda i: (i, 0, 0)),
                  spec((S5_FLAT, S5_FLAT)), spec((2, S5_FLAT, SSM_STATE)), spec((2, SSM_STATE, S5_FLAT)),
                  spec((2, 1, SSM_STATE))],
        out_specs=pl.BlockSpec((SSM_GROUP, r, S5_CHUNK), lambda i: (i, 0, 0)),
        out_shape=jax.ShapeDtypeStruct((BRANCH_W, r, S5_CHUNK), F32),
        scratch_shapes=[pltpu.VMEM((2, batch, seq // S5_CHUNK, SSM_STATE), F32)] * 2,
        compiler_params=_params("parallel"),
        name="s5_conv",
    )(d_skip, u3, m, p, q, dec)
    return y3.reshape(BRANCH_W, n)


def _retention_kernel(q_ref, k_ref, v_ref, mask_ref, qd_ref, kd_ref, cd_ref, o_ref, st_ref):
    @pl.when(pl.program_id(1) == 0)
    def _():
        st_ref[...] = jnp.zeros_like(st_ref)

    for h in range(HEADS):
        sl = slice(h * HEAD_W, (h + 1) * HEAD_W)
        q, k, v = q_ref[:, sl], k_ref[:, sl], v_ref[:, sl]
        scores = lax.dot_general(q, k, NT_DIMS, preferred_element_type=F32) * mask_ref[h]
        st = st_ref[h]
        o = (jnp.dot(scores.astype(BF16), v, preferred_element_type=F32)
             + jnp.dot(q, st.astype(BF16), preferred_element_type=F32) * qd_ref[h])
        kd = (k.astype(F32) * kd_ref[h]).astype(BF16)
        st_ref[h] = st * cd_ref[h] + lax.dot_general(kd, v, TN_DIMS, preferred_element_type=F32)
        o_ref[:, sl] = _rms(o).astype(o_ref.dtype)


def _retention_consts(blk):
    log_gamma = np.log1p(-(2.0 ** (-5.0 - np.arange(HEADS, dtype=np.float64))))
    idx = np.arange(blk, dtype=np.float64)
    diff = idx[:, None] - idx[None, :]
    same = (np.arange(blk)[:, None] // CHUNK) == (np.arange(blk)[None, :] // CHUNK)
    past = (np.arange(blk)[None, :] // CHUNK) < (np.arange(blk)[:, None] // CHUNK)
    expo = np.where(same, np.abs(diff), diff)
    mask = np.where(same | past, np.exp(expo[None] * log_gamma[:, None, None]), 0.0)
    q_dec = np.exp((idx + 1.0)[None, :] * log_gamma[:, None])
    k_dec = np.exp((blk - 1.0 - idx)[None, :] * log_gamma[:, None])
    c_dec = np.exp(blk * log_gamma)
    bc = lambda a: np.broadcast_to(a[:, :, None], (HEADS, blk, HEAD_W))
    return (jnp.asarray(mask, F32), jnp.asarray(bc(q_dec), F32), jnp.asarray(bc(k_dec), F32),
            jnp.asarray(np.broadcast_to(c_dec[:, None, None], (HEADS, 1, HEAD_W)), F32))


def _retention(rq, rk, proj, batch, seq, blk):
    n = rq.shape[0]
    nb = seq // blk
    mask, q_dec, k_dec, c_dec = _retention_consts(blk)
    tok = pl.BlockSpec((blk, BRANCH_W), lambda b, l: (b * nb + l, 0))
    full = lambda a: pl.BlockSpec(a.shape, lambda b, l: (0,) * a.ndim)
    return pl.pallas_call(
        _retention_kernel,
        grid=(batch, nb),
        in_specs=[tok, tok, pl.BlockSpec((blk, BRANCH_W), lambda b, l: (b * nb + l, P_RET_V)),
                  full(mask), full(q_dec), full(k_dec), full(c_dec)],
        out_specs=tok,
        out_shape=jax.ShapeDtypeStruct((n, BRANCH_W), BF16),
        scratch_shapes=[pltpu.VMEM((HEADS, HEAD_W, HEAD_W), F32)],
        compiler_params=_params("parallel", "arbitrary"),
        name="retention",
    )(rq, rk, proj, mask, q_dec, k_dec, c_dec)


def _diff_attn_kernel(bound_ref, q_ref, k_ref, v_ref, lq1_ref, lk1_ref, lq2_ref, lk2_ref, hg_ref, o_ref,
                      acc_ref, *stat_refs, tq, tk, lambda_init, fixed_stabiliser):
    i = pl.program_id(2)
    n_sub = tq // tk
    q = q_ref[...]
    lane = lax.broadcasted_iota(jnp.int32, (tq, HEAD_W), 1)
    qs = (jnp.where(lane < DIFF_DH, q, jnp.zeros_like(q)), jnp.where(lane < DIFF_DH, jnp.zeros_like(q), q))
    acc_ref[...] = jnp.zeros_like(acc_ref)
    if fixed_stabiliser:
        shift = bound_ref[0, 0]
    else:
        m_ref, = stat_refs
        m_ref[...] = jnp.full_like(m_ref, NEG_INF)
    qc = lax.broadcasted_iota(jnp.int32, (tq, tk), 0) // CHUNK
    kc = lax.broadcasted_iota(jnp.int32, (tq, tk), 1) // CHUNK

    def tile(j, diag):
        rows = pl.ds(pl.multiple_of(j * tk, tk), tk)
        kt = k_ref[rows, :]
        vt = v_ref[rows, :]
        v_ones = jnp.concatenate([vt, jnp.ones_like(vt)], axis=1)
        r0 = 0 if diag is None else diag * tk
        visible = None if diag is None else ((kc + diag * (tk // CHUNK)) <= qc)[r0:]
        for mp in range(2):
            s = lax.dot_general(qs[mp][r0:], kt, NT_DIMS, preferred_element_type=F32)
            if fixed_stabiliser:
                p = jnp.exp2(s - shift)
                if visible is not None:
                    p = jnp.where(visible, p, 0.0)
                acc_ref[mp, r0:] += jnp.dot(p.astype(BF16), v_ones, preferred_element_type=F32)
            else:
                if visible is not None:
                    s = jnp.where(visible, s, NEG_INF)
                m_prev = m_ref[mp, r0:]
                m_new = jnp.maximum(m_prev, jnp.max(s, axis=-1, keepdims=True))
                p = jnp.exp2(s - m_new)
                acc_ref[mp, r0:] = (jnp.exp2(m_prev - m_new) * acc_ref[mp, r0:]
                                    + jnp.dot(p.astype(BF16), v_ones, preferred_element_type=F32))
                m_ref[mp, r0:] = m_new

    def body(jj, carry):
        for u in range(n_sub):
            tile(jj * n_sub + u, None)
        return carry

    lax.fori_loop(0, i, body, 0)
    for d in range(n_sub):
        tile(i * n_sub + d, d)

    lam = (jnp.exp(jnp.sum(lq1_ref[...] * lk1_ref[...], axis=-1, keepdims=True))
           - jnp.exp(jnp.sum(lq2_ref[...] * lk2_ref[...], axis=-1, keepdims=True)) + lambda_init)
    o = (acc_ref[0, :, :HEAD_W] / acc_ref[0, :, HEAD_W:]
         - lam * (acc_ref[1, :, :HEAD_W] / acc_ref[1, :, HEAD_W:]))
    o_ref[...] = (_rms(o) * hg_ref[...] * (1.0 - lambda_init)).astype(o_ref.dtype)


def _diff_attn(dq, dk, proj, qg, kg, lq1, lk1, lq2, lk2, hg, batch, seq, tq, tk, lambda_init):
    n = dq.shape[0]
    nq = seq // tq
    bound = (1.01 * DIFF_DH * DIFF_Q_SCALE * jnp.max(jnp.abs(qg)) * jnp.max(jnp.abs(kg))).reshape(1, 1)
    vec64 = pl.BlockSpec((1, DIFF_DH), lambda b, h, i: (0, 0))

    def call(fixed_stabiliser):
        stat_scratch = [] if fixed_stabiliser else [pltpu.VMEM((2, tq, 1), F32)]
        return pl.pallas_call(
            functools.partial(_diff_attn_kernel, tq=tq, tk=tk, lambda_init=lambda_init,
                              fixed_stabiliser=fixed_stabiliser),
            grid=(batch, HEADS, nq),
            in_specs=[pl.BlockSpec(memory_space=pltpu.SMEM),
                      pl.BlockSpec((tq, HEAD_W), lambda b, h, i: (b * nq + i, h)),
                      pl.BlockSpec((seq, HEAD_W), lambda b, h, i: (b, h)),
                      pl.BlockSpec((seq, HEAD_W), lambda b, h, i: (b, P_DIF_V * HEADS + h)),
                      vec64, vec64, vec64, vec64,
                      pl.BlockSpec((1, HEAD_W), lambda b, h, i: (0, 0))],
            out_specs=pl.BlockSpec((tq, HEAD_W), lambda b, h, i: (b * nq + i, h)),
            out_shape=jax.ShapeDtypeStruct((n, BRANCH_W), BF16),
            scratch_shapes=[pltpu.VMEM((2, tq, 2 * HEAD_W), F32)] + stat_scratch,
            compiler_params=_params("parallel", "parallel", "arbitrary"),
            name="diff_attn_fixed" if fixed_stabiliser else "diff_attn_online",
        )(bound, dq, dk, proj, lq1, lk1, lq2, lk2, hg)

    return lax.cond(bound[0, 0] * 2.0 <= MAX_SCORE_RANGE_LOG2, lambda: call(True), lambda: call(False))


def _mem_kv_kernel(mem_ref, g_ref, w_ref, kg_ref, k_ref, v_ref):
    h = (_rms(mem_ref[...]) * g_ref[...]).astype(BF16)
    kv = jnp.dot(h, w_ref[...], preferred_element_type=F32)
    for hd in range(HEADS):
        sl = slice(hd * HEAD_W, (hd + 1) * HEAD_W)
        k_ref[:, sl] = (_rms(kv[:, sl]) * kg_ref[...]).astype(BF16)
    v_ref[...] = kv[:, BRANCH_W:].astype(BF16)


def _mem_kv(mem2, g, w_bf, kg, mem_len):
    n, d = mem2.shape
    out = pl.BlockSpec((mem_len, BRANCH_W), lambda i: (i, 0))
    return pl.pallas_call(
        _mem_kv_kernel,
        grid=(n // mem_len,),
        in_specs=[pl.BlockSpec((mem_len, d), lambda i: (i, 0)),
                  pl.BlockSpec((1, d), lambda i: (0, 0)),
                  pl.BlockSpec((d, 2 * BRANCH_W), lambda i: (0, 0)),
                  pl.BlockSpec((1, HEAD_W), lambda i: (0, 0))],
        out_specs=[out, out],
        out_shape=[jax.ShapeDtypeStruct((n, BRANCH_W), BF16)] * 2,
        compiler_params=_params("parallel"),
        name="mem_kv",
    )(mem2, g, w_bf, kg)


def _merge_kernel(x_ref, y_ref, sg_ref, ret_ref, rg_ref, dif_ref, dg_ref, mq_ref, mg_ref, mk_ref, mv_ref,
                  mrg0_ref, mrg1_ref, mrg2_ref, mrg3_ref, wglu_ref, bglu_ref, mqg_ref, wbr_ref, bm_ref, wout_ref,
                  o_ref):
    z = jax.nn.gelu(y_ref[...])
    glu = jnp.dot(wglu_ref[...], z.astype(BF16), preferred_element_type=F32) + bglu_ref[...]
    a_t = (z * _sigmoid(glu)).astype(BF16) * _silu_of_half(sg_ref[...])
    branches = [lax.dot_general(a_t, wbr_ref[0], TN_DIMS, preferred_element_type=F32)]
    b_out = ret_ref[...] * _silu_of_half(rg_ref[...])
    branches.append(jnp.dot(b_out, wbr_ref[1], preferred_element_type=F32))
    c_out = dif_ref[...] * _silu_of_half(dg_ref[...])
    branches.append(jnp.dot(c_out, wbr_ref[2], preferred_element_type=F32))
    heads = []
    for hd in range(HEADS):
        sl = slice(hd * HEAD_W, (hd + 1) * HEAD_W)
        qh = (_rms(mq_ref[:, sl].astype(F32)) * mqg_ref[...] * (HEAD_W ** -0.5)).astype(BF16)
        s = lax.dot_general(qh, mk_ref[:, sl], NT_DIMS, preferred_element_type=F32)
        p = jnp.exp(s - jnp.max(s, axis=-1, keepdims=True))
        o = jnp.dot(p.astype(BF16), mv_ref[:, sl], preferred_element_type=F32)
        heads.append(o / jnp.sum(p, axis=-1, keepdims=True))
    m_out = jnp.concatenate(heads, axis=-1).astype(BF16) * _silu_of_half(mg_ref[...])
    branches.append(jnp.dot(m_out, wbr_ref[3], preferred_element_type=F32))
    merged = None
    for nb, mrg_ref in enumerate((mrg0_ref, mrg1_ref, mrg2_ref, mrg3_ref)):
        gate = 1.0 + jnp.tanh(mrg_ref[...] + bm_ref[nb:nb + 1, :].astype(BF16))
        term = gate * branches[nb].astype(BF16)
        merged = term if merged is None else merged + term
    o_ref[...] = x_ref[...] + jnp.dot(merged, wout_ref[...], preferred_element_type=F32)


def _merge(x2, y_t, u_t, proj, ret_o, dif_o, mk, mv, wglu_t, bglu, mqg, wbr, bm, wout, seq, mem_len, tm):
    n, d = x2.shape

    def piece(p):
        return pl.BlockSpec((tm, BRANCH_W), lambda i: (i, p))

    def gate_piece(nb):
        return pl.BlockSpec((tm, D_MODEL), lambda i: (i, P_MERGE * BRANCH_W // D_MODEL + nb))

    tok = pl.BlockSpec((tm, BRANCH_W), lambda i: (i, 0))
    mem = pl.BlockSpec((mem_len, BRANCH_W), lambda i: ((i * tm) // seq, 0))
    full = lambda a: pl.BlockSpec(a.shape, lambda i: (0,) * a.ndim)
    return pl.pallas_call(
        _merge_kernel,
        grid=(n // tm,),
        in_specs=[pl.BlockSpec((tm, d), lambda i: (i, 0)),
                  pl.BlockSpec((BRANCH_W, tm), lambda i: (0, i)),
                  pl.BlockSpec((BRANCH_W, tm), lambda i: (1, i)),
                  tok, piece(P_RET_G), tok, piece(P_DIF_G), piece(P_MEM_Q), piece(P_MEM_G), mem, mem,
                  gate_piece(0), gate_piece(1), gate_piece(2), gate_piece(3),
                  full(wglu_t), full(bglu), full(mqg), full(wbr), full(bm), full(wout)],
        out_specs=pl.BlockSpec((tm, d), lambda i: (i, 0)),
        out_shape=jax.ShapeDtypeStruct((n, d), F32),
        compiler_params=_params("parallel"),
        name="merge",
    )(x2, y_t, u_t, ret_o, proj, dif_o, proj, proj, proj, mk, mv, proj, proj, proj, proj,
      wglu_t, bglu, mqg, wbr, bm, wout)


def _tile(n, pref):
    return pref if n % pref == 0 else n


def kernel(x, mem, norm_g, w_in, ssm_lambda_re, ssm_lambda_im, ssm_log_dt, ssm_b_re, ssm_b_im, ssm_c_re,
           ssm_c_im, ssm_d, ssm_w_glu, ssm_b_glu, diff_q_norm_g, diff_k_norm_g, diff_lambda_q1,
           diff_lambda_k1, diff_lambda_q2, diff_lambda_k2, diff_head_norm_g, mem_norm_g, w_mem_kv,
           mem_q_norm_g, mem_k_norm_g, w_branch, b_merge, w_out):
    batch, seq, d = x.shape
    mem_len = mem.shape[1]
    depth = w_in.shape[0]
    n = batch * seq
    assert d == D_MODEL and seq % S5_CHUNK == 0

    diff_inv = ROPE_THETA ** (-jnp.arange(0, DIFF_DH, 2, dtype=F32) / DIFF_DH)
    ret_inv = 1.0 / (ROPE_THETA ** jnp.linspace(0.0, 1.0, HEAD_W // 2, dtype=F32))
    inv = jnp.stack([jnp.tile(diff_inv, LANES // diff_inv.shape[0]),
                     jnp.tile(ret_inv, LANES // ret_inv.shape[0])])[:, None, :]
    lane = np.arange(LANES)
    sgn = np.stack([np.where(lane % DIFF_DH < DIFF_DH // 2, -1.0, 1.0),
                    np.where(lane < HEAD_W // 2, -1.0, 1.0)])[:, None, :]
    cos_t, sin_t = _rope_tables(seq, inv, jnp.asarray(sgn, F32), _tile(seq, 1024))

    piece = np.arange(w_in.shape[2]) // BRANCH_W
    halved = (piece >= N_IN_PIECES) | np.isin(piece, IN_GATE_PIECES)
    in_col_scale = jnp.asarray(np.where(halved, GATE_FOLD, 1.0), F32)

    x2 = x.reshape(n, d)
    mem2 = mem.reshape(batch * mem_len, d)
    for l in range(depth):
        lambda_init = 0.8 - 0.6 * math.exp(-0.3 * l)
        g = norm_g[l][None, :]
        w = w_in[l] * in_col_scale[None, :]
        proj = _in_proj(x2, g, w[:, 2 * BRANCH_W:].astype(BF16), _tile(n, 2048), 1024)
        u_t = _in_proj_t(x2, g, w[:, :2 * BRANCH_W].T.astype(BF16), _tile(n, 1024))

        m_op, p_op, q_op, dec = _s5_prep(ssm_lambda_re[l], ssm_lambda_im[l], ssm_log_dt[l], ssm_b_re[l],
                                         ssm_b_im[l], ssm_c_re[l], ssm_c_im[l])
        y_t = _s5_conv(u_t, ssm_d[l], m_op, p_op, q_op, dec, batch, seq)

        rq, rk, dq, dk = _qk_prep(proj, cos_t, sin_t, diff_q_norm_g[l], diff_k_norm_g[l], seq, _tile(seq, 1024))
        ret_o = _retention(rq, rk, proj, batch, seq, _tile(seq, 256))
        dif_o = _diff_attn(dq, dk, proj, diff_q_norm_g[l], diff_k_norm_g[l], diff_lambda_q1[l][None],
                           diff_lambda_k1[l][None], diff_lambda_q2[l][None], diff_lambda_k2[l][None],
                           diff_head_norm_g[l][None], batch, seq, _tile(seq, 2048), _tile(seq, 512), lambda_init)
        mk, mv = _mem_kv(mem2, mem_norm_g[l][None, :], w_mem_kv[l].astype(BF16), mem_k_norm_g[l][None], mem_len)
        x2 = _merge(x2, y_t, u_t, proj, ret_o, dif_o, mk, mv, ssm_w_glu[l].T.astype(BF16),
                    ssm_b_glu[l][:, None], mem_q_norm_g[l][None], w_branch[l].astype(BF16), GATE_FOLD * b_merge[l],
                    (GATE_FOLD * w_out[l]).astype(BF16), seq, mem_len, _tile(seq, 512))
    return x2.reshape(batch, seq, d)
```

```python
import functools
import math

import numpy as np
import jax
import jax.numpy as jnp
from jax import lax
from jax.experimental import pallas as pl
from jax.experimental.pallas import tpu as pltpu

F32 = jnp.float32
BF16 = jnp.bfloat16

D_MODEL = 1024
BRANCH_W = D_MODEL // 2
N_BRANCH = 4
N_IN_PIECES = 12
CHUNK = 64
SSM_GROUP = 16
SSM_GROUPS = BRANCH_W // SSM_GROUP
SSM_STATE = 64
HEADS = 4
HEAD_W = BRANCH_W // HEADS
DIFF_DH = HEAD_W // 2
ROPE_THETA = 10000.0
EPS = 1e-6
NEG_INF = -1e30
GATE_FOLD = 0.5
IN_GATE_PIECES = (1, 5, 9, 11)
LOG2_E = math.log2(math.e)
DIFF_Q_SCALE = DIFF_DH ** -0.5 * LOG2_E
MAX_SCORE_RANGE_LOG2 = 100.0

LANES = 128
S5_CHUNK = LANES
S5_FLAT = SSM_GROUP * S5_CHUNK
PROJ_COLS = (N_IN_PIECES - 2) * BRANCH_W + N_BRANCH * D_MODEL
P_RET_Q, P_RET_K, P_RET_V, P_RET_G, P_DIF_Q, P_DIF_K, P_DIF_V, P_DIF_G, P_MEM_Q, P_MEM_G, P_MERGE = range(11)

VMEM_LIMIT = 56 * 1024 * 1024

NT_DIMS = (((1,), (1,)), ((), ()))
TN_DIMS = (((0,), (0,)), ((), ()))


def _params(*sem):
    return pltpu.CompilerParams(dimension_semantics=sem, vmem_limit_bytes=VMEM_LIMIT)


def _rms(x, eps=EPS):
    return x * lax.rsqrt(jnp.mean(x * x, axis=-1, keepdims=True) + eps)


def _sigmoid(x):
    return 0.5 * jnp.tanh(0.5 * x) + 0.5


def _silu_of_half(xh):
    return xh * (1.0 + jnp.tanh(xh))


def _in_proj_kernel(x_ref, g_ref, w_ref, o_ref, h_ref):
    @pl.when(pl.program_id(1) == 0)
    def _():
        h_ref[...] = (_rms(x_ref[...]) * g_ref[...]).astype(BF16)

    o_ref[...] = jnp.dot(h_ref[...], w_ref[...], preferred_element_type=F32).astype(o_ref.dtype)


def _in_proj(x2, g, w_bf, tm, tn):
    n, d = x2.shape
    cols = w_bf.shape[1]
    return pl.pallas_call(
        _in_proj_kernel,
        grid=(n // tm, cols // tn),
        in_specs=[pl.BlockSpec((tm, d), lambda i, j: (i, 0)),
                  pl.BlockSpec((1, d), lambda i, j: (0, 0)),
                  pl.BlockSpec((d, tn), lambda i, j: (0, j))],
        out_specs=pl.BlockSpec((tm, tn), lambda i, j: (i, j)),
        out_shape=jax.ShapeDtypeStruct((n, cols), BF16),
        scratch_shapes=[pltpu.VMEM((tm, d), BF16)],
        compiler_params=_params("parallel", "arbitrary"),
        name="in_proj",
    )(x2, g, w_bf)


def _in_proj_t_kernel(x_ref, g_ref, wt_ref, o_ref):
    h = (_rms(x_ref[...]) * g_ref[...]).astype(BF16)
    o_ref[...] = lax.dot_general(wt_ref[...], h, NT_DIMS, preferred_element_type=F32).astype(o_ref.dtype)


def _in_proj_t(x2, g, wt_bf, tm):
    n, d = x2.shape
    rows = wt_bf.shape[0]
    return pl.pallas_call(
        _in_proj_t_kernel,
        grid=(n // tm,),
        in_specs=[pl.BlockSpec((tm, d), lambda i: (i, 0)),
                  pl.BlockSpec((1, d), lambda i: (0, 0)),
                  pl.BlockSpec((rows, d), lambda i: (0, 0))],
        out_specs=pl.BlockSpec((rows, tm), lambda i: (0, i)),
        out_shape=jax.ShapeDtypeStruct((rows, n), BF16),
        compiler_params=_params("parallel"),
        name="in_proj_t",
    )(x2, g, wt_bf)


def _rope_table_kernel(inv_ref, sgn_ref, cos_ref, sin_ref):
    ts = cos_ref.shape[0]
    pos = (lax.broadcasted_iota(jnp.int32, (ts, LANES), 0) + pl.program_id(1) * ts).astype(F32)
    ang = pos * inv_ref[...]
    cos_ref[...] = jnp.cos(ang)
    sin_ref[...] = jnp.sin(ang) * sgn_ref[...]


def _rope_tables(seq, inv, sgn, ts):
    spec_in = pl.BlockSpec((None, 1, LANES), lambda t, i: (t, 0, 0))
    spec_out = pl.BlockSpec((None, ts, LANES), lambda t, i: (t, i, 0))
    return pl.pallas_call(
        _rope_table_kernel,
        grid=(2, seq // ts),
        in_specs=[spec_in, spec_in],
        out_specs=[spec_out, spec_out],
        out_shape=[jax.ShapeDtypeStruct((2, seq, LANES), F32)] * 2,
        compiler_params=_params("parallel", "parallel"),
        name="rope_tables",
    )(inv, sgn)


def _qk_prep_kernel(rq_ref, rk_ref, dq_ref, dk_ref, cos_ref, sin_ref, qg_ref, kg_ref, perm_ref,
                    orq_ref, ork_ref, odq_ref, odk_ref):
    two = lambda t: jnp.concatenate([t, t], axis=1)
    cos_d, sin_d, cos_r, sin_r = two(cos_ref[0]), two(sin_ref[0]), two(cos_ref[1]), two(sin_ref[1])
    k_scale = HEAD_W ** -0.5
    cos_rk, sin_rk = cos_r * k_scale, sin_r * k_scale
    q_cos, q_sin = cos_d * (two(qg_ref[0]) * DIFF_Q_SCALE), sin_d * (two(qg_ref[1]) * DIFF_Q_SCALE)
    k_cos, k_sin = cos_d * two(kg_ref[0]), sin_d * two(kg_ref[1])

    def shuffled(x, which):
        return jnp.dot(x, perm_ref[which], preferred_element_type=F32)

    def rope_r(x, c, s):
        return (x.astype(F32) * c + shuffled(x, 0) * s).astype(BF16)

    def norm_rope_d(x, c, s):
        xf = x.astype(F32)
        ms = shuffled((xf * xf).astype(BF16), 2) * (1.0 / DIFF_DH)
        return (lax.rsqrt(ms + EPS) * (xf * c + shuffled(x, 1) * s)).astype(BF16)

    for hp in range(HEADS // 2):
        sl = slice(hp * 2 * HEAD_W, (hp + 1) * 2 * HEAD_W)
        orq_ref[:, sl] = rope_r(rq_ref[:, sl], cos_r, sin_r)
        ork_ref[:, sl] = rope_r(rk_ref[:, sl], cos_rk, sin_rk)
        odq_ref[:, sl] = norm_rope_d(dq_ref[:, sl], q_cos, q_sin)
        odk_ref[:, sl] = norm_rope_d(dk_ref[:, sl], k_cos, k_sin)


def _qk_prep_consts():
    lane = np.arange(LANES)
    half_rot = (lane + HEAD_W // 2) % LANES
    partner = np.where(lane % DIFF_DH < DIFF_DH // 2, lane + DIFF_DH // 2, lane - DIFF_DH // 2)
    perms = np.zeros((3, LANES, LANES), np.float32)
    perms[0, half_rot, lane] = 1.0
    perms[1, partner, lane] = 1.0
    perms[2] = (lane[:, None] // DIFF_DH) == (lane[None, :] // DIFF_DH)
    two_heads = np.zeros((3, 2 * LANES, 2 * LANES), np.float32)
    two_heads[:, :LANES, :LANES] = perms
    two_heads[:, LANES:, LANES:] = perms
    return jnp.asarray(two_heads, BF16), partner


def _qk_prep(proj, cos_t, sin_t, qg, kg, seq, ts):
    n = proj.shape[0]
    nsb = seq // ts
    perms, partner = _qk_prep_consts()
    with_partner = lambda g: jnp.stack([jnp.tile(g, 2), jnp.tile(g, 2)[partner]])[:, None, :]

    def piece(p):
        return pl.BlockSpec((ts, BRANCH_W), lambda i: (i, p))

    tab = pl.BlockSpec((2, ts, LANES), lambda i: (0, i % nsb, 0))
    vec = pl.BlockSpec((2, 1, LANES), lambda i: (0, 0, 0))
    out = pl.BlockSpec((ts, BRANCH_W), lambda i: (i, 0))
    qg, kg = with_partner(qg), with_partner(kg)
    return pl.pallas_call(
        _qk_prep_kernel,
        grid=(n // ts,),
        in_specs=[piece(P_RET_Q), piece(P_RET_K), piece(P_DIF_Q), piece(P_DIF_K), tab, tab, vec, vec,
                  pl.BlockSpec(perms.shape, lambda i: (0, 0, 0))],
        out_specs=[out] * 4,
        out_shape=[jax.ShapeDtypeStruct((n, BRANCH_W), BF16)] * 4,
        compiler_params=_params("parallel"),
        name="qk_prep",
    )(proj, proj, proj, proj, cos_t, sin_t, qg, kg, perms)


def _s5_prep_kernel(lam_c_ref, lam_r_ref, b_ref, bt_ref, c_ref, ct_ref,
                    m_ref, p_ref, q_ref, dec_ref, kern_ref):
    T = S5_CHUNK

    def disc(lam_re, lam_im, log_dt):
        lr = jnp.minimum(lam_re, -1e-4)
        li = lam_im
        dt = jnp.exp(log_dt)
        mag = jnp.exp(lr * dt)
        ab_re = mag * jnp.cos(li * dt)
        ab_im = mag * jnp.sin(li * dt)
        nr = ab_re - 1.0
        ni = ab_im
        den = lr * lr + li * li
        f_re = (nr * lr + ni * li) / den
        f_im = (ni * lr - nr * li) / den
        return lr * dt, li * dt, f_re, f_im

    def powers(ldt_re, ldt_im, k):
        mag = jnp.exp(k * ldt_re)
        return mag * jnp.cos(k * ldt_im), mag * jnp.sin(k * ldt_im)

    ldr, ldi, f_re, f_im = disc(lam_c_ref[0], lam_c_ref[1], lam_c_ref[2])
    b_re, b_im = b_ref[0], b_ref[1]
    bb_re = f_re * b_re - f_im * b_im
    bb_im = f_re * b_im + f_im * b_re
    k_row = lax.broadcasted_iota(jnp.int32, (1, T), 1).astype(F32)
    pw_re, pw_im = powers(ldr, ldi, k_row)
    pw1_re, pw1_im = powers(ldr, ldi, k_row + 1.0)
    c_re, c_im = c_ref[0], c_ref[1]
    ct_re, ct_im = ct_ref[0], ct_ref[1]

    for c in range(SSM_GROUP):
        w_re = pw_re * bb_re[:, c:c + 1] - pw_im * bb_im[:, c:c + 1]
        w_im = pw_re * bb_im[:, c:c + 1] + pw_im * bb_re[:, c:c + 1]
        kern = (jnp.dot(c_re, w_re, preferred_element_type=F32, precision=lax.Precision.HIGHEST)
                - jnp.dot(c_im, w_im, preferred_element_type=F32, precision=lax.Precision.HIGHEST))
        for co in range(SSM_GROUP):
            kern_ref[c, :, co * T:(co + 1) * T] = kern[co:co + 1, :]
        q_ref[0, :, c * T:(c + 1) * T] = (ct_re[:, c:c + 1] * pw1_re - ct_im[:, c:c + 1] * pw1_im).astype(q_ref.dtype)
        q_ref[1, :, c * T:(c + 1) * T] = (-ct_re[:, c:c + 1] * pw1_im - ct_im[:, c:c + 1] * pw1_re).astype(q_ref.dtype)

    jj = lax.broadcasted_iota(jnp.int32, (T, S5_FLAT), 0)
    tt = lax.broadcasted_iota(jnp.int32, (T, S5_FLAT), 1) % T
    causal = tt >= jj

    def toeplitz_rows(c, carry):
        row = jnp.broadcast_to(kern_ref[c], (T, S5_FLAT))
        toep = pltpu.roll(row, 0, 1, stride=1, stride_axis=0)
        m_ref[pl.ds(pl.multiple_of(c * T, T), T), :] = jnp.where(causal, toep, 0.0).astype(m_ref.dtype)
        return carry

    lax.fori_loop(0, SSM_GROUP, toeplitz_rows, 0)

    ldr_r, ldi_r, fr_r, fi_r = disc(lam_r_ref[0], lam_r_ref[1], lam_r_ref[2])
    bt_re, bt_im = bt_ref[0], bt_ref[1]
    bbt_re = fr_r * bt_re - fi_r * bt_im
    bbt_im = fr_r * bt_im + fi_r * bt_re
    k_col = (T - 1.0) - lax.broadcasted_iota(jnp.int32, (T, 1), 0).astype(F32)
    pr_re, pr_im = powers(ldr_r, ldi_r, k_col)
    for c in range(SSM_GROUP):
        p_ref[0, c * T:(c + 1) * T, :] = (pr_re * bbt_re[c:c + 1, :] - pr_im * bbt_im[c:c + 1, :]).astype(p_ref.dtype)
        p_ref[1, c * T:(c + 1) * T, :] = (pr_re * bbt_im[c:c + 1, :] + pr_im * bbt_re[c:c + 1, :]).astype(p_ref.dtype)
    d_re, d_im = powers(ldr_r, ldi_r, float(T))
    dec_ref[0] = d_re
    dec_ref[1] = d_im


def _s5_prep(lam_re, lam_im, log_dt, b_re, b_im, c_re, c_im):
    g, p = lam_re.shape
    ldt = jnp.broadcast_to(log_dt[:, None], (g, p))
    lam = jnp.stack([lam_re, lam_im, ldt], axis=1)
    lam_c = lam[..., None]
    lam_r = lam[:, :, None, :]
    b = jnp.stack([b_re, b_im], axis=1)
    bt = jnp.swapaxes(b, 2, 3)
    c = jnp.stack([c_re, c_im], axis=1)
    ct = jnp.swapaxes(c, 2, 3)

    def spec(shape):
        return pl.BlockSpec((None,) + shape, lambda i: (i,) + (0,) * len(shape))

    return pl.pallas_call(
        _s5_prep_kernel,
        grid=(g,),
        in_specs=[spec((3, p, 1)), spec((3, 1, p)), spec((2, p, SSM_GROUP)), spec((2, SSM_GROUP, p)),
                  spec((2, SSM_GROUP, p)), spec((2, p, SSM_GROUP))],
        out_specs=[spec((S5_FLAT, S5_FLAT)), spec((2, S5_FLAT, p)), spec((2, p, S5_FLAT)), spec((2, 1, p))],
        out_shape=[jax.ShapeDtypeStruct((g, S5_FLAT, S5_FLAT), BF16),
                   jax.ShapeDtypeStruct((g, 2, S5_FLAT, p), BF16),
                   jax.ShapeDtypeStruct((g, 2, p, S5_FLAT), BF16),
                   jax.ShapeDtypeStruct((g, 2, 1, p), F32)],
        scratch_shapes=[pltpu.VMEM((SSM_GROUP, 1, S5_FLAT), F32)],
        compiler_params=_params("parallel"),
        name="s5_prep",
    )(lam_c, lam_r, b, bt, c, ct)


def _s5_conv_kernel(d_ref, u_ref, m_ref, p_ref, q_ref, dec_ref, y_ref, v_ref, s_ref, *, batch, n_chunks):
    g = pl.program_id(0)
    T = S5_CHUNK
    u_flat = jnp.concatenate([u_ref[c] for c in range(SSM_GROUP)], axis=-1)
    rows = batch * n_chunks
    for ri in range(2):
        v_ref[ri] = jnp.dot(u_flat, p_ref[ri], preferred_element_type=F32).reshape(batch, n_chunks, SSM_STATE)
    dr, di = dec_ref[0], dec_ref[1]

    def body(ch, carry):
        sr, si = carry
        s_ref[0, :, pl.ds(ch, 1), :] = sr
        s_ref[1, :, pl.ds(ch, 1), :] = si
        vr = v_ref[0, :, pl.ds(ch, 1), :]
        vi = v_ref[1, :, pl.ds(ch, 1), :]
        return dr * sr - di * si + vr, dr * si + di * sr + vi

    zero = jnp.zeros((batch, 1, SSM_STATE), F32)
    lax.fori_loop(0, n_chunks, body, (zero, zero))
    y = jnp.dot(u_flat, m_ref[...], preferred_element_type=F32)
    for ri in range(2):
        s_start = s_ref[ri].reshape(rows, SSM_STATE).astype(BF16)
        y += jnp.dot(s_start, q_ref[ri], preferred_element_type=F32)
    for c in range(SSM_GROUP):
        y_ref[c] = y[:, c * T:(c + 1) * T] + d_ref[g * SSM_GROUP + c] * u_ref[c].astype(F32)


def _s5_conv(u_t, d_skip, m, p, q, dec, batch, seq):
    n = u_t.shape[1]
    r = n // S5_CHUNK
    u3 = u_t.reshape(u_t.shape[0], r, S5_CHUNK)
    g = SSM_GROUPS

    def spec(shape):
        return pl.BlockSpec((None,) + shape, lambda i: (i,) + (0,) * len(shape))

    y3 = pl.pallas_call(
        functools.partial(_s5_conv_kernel, batch=batch, n_chunks=seq // S5_CHUNK),
        grid=(g,),
        in_specs=[pl.BlockSpec(memory_space=pltpu.SMEM),
                  pl.BlockSpec((SSM_GROUP, r, S5_CHUNK), lambda i: (i, 0, 0)),
                  spec((S5_FLAT, S5_FLAT)), spec((2, S5_FLAT, SSM_STATE)), spec((2, SSM_STATE, S5_FLAT)),
                  spec((2, 1, SSM_STATE))],
        out_specs=pl.BlockSpec((SSM_GROUP, r, S5_CHUNK), lambda i: (i, 0, 0)),
        out_shape=jax.ShapeDtypeStruct((BRANCH_W, r, S5_CHUNK), F32),
        scratch_shapes=[pltpu.VMEM((2, batch, seq // S5_CHUNK, SSM_STATE), F32)] * 2,
        compiler_params=_params("parallel"),
        name="s5_conv",
    )(d_skip, u3, m, p, q, dec)
    return y3.reshape(BRANCH_W, n)


def _retention_kernel(q_ref, k_ref, v_ref, mask_ref, qd_ref, kd_ref, cd_ref, o_ref, st_ref):
    @pl.when(pl.program_id(1) == 0)
    def _():
        st_ref[...] = jnp.zeros_like(st_ref)

    for bi in range(q_ref.shape[0]):
        for h in range(HEADS):
            sl = slice(h * HEAD_W, (h + 1) * HEAD_W)
            q, k, v = q_ref[bi, :, sl], k_ref[bi, :, sl], v_ref[bi, :, sl]
            scores = lax.dot_general(q, k, NT_DIMS, preferred_element_type=F32) * mask_ref[h]
            st = st_ref[bi, h]
            o = (jnp.dot(scores.astype(BF16), v, preferred_element_type=F32)
                 + jnp.dot(q, st.astype(BF16), preferred_element_type=F32) * qd_ref[h])
            kd = (k.astype(F32) * kd_ref[h]).astype(BF16)
            st_ref[bi, h] = st * cd_ref[h] + lax.dot_general(kd, v, TN_DIMS, preferred_element_type=F32)
            o_ref[bi, :, sl] = _rms(o).astype(o_ref.dtype)


def _retention_consts(blk):
    log_gamma = np.log1p(-(2.0 ** (-5.0 - np.arange(HEADS, dtype=np.float64))))
    idx = np.arange(blk, dtype=np.float64)
    diff = idx[:, None] - idx[None, :]
    same = (np.arange(blk)[:, None] // CHUNK) == (np.arange(blk)[None, :] // CHUNK)
    past = (np.arange(blk)[None, :] // CHUNK) < (np.arange(blk)[:, None] // CHUNK)
    expo = np.where(same, np.abs(diff), diff)
    mask = np.where(same | past, np.exp(expo[None] * log_gamma[:, None, None]), 0.0)
    q_dec = np.exp((idx + 1.0)[None, :] * log_gamma[:, None])
    k_dec = np.exp((blk - 1.0 - idx)[None, :] * log_gamma[:, None])
    c_dec = np.exp(blk * log_gamma)
    bc = lambda a: np.broadcast_to(a[:, :, None], (HEADS, blk, HEAD_W))
    return (jnp.asarray(mask, F32), jnp.asarray(bc(q_dec), F32), jnp.asarray(bc(k_dec), F32),
            jnp.asarray(np.broadcast_to(c_dec[:, None, None], (HEADS, 1, HEAD_W)), F32))


def _retention(rq, rk, proj, batch, seq, blk, bpb):
    n = rq.shape[0]
    nb = seq // blk
    mask, q_dec, k_dec, c_dec = _retention_consts(blk)
    tok = pl.BlockSpec((bpb, blk, BRANCH_W), lambda b, l: (b, l, 0))
    full = lambda a: pl.BlockSpec(a.shape, lambda b, l: (0,) * a.ndim)
    as3 = lambda a: a.reshape(batch, seq, a.shape[-1])
    out = pl.pallas_call(
        _retention_kernel,
        grid=(batch // bpb, nb),
        in_specs=[tok, tok, pl.BlockSpec((bpb, blk, BRANCH_W), lambda b, l: (b, l, P_RET_V)),
                  full(mask), full(q_dec), full(k_dec), full(c_dec)],
        out_specs=tok,
        out_shape=jax.ShapeDtypeStruct((batch, seq, BRANCH_W), BF16),
        scratch_shapes=[pltpu.VMEM((bpb, HEADS, HEAD_W, HEAD_W), F32)],
        compiler_params=_params("parallel", "arbitrary"),
        name="retention",
    )(as3(rq), as3(rk), as3(proj), mask, q_dec, k_dec, c_dec)
    return out.reshape(n, BRANCH_W)


def _diff_attn_kernel(bound_ref, q_ref, k_ref, v_ref, lq1_ref, lk1_ref, lq2_ref, lk2_ref, hg_ref, o_ref,
                      acc_ref, *stat_refs, tq, tk, lambda_init, fixed_stabiliser):
    i = pl.program_id(2)
    n_sub = tq // tk
    q = q_ref[...]
    lane = lax.broadcasted_iota(jnp.int32, (tq, HEAD_W), 1)
    qs = (jnp.where(lane < DIFF_DH, q, jnp.zeros_like(q)), jnp.where(lane < DIFF_DH, jnp.zeros_like(q), q))
    acc_ref[...] = jnp.zeros_like(acc_ref)
    if fixed_stabiliser:
        shift = bound_ref[0, 0]
    else:
        m_ref, = stat_refs
        m_ref[...] = jnp.full_like(m_ref, NEG_INF)
    qc = lax.broadcasted_iota(jnp.int32, (tq, tk), 0) // CHUNK
    kc = lax.broadcasted_iota(jnp.int32, (tq, tk), 1) // CHUNK

    def tile(j, diag):
        rows = pl.ds(pl.multiple_of(j * tk, tk), tk)
        kt = k_ref[rows, :]
        vt = v_ref[rows, :]
        v_ones = jnp.concatenate([vt, jnp.ones_like(vt)], axis=1)
        r0 = 0 if diag is None else diag * tk
        visible = None if diag is None else ((kc + diag * (tk // CHUNK)) <= qc)[r0:]
        for mp in range(2):
            s = lax.dot_general(qs[mp][r0:], kt, NT_DIMS, preferred_element_type=F32)
            if fixed_stabiliser:
                p = jnp.exp2(s - shift)
                if visible is not None:
                    p = jnp.where(visible, p, 0.0)
                acc_ref[mp, r0:] += jnp.dot(p.astype(BF16), v_ones, preferred_element_type=F32)
            else:
                if visible is not None:
                    s = jnp.where(visible, s, NEG_INF)
                m_prev = m_ref[mp, r0:]
                m_new = jnp.maximum(m_prev, jnp.max(s, axis=-1, keepdims=True))
                p = jnp.exp2(s - m_new)
                acc_ref[mp, r0:] = (jnp.exp2(m_prev - m_new) * acc_ref[mp, r0:]
                                    + jnp.dot(p.astype(BF16), v_ones, preferred_element_type=F32))
                m_ref[mp, r0:] = m_new

    def body(jj, carry):
        for u in range(n_sub):
            tile(jj * n_sub + u, None)
        return carry

    lax.fori_loop(0, i, body, 0)
    for d in range(n_sub):
        tile(i * n_sub + d, d)

    lam = (jnp.exp(jnp.sum(lq1_ref[...] * lk1_ref[...], axis=-1, keepdims=True))
           - jnp.exp(jnp.sum(lq2_ref[...] * lk2_ref[...], axis=-1, keepdims=True)) + lambda_init)
    o = (acc_ref[0, :, :HEAD_W] / acc_ref[0, :, HEAD_W:]
         - lam * (acc_ref[1, :, :HEAD_W] / acc_ref[1, :, HEAD_W:]))
    o_ref[...] = (_rms(o) * hg_ref[...] * (1.0 - lambda_init)).astype(o_ref.dtype)


def _diff_attn(dq, dk, proj, qg, kg, lq1, lk1, lq2, lk2, hg, batch, seq, tq, tk, lambda_init):
    n = dq.shape[0]
    nq = seq // tq
    bound = (1.01 * DIFF_DH * DIFF_Q_SCALE * jnp.max(jnp.abs(qg)) * jnp.max(jnp.abs(kg))).reshape(1, 1)
    vec64 = pl.BlockSpec((1, DIFF_DH), lambda b, h, i: (0, 0))

    def call(fixed_stabiliser):
        stat_scratch = [] if fixed_stabiliser else [pltpu.VMEM((2, tq, 1), F32)]
        return pl.pallas_call(
            functools.partial(_diff_attn_kernel, tq=tq, tk=tk, lambda_init=lambda_init,
                              fixed_stabiliser=fixed_stabiliser),
            grid=(batch, HEADS, nq),
            in_specs=[pl.BlockSpec(memory_space=pltpu.SMEM),
                      pl.BlockSpec((tq, HEAD_W), lambda b, h, i: (b * nq + i, h)),
                      pl.BlockSpec((seq, HEAD_W), lambda b, h, i: (b, h)),
                      pl.BlockSpec((seq, HEAD_W), lambda b, h, i: (b, P_DIF_V * HEADS + h)),
                      vec64, vec64, vec64, vec64,
                      pl.BlockSpec((1, HEAD_W), lambda b, h, i: (0, 0))],
            out_specs=pl.BlockSpec((tq, HEAD_W), lambda b, h, i: (b * nq + i, h)),
            out_shape=jax.ShapeDtypeStruct((n, BRANCH_W), BF16),
            scratch_shapes=[pltpu.VMEM((2, tq, 2 * HEAD_W), F32)] + stat_scratch,
            compiler_params=_params("parallel", "parallel", "arbitrary"),
            name="diff_attn_fixed" if fixed_stabiliser else "diff_attn_online",
        )(bound, dq, dk, proj, lq1, lk1, lq2, lk2, hg)

    return lax.cond(bound[0, 0] * 2.0 <= MAX_SCORE_RANGE_LOG2, lambda: call(True), lambda: call(False))


def _mem_kv_kernel(mem_ref, g_ref, w_ref, kg_ref, k_ref, v_ref):
    h = (_rms(mem_ref[...]) * g_ref[...]).astype(BF16)
    kv = jnp.dot(h, w_ref[...], preferred_element_type=F32)
    for hd in range(HEADS):
        sl = slice(hd * HEAD_W, (hd + 1) * HEAD_W)
        k_ref[:, sl] = (_rms(kv[:, sl]) * kg_ref[...]).astype(BF16)
    v_ref[...] = kv[:, BRANCH_W:].astype(BF16)


def _mem_kv(mem2, g, w_bf, kg, mem_len):
    n, d = mem2.shape
    out = pl.BlockSpec((mem_len, BRANCH_W), lambda i: (i, 0))
    return pl.pallas_call(
        _mem_kv_kernel,
        grid=(n // mem_len,),
        in_specs=[pl.BlockSpec((mem_len, d), lambda i: (i, 0)),
                  pl.BlockSpec((1, d), lambda i: (0, 0)),
                  pl.BlockSpec((d, 2 * BRANCH_W), lambda i: (0, 0)),
                  pl.BlockSpec((1, HEAD_W), lambda i: (0, 0))],
        out_specs=[out, out],
        out_shape=[jax.ShapeDtypeStruct((n, BRANCH_W), BF16)] * 2,
        compiler_params=_params("parallel"),
        name="mem_kv",
    )(mem2, g, w_bf, kg)


def _merge_kernel(x_ref, y_ref, sg_ref, ret_ref, rg_ref, dif_ref, dg_ref, mq_ref, mg_ref, mk_ref, mv_ref,
                  mrg0_ref, mrg1_ref, mrg2_ref, mrg3_ref, wglu_ref, bglu_ref, mqg_ref, wbr_ref, bm_ref, wout_ref,
                  o_ref):
    z = jax.nn.gelu(y_ref[...])
    glu = jnp.dot(wglu_ref[...], z.astype(BF16), preferred_element_type=F32) + bglu_ref[...]
    a_t = (z * _sigmoid(glu)).astype(BF16) * _silu_of_half(sg_ref[...])
    branches = [lax.dot_general(a_t, wbr_ref[0], TN_DIMS, preferred_element_type=F32)]
    b_out = ret_ref[...] * _silu_of_half(rg_ref[...])
    branches.append(jnp.dot(b_out, wbr_ref[1], preferred_element_type=F32))
    c_out = dif_ref[...] * _silu_of_half(dg_ref[...])
    branches.append(jnp.dot(c_out, wbr_ref[2], preferred_element_type=F32))
    heads = []
    for hd in range(HEADS):
        sl = slice(hd * HEAD_W, (hd + 1) * HEAD_W)
        qh = (_rms(mq_ref[:, sl].astype(F32)) * mqg_ref[...] * (HEAD_W ** -0.5)).astype(BF16)
        s = lax.dot_general(qh, mk_ref[:, sl], NT_DIMS, preferred_element_type=F32)
        p = jnp.exp(s - jnp.max(s, axis=-1, keepdims=True))
        o = jnp.dot(p.astype(BF16), mv_ref[:, sl], preferred_element_type=F32)
        heads.append(o / jnp.sum(p, axis=-1, keepdims=True))
    m_out = jnp.concatenate(heads, axis=-1).astype(BF16) * _silu_of_half(mg_ref[...])
    branches.append(jnp.dot(m_out, wbr_ref[3], preferred_element_type=F32))
    merged = None
    for nb, mrg_ref in enumerate((mrg0_ref, mrg1_ref, mrg2_ref, mrg3_ref)):
        gate = 1.0 + jnp.tanh(mrg_ref[...] + bm_ref[nb:nb + 1, :].astype(BF16))
        term = gate * branches[nb].astype(BF16)
        merged = term if merged is None else merged + term
    o_ref[...] = x_ref[...] + jnp.dot(merged, wout_ref[...], preferred_element_type=F32)


def _merge(x2, y_t, u_t, proj, ret_o, dif_o, mk, mv, wglu_t, bglu, mqg, wbr, bm, wout, seq, mem_len, tm):
    n, d = x2.shape

    def piece(p):
        return pl.BlockSpec((tm, BRANCH_W), lambda i: (i, p))

    def gate_piece(nb):
        return pl.BlockSpec((tm, D_MODEL), lambda i: (i, P_MERGE * BRANCH_W // D_MODEL + nb))

    tok = pl.BlockSpec((tm, BRANCH_W), lambda i: (i, 0))
    mem = pl.BlockSpec((mem_len, BRANCH_W), lambda i: ((i * tm) // seq, 0))
    full = lambda a: pl.BlockSpec(a.shape, lambda i: (0,) * a.ndim)
    return pl.pallas_call(
        _merge_kernel,
        grid=(n // tm,),
        in_specs=[pl.BlockSpec((tm, d), lambda i: (i, 0)),
                  pl.BlockSpec((BRANCH_W, tm), lambda i: (0, i)),
                  pl.BlockSpec((BRANCH_W, tm), lambda i: (1, i)),
                  tok, piece(P_RET_G), tok, piece(P_DIF_G), piece(P_MEM_Q), piece(P_MEM_G), mem, mem,
                  gate_piece(0), gate_piece(1), gate_piece(2), gate_piece(3),
                  full(wglu_t), full(bglu), full(mqg), full(wbr), full(bm), full(wout)],
        out_specs=pl.BlockSpec((tm, d), lambda i: (i, 0)),
        out_shape=jax.ShapeDtypeStruct((n, d), F32),
        compiler_params=_params("parallel"),
        name="merge",
    )(x2, y_t, u_t, ret_o, proj, dif_o, proj, proj, proj, mk, mv, proj, proj, proj, proj,
      wglu_t, bglu, mqg, wbr, bm, wout)


def _tile(n, pref):
    return pref if n % pref == 0 else n


def kernel(x, mem, norm_g, w_in, ssm_lambda_re, ssm_lambda_im, ssm_log_dt, ssm_b_re, ssm_b_im, ssm_c_re,
           ssm_c_im, ssm_d, ssm_w_glu, ssm_b_glu, diff_q_norm_g, diff_k_norm_g, diff_lambda_q1,
           diff_lambda_k1, diff_lambda_q2, diff_lambda_k2, diff_head_norm_g, mem_norm_g, w_mem_kv,
           mem_q_norm_g, mem_k_norm_g, w_branch, b_merge, w_out):
    batch, seq, d = x.shape
    mem_len = mem.shape[1]
    depth = w_in.shape[0]
    n = batch * seq
    assert d == D_MODEL and seq % S5_CHUNK == 0

    diff_inv = ROPE_THETA ** (-jnp.arange(0, DIFF_DH, 2, dtype=F32) / DIFF_DH)
    ret_inv = 1.0 / (ROPE_THETA ** jnp.linspace(0.0, 1.0, HEAD_W // 2, dtype=F32))
    inv = jnp.stack([jnp.tile(diff_inv, LANES // diff_inv.shape[0]),
                     jnp.tile(ret_inv, LANES // ret_inv.shape[0])])[:, None, :]
    lane = np.arange(LANES)
    sgn = np.stack([np.where(lane % DIFF_DH < DIFF_DH // 2, -1.0, 1.0),
                    np.where(lane < HEAD_W // 2, -1.0, 1.0)])[:, None, :]
    cos_t, sin_t = _rope_tables(seq, inv, jnp.asarray(sgn, F32), _tile(seq, 1024))

    piece = np.arange(w_in.shape[2]) // BRANCH_W
    halved = (piece >= N_IN_PIECES) | np.isin(piece, IN_GATE_PIECES)
    in_col_scale = jnp.asarray(np.where(halved, GATE_FOLD, 1.0), F32)

    x2 = x.reshape(n, d)
    mem2 = mem.reshape(batch * mem_len, d)
    for l in range(depth):
        lambda_init = 0.8 - 0.6 * math.exp(-0.3 * l)
        g = norm_g[l][None, :]
        w = w_in[l] * in_col_scale[None, :]
        proj = _in_proj(x2, g, w[:, 2 * BRANCH_W:].astype(BF16), _tile(n, 2048), 1536)
        u_t = _in_proj_t(x2, g, w[:, :2 * BRANCH_W].T.astype(BF16), _tile(n, 1024))

        m_op, p_op, q_op, dec = _s5_prep(ssm_lambda_re[l], ssm_lambda_im[l], ssm_log_dt[l], ssm_b_re[l],
                                         ssm_b_im[l], ssm_c_re[l], ssm_c_im[l])
        y_t = _s5_conv(u_t, ssm_d[l], m_op, p_op, q_op, dec, batch, seq)

        rq, rk, dq, dk = _qk_prep(proj, cos_t, sin_t, diff_q_norm_g[l], diff_k_norm_g[l], seq, _tile(seq, 1024))
        ret_o = _retention(rq, rk, proj, batch, seq, _tile(seq, 256), 2 if batch % 2 == 0 else 1)
        dif_o = _diff_attn(dq, dk, proj, diff_q_norm_g[l], diff_k_norm_g[l], diff_lambda_q1[l][None],
                           diff_lambda_k1[l][None], diff_lambda_q2[l][None], diff_lambda_k2[l][None],
                           diff_head_norm_g[l][None], batch, seq, _tile(seq, 2048), _tile(seq, 512), lambda_init)
        mk, mv = _mem_kv(mem2, mem_norm_g[l][None, :], w_mem_kv[l].astype(BF16), mem_k_norm_g[l][None], mem_len)
        x2 = _merge(x2, y_t, u_t, proj, ret_o, dif_o, mk, mv, ssm_w_glu[l].T.astype(BF16),
                    ssm_b_glu[l][:, None], mem_q_norm_g[l][None], w_branch[l].astype(BF16), GATE_FOLD * b_merge[l],
                    (GATE_FOLD * w_out[l]).astype(BF16), seq, mem_len, _tile(seq, 512))
    return x2.reshape(batch, seq, d)
```

```python
import functools
import math

import numpy as np
import jax
import jax.numpy as jnp
from jax import lax
from jax.experimental import pallas as pl
from jax.experimental.pallas import tpu as pltpu

F32 = jnp.float32
BF16 = jnp.bfloat16

D_MODEL = 1024
BRANCH_W = D_MODEL // 2
N_BRANCH = 4
N_IN_PIECES = 12
CHUNK = 64
SSM_GROUP = 16
SSM_GROUPS = BRANCH_W // SSM_GROUP
SSM_STATE = 64
HEADS = 4
HEAD_W = BRANCH_W // HEADS
DIFF_DH = HEAD_W // 2
ROPE_THETA = 10000.0
EPS = 1e-6
NEG_INF = -1e30
GATE_FOLD = 0.5
IN_GATE_PIECES = (1, 5, 9, 11)
LOG2_E = math.log2(math.e)
DIFF_Q_SCALE = DIFF_DH ** -0.5 * LOG2_E
MAX_SCORE_RANGE_LOG2 = 100.0

LANES = 128
S5_CHUNK = LANES
S5_FLAT = SSM_GROUP * S5_CHUNK
PROJ_COLS = (N_IN_PIECES - 2) * BRANCH_W + N_BRANCH * D_MODEL
P_RET_Q, P_RET_K, P_RET_V, P_RET_G, P_DIF_Q, P_DIF_K, P_DIF_V, P_DIF_G, P_MEM_Q, P_MEM_G, P_MERGE = range(11)

VMEM_LIMIT = 56 * 1024 * 1024

NT_DIMS = (((1,), (1,)), ((), ()))
TN_DIMS = (((0,), (0,)), ((), ()))


def _params(*sem):
    return pltpu.CompilerParams(dimension_semantics=sem, vmem_limit_bytes=VMEM_LIMIT)


def _rms(x, eps=EPS):
    return x * lax.rsqrt(jnp.mean(x * x, axis=-1, keepdims=True) + eps)


def _sigmoid(x):
    return 0.5 * jnp.tanh(0.5 * x) + 0.5


def _silu_of_half(xh):
    return xh * (1.0 + jnp.tanh(xh))


def _in_proj_kernel(x_ref, g_ref, w_ref, o_ref, h_ref):
    @pl.when(pl.program_id(1) == 0)
    def _():
        h_ref[...] = (_rms(x_ref[...]) * g_ref[...]).astype(BF16)

    o_ref[...] = jnp.dot(h_ref[...], w_ref[...], preferred_element_type=F32).astype(o_ref.dtype)


def _in_proj(x2, g, w_bf, tm, tn):
    n, d = x2.shape
    cols = w_bf.shape[1]
    return pl.pallas_call(
        _in_proj_kernel,
        grid=(n // tm, cols // tn),
        in_specs=[pl.BlockSpec((tm, d), lambda i, j: (i, 0)),
                  pl.BlockSpec((1, d), lambda i, j: (0, 0)),
                  pl.BlockSpec((d, tn), lambda i, j: (0, j))],
        out_specs=pl.BlockSpec((tm, tn), lambda i, j: (i, j)),
        out_shape=jax.ShapeDtypeStruct((n, cols), BF16),
        scratch_shapes=[pltpu.VMEM((tm, d), BF16)],
        compiler_params=_params("parallel", "arbitrary"),
        name="in_proj",
    )(x2, g, w_bf)


def _in_proj_t_kernel(x_ref, g_ref, wt_ref, o_ref):
    h = (_rms(x_ref[...]) * g_ref[...]).astype(BF16)
    o_ref[...] = lax.dot_general(wt_ref[...], h, NT_DIMS, preferred_element_type=F32).astype(o_ref.dtype)


def _in_proj_t(x2, g, wt_bf, tm):
    n, d = x2.shape
    rows = wt_bf.shape[0]
    return pl.pallas_call(
        _in_proj_t_kernel,
        grid=(n // tm,),
        in_specs=[pl.BlockSpec((tm, d), lambda i: (i, 0)),
                  pl.BlockSpec((1, d), lambda i: (0, 0)),
                  pl.BlockSpec((rows, d), lambda i: (0, 0))],
        out_specs=pl.BlockSpec((rows, tm), lambda i: (0, i)),
        out_shape=jax.ShapeDtypeStruct((rows, n), BF16),
        compiler_params=_params("parallel"),
        name="in_proj_t",
    )(x2, g, wt_bf)


def _rope_table_kernel(inv_ref, sgn_ref, cos_ref, sin_ref):
    ts = cos_ref.shape[0]
    pos = (lax.broadcasted_iota(jnp.int32, (ts, LANES), 0) + pl.program_id(1) * ts).astype(F32)
    ang = pos * inv_ref[...]
    cos_ref[...] = jnp.cos(ang)
    sin_ref[...] = jnp.sin(ang) * sgn_ref[...]


def _rope_tables(seq, inv, sgn, ts):
    spec_in = pl.BlockSpec((None, 1, LANES), lambda t, i: (t, 0, 0))
    spec_out = pl.BlockSpec((None, ts, LANES), lambda t, i: (t, i, 0))
    return pl.pallas_call(
        _rope_table_kernel,
        grid=(2, seq // ts),
        in_specs=[spec_in, spec_in],
        out_specs=[spec_out, spec_out],
        out_shape=[jax.ShapeDtypeStruct((2, seq, LANES), F32)] * 2,
        compiler_params=_params("parallel", "parallel"),
        name="rope_tables",
    )(inv, sgn)


def _qk_prep_kernel(rq_ref, rk_ref, dq_ref, dk_ref, cos_ref, sin_ref, qg_ref, kg_ref, perm_ref,
                    orq_ref, ork_ref, odq_ref, odk_ref):
    two = lambda t: jnp.concatenate([t, t], axis=1)
    cos_d, sin_d, cos_r, sin_r = two(cos_ref[0]), two(sin_ref[0]), two(cos_ref[1]), two(sin_ref[1])
    k_scale = HEAD_W ** -0.5
    cos_rk, sin_rk = cos_r * k_scale, sin_r * k_scale
    q_cos, q_sin = cos_d * (two(qg_ref[0]) * DIFF_Q_SCALE), sin_d * (two(qg_ref[1]) * DIFF_Q_SCALE)
    k_cos, k_sin = cos_d * two(kg_ref[0]), sin_d * two(kg_ref[1])

    def shuffled(x, which):
        return jnp.dot(x, perm_ref[which], preferred_element_type=F32)

    def rope_r(x, c, s):
        return (x.astype(F32) * c + shuffled(x, 0) * s).astype(BF16)

    def norm_rope_d(x, c, s):
        xf = x.astype(F32)
        ms = shuffled((xf * xf).astype(BF16), 2) * (1.0 / DIFF_DH)
        return (lax.rsqrt(ms + EPS) * (xf * c + shuffled(x, 1) * s)).astype(BF16)

    for hp in range(HEADS // 2):
        sl = slice(hp * 2 * HEAD_W, (hp + 1) * 2 * HEAD_W)
        orq_ref[:, sl] = rope_r(rq_ref[:, sl], cos_r, sin_r)
        ork_ref[:, sl] = rope_r(rk_ref[:, sl], cos_rk, sin_rk)
        odq_ref[:, sl] = norm_rope_d(dq_ref[:, sl], q_cos, q_sin)
        odk_ref[:, sl] = norm_rope_d(dk_ref[:, sl], k_cos, k_sin)


def _qk_prep_consts():
    lane = np.arange(LANES)
    half_rot = (lane + HEAD_W // 2) % LANES
    partner = np.where(lane % DIFF_DH < DIFF_DH // 2, lane + DIFF_DH // 2, lane - DIFF_DH // 2)
    perms = np.zeros((3, LANES, LANES), np.float32)
    perms[0, half_rot, lane] = 1.0
    perms[1, partner, lane] = 1.0
    perms[2] = (lane[:, None] // DIFF_DH) == (lane[None, :] // DIFF_DH)
    two_heads = np.zeros((3, 2 * LANES, 2 * LANES), np.float32)
    two_heads[:, :LANES, :LANES] = perms
    two_heads[:, LANES:, LANES:] = perms
    return jnp.asarray(two_heads, BF16), partner


def _qk_prep(proj, cos_t, sin_t, qg, kg, seq, ts):
    n = proj.shape[0]
    nsb = seq // ts
    perms, partner = _qk_prep_consts()
    with_partner = lambda g: jnp.stack([jnp.tile(g, 2), jnp.tile(g, 2)[partner]])[:, None, :]

    def piece(p):
        return pl.BlockSpec((ts, BRANCH_W), lambda i: (i, p))

    tab = pl.BlockSpec((2, ts, LANES), lambda i: (0, i % nsb, 0))
    vec = pl.BlockSpec((2, 1, LANES), lambda i: (0, 0, 0))
    out = pl.BlockSpec((ts, BRANCH_W), lambda i: (i, 0))
    qg, kg = with_partner(qg), with_partner(kg)
    return pl.pallas_call(
        _qk_prep_kernel,
        grid=(n // ts,),
        in_specs=[piece(P_RET_Q), piece(P_RET_K), piece(P_DIF_Q), piece(P_DIF_K), tab, tab, vec, vec,
                  pl.BlockSpec(perms.shape, lambda i: (0, 0, 0))],
        out_specs=[out] * 4,
        out_shape=[jax.ShapeDtypeStruct((n, BRANCH_W), BF16)] * 4,
        compiler_params=_params("parallel"),
        name="qk_prep",
    )(proj, proj, proj, proj, cos_t, sin_t, qg, kg, perms)


def _s5_prep_kernel(lam_c_ref, lam_r_ref, b_ref, bt_ref, c_ref, ct_ref,
                    kern_ref, p_ref, q_ref, dec_ref):
    T = S5_CHUNK

    def disc(lam_re, lam_im, log_dt):
        lr = jnp.minimum(lam_re, -1e-4)
        li = lam_im
        dt = jnp.exp(log_dt)
        mag = jnp.exp(lr * dt)
        ab_re = mag * jnp.cos(li * dt)
        ab_im = mag * jnp.sin(li * dt)
        nr = ab_re - 1.0
        ni = ab_im
        den = lr * lr + li * li
        f_re = (nr * lr + ni * li) / den
        f_im = (ni * lr - nr * li) / den
        return lr * dt, li * dt, f_re, f_im

    def powers(ldt_re, ldt_im, k):
        mag = jnp.exp(k * ldt_re)
        return mag * jnp.cos(k * ldt_im), mag * jnp.sin(k * ldt_im)

    ldr, ldi, f_re, f_im = disc(lam_c_ref[0], lam_c_ref[1], lam_c_ref[2])
    b_re, b_im = b_ref[0], b_ref[1]
    bb_re = f_re * b_re - f_im * b_im
    bb_im = f_re * b_im + f_im * b_re
    k_row = lax.broadcasted_iota(jnp.int32, (1, T), 1).astype(F32)
    pw_re, pw_im = powers(ldr, ldi, k_row)
    pw1_re, pw1_im = powers(ldr, ldi, k_row + 1.0)
    c_re, c_im = c_ref[0], c_ref[1]
    ct_re, ct_im = ct_ref[0], ct_ref[1]

    for c in range(SSM_GROUP):
        w_re = pw_re * bb_re[:, c:c + 1] - pw_im * bb_im[:, c:c + 1]
        w_im = pw_re * bb_im[:, c:c + 1] + pw_im * bb_re[:, c:c + 1]
        kern = (jnp.dot(c_re, w_re, preferred_element_type=F32, precision=lax.Precision.HIGHEST)
                - jnp.dot(c_im, w_im, preferred_element_type=F32, precision=lax.Precision.HIGHEST))
        for co in range(SSM_GROUP):
            kern_ref[c, :, co * T:(co + 1) * T] = kern[co:co + 1, :]
        q_ref[0, :, c * T:(c + 1) * T] = (ct_re[:, c:c + 1] * pw1_re - ct_im[:, c:c + 1] * pw1_im).astype(q_ref.dtype)
        q_ref[1, :, c * T:(c + 1) * T] = (-ct_re[:, c:c + 1] * pw1_im - ct_im[:, c:c + 1] * pw1_re).astype(q_ref.dtype)

    ldr_r, ldi_r, fr_r, fi_r = disc(lam_r_ref[0], lam_r_ref[1], lam_r_ref[2])
    bt_re, bt_im = bt_ref[0], bt_ref[1]
    bbt_re = fr_r * bt_re - fi_r * bt_im
    bbt_im = fr_r * bt_im + fi_r * bt_re
    k_col = (T - 1.0) - lax.broadcasted_iota(jnp.int32, (T, 1), 0).astype(F32)
    pr_re, pr_im = powers(ldr_r, ldi_r, k_col)
    for c in range(SSM_GROUP):
        p_ref[0, c * T:(c + 1) * T, :] = (pr_re * bbt_re[c:c + 1, :] - pr_im * bbt_im[c:c + 1, :]).astype(p_ref.dtype)
        p_ref[1, c * T:(c + 1) * T, :] = (pr_re * bbt_im[c:c + 1, :] + pr_im * bbt_re[c:c + 1, :]).astype(p_ref.dtype)
    d_re, d_im = powers(ldr_r, ldi_r, float(T))
    dec_ref[0] = d_re
    dec_ref[1] = d_im


def _s5_prep(lam_re, lam_im, log_dt, b_re, b_im, c_re, c_im):
    g, p = lam_re.shape
    ldt = jnp.broadcast_to(log_dt[:, None], (g, p))
    lam = jnp.stack([lam_re, lam_im, ldt], axis=1)
    lam_c = lam[..., None]
    lam_r = lam[:, :, None, :]
    b = jnp.stack([b_re, b_im], axis=1)
    bt = jnp.swapaxes(b, 2, 3)
    c = jnp.stack([c_re, c_im], axis=1)
    ct = jnp.swapaxes(c, 2, 3)

    def spec(shape):
        return pl.BlockSpec((None,) + shape, lambda i: (i,) + (0,) * len(shape))

    return pl.pallas_call(
        _s5_prep_kernel,
        grid=(g,),
        in_specs=[spec((3, p, 1)), spec((3, 1, p)), spec((2, p, SSM_GROUP)), spec((2, SSM_GROUP, p)),
                  spec((2, SSM_GROUP, p)), spec((2, p, SSM_GROUP))],
        out_specs=[spec((SSM_GROUP, 1, S5_FLAT)), spec((2, S5_FLAT, p)), spec((2, p, S5_FLAT)), spec((2, 1, p))],
        out_shape=[jax.ShapeDtypeStruct((g, SSM_GROUP, 1, S5_FLAT), F32),
                   jax.ShapeDtypeStruct((g, 2, S5_FLAT, p), BF16),
                   jax.ShapeDtypeStruct((g, 2, p, S5_FLAT), BF16),
                   jax.ShapeDtypeStruct((g, 2, 1, p), F32)],
        compiler_params=_params("parallel"),
        name="s5_prep",
    )(lam_c, lam_r, b, bt, c, ct)


def _s5_conv_kernel(d_ref, u_ref, kern_ref, p_ref, q_ref, dec_ref, y_ref, v_ref, s_ref, *, batch, n_chunks):
    g = pl.program_id(0)
    T = S5_CHUNK
    u_flat = jnp.concatenate([u_ref[c] for c in range(SSM_GROUP)], axis=-1)
    jj = lax.broadcasted_iota(jnp.int32, (T, S5_FLAT), 0)
    causal = (lax.broadcasted_iota(jnp.int32, (T, S5_FLAT), 1) % T) >= jj

    def toeplitz_rows(c):
        row = jnp.broadcast_to(kern_ref[c], (T, S5_FLAT))
        toep = pltpu.roll(row, 0, 1, stride=1, stride_axis=0)
        return jnp.where(causal, toep, 0.0).astype(BF16)

    rows = batch * n_chunks
    for ri in range(2):
        v_ref[ri] = jnp.dot(u_flat, p_ref[ri], preferred_element_type=F32).reshape(batch, n_chunks, SSM_STATE)
    dr, di = dec_ref[0], dec_ref[1]

    def body(ch, carry):
        sr, si = carry
        s_ref[0, :, pl.ds(ch, 1), :] = sr
        s_ref[1, :, pl.ds(ch, 1), :] = si
        vr = v_ref[0, :, pl.ds(ch, 1), :]
        vi = v_ref[1, :, pl.ds(ch, 1), :]
        return dr * sr - di * si + vr, dr * si + di * sr + vi

    zero = jnp.zeros((batch, 1, SSM_STATE), F32)
    lax.fori_loop(0, n_chunks, body, (zero, zero))
    y = None
    for c in range(0, SSM_GROUP, 2):
        lhs = jnp.concatenate([u_ref[c], u_ref[c + 1]], axis=-1)
        rhs = jnp.concatenate([toeplitz_rows(c), toeplitz_rows(c + 1)], axis=0)
        part = jnp.dot(lhs, rhs, preferred_element_type=F32)
        y = part if y is None else y + part
    for ri in range(2):
        s_start = s_ref[ri].reshape(rows, SSM_STATE).astype(BF16)
        y += jnp.dot(s_start, q_ref[ri], preferred_element_type=F32)
    for c in range(SSM_GROUP):
        y_ref[c] = y[:, c * T:(c + 1) * T] + d_ref[g * SSM_GROUP + c] * u_ref[c].astype(F32)


def _s5_conv(u_t, d_skip, kern, p, q, dec, batch, seq):
    n = u_t.shape[1]
    r = n // S5_CHUNK
    u3 = u_t.reshape(u_t.shape[0], r, S5_CHUNK)
    g = SSM_GROUPS

    def spec(shape):
        return pl.BlockSpec((None,) + shape, lambda i: (i,) + (0,) * len(shape))

    y3 = pl.pallas_call(
        functools.partial(_s5_conv_kernel, batch=batch, n_chunks=seq // S5_CHUNK),
        grid=(g,),
        in_specs=[pl.BlockSpec(memory_space=pltpu.SMEM),
                  pl.BlockSpec((SSM_GROUP, r, S5_CHUNK), lambda i: (i, 0, 0)),
                  spec((SSM_GROUP, 1, S5_FLAT)), spec((2, S5_FLAT, SSM_STATE)), spec((2, SSM_STATE, S5_FLAT)),
                  spec((2, 1, SSM_STATE))],
        out_specs=pl.BlockSpec((SSM_GROUP, r, S5_CHUNK), lambda i: (i, 0, 0)),
        out_shape=jax.ShapeDtypeStruct((BRANCH_W, r, S5_CHUNK), F32),
        scratch_shapes=[pltpu.VMEM((2, batch, seq // S5_CHUNK, SSM_STATE), F32)] * 2,
        compiler_params=_params("parallel"),
        name="s5_conv",
    )(d_skip, u3, kern, p, q, dec)
    return y3.reshape(BRANCH_W, n)


def _retention_kernel(q_ref, k_ref, v_ref, mask_ref, qd_ref, kd_ref, cd_ref, o_ref, st_ref):
    @pl.when(pl.program_id(1) == 0)
    def _():
        st_ref[...] = jnp.zeros_like(st_ref)

    for bi in range(q_ref.shape[0]):
        for h in range(HEADS):
            sl = slice(h * HEAD_W, (h + 1) * HEAD_W)
            q, k, v = q_ref[bi, :, sl], k_ref[bi, :, sl], v_ref[bi, :, sl]
            scores = lax.dot_general(q, k, NT_DIMS, preferred_element_type=F32) * mask_ref[h]
            st = st_ref[bi, h]
            o = (jnp.dot(scores.astype(BF16), v, preferred_element_type=F32)
                 + jnp.dot(q, st.astype(BF16), preferred_element_type=F32) * qd_ref[h])
            kd = (k.astype(F32) * kd_ref[h]).astype(BF16)
            st_ref[bi, h] = st * cd_ref[h] + lax.dot_general(kd, v, TN_DIMS, preferred_element_type=F32)
            o_ref[bi, :, sl] = _rms(o).astype(o_ref.dtype)


def _retention_consts(blk):
    log_gamma = np.log1p(-(2.0 ** (-5.0 - np.arange(HEADS, dtype=np.float64))))
    idx = np.arange(blk, dtype=np.float64)
    diff = idx[:, None] - idx[None, :]
    same = (np.arange(blk)[:, None] // CHUNK) == (np.arange(blk)[None, :] // CHUNK)
    past = (np.arange(blk)[None, :] // CHUNK) < (np.arange(blk)[:, None] // CHUNK)
    expo = np.where(same, np.abs(diff), diff)
    mask = np.where(same | past, np.exp(expo[None] * log_gamma[:, None, None]), 0.0)
    q_dec = np.exp((idx + 1.0)[None, :] * log_gamma[:, None])
    k_dec = np.exp((blk - 1.0 - idx)[None, :] * log_gamma[:, None])
    c_dec = np.exp(blk * log_gamma)
    bc = lambda a: np.broadcast_to(a[:, :, None], (HEADS, blk, HEAD_W))
    return (jnp.asarray(mask, F32), jnp.asarray(bc(q_dec), F32), jnp.asarray(bc(k_dec), F32),
            jnp.asarray(np.broadcast_to(c_dec[:, None, None], (HEADS, 1, HEAD_W)), F32))


def _retention(rq, rk, proj, batch, seq, blk, bpb):
    n = rq.shape[0]
    nb = seq // blk
    mask, q_dec, k_dec, c_dec = _retention_consts(blk)
    tok = pl.BlockSpec((bpb, blk, BRANCH_W), lambda b, l: (b, l, 0))
    full = lambda a: pl.BlockSpec(a.shape, lambda b, l: (0,) * a.ndim)
    as3 = lambda a: a.reshape(batch, seq, a.shape[-1])
    out = pl.pallas_call(
        _retention_kernel,
        grid=(batch // bpb, nb),
        in_specs=[tok, tok, pl.BlockSpec((bpb, blk, BRANCH_W), lambda b, l: (b, l, P_RET_V)),
                  full(mask), full(q_dec), full(k_dec), full(c_dec)],
        out_specs=tok,
        out_shape=jax.ShapeDtypeStruct((batch, seq, BRANCH_W), BF16),
        scratch_shapes=[pltpu.VMEM((bpb, HEADS, HEAD_W, HEAD_W), F32)],
        compiler_params=_params("parallel", "arbitrary"),
        name="retention",
    )(as3(rq), as3(rk), as3(proj), mask, q_dec, k_dec, c_dec)
    return out.reshape(n, BRANCH_W)


def _diff_attn_kernel(bound_ref, q_ref, k_ref, v_ref, lq1_ref, lk1_ref, lq2_ref, lk2_ref, hg_ref, o_ref,
                      acc_ref, *stat_refs, tq, tk, lambda_init, fixed_stabiliser):
    i = pl.program_id(2)
    n_sub = tq // tk
    q = q_ref[...]
    lane = lax.broadcasted_iota(jnp.int32, (tq, HEAD_W), 1)
    qs = (jnp.where(lane < DIFF_DH, q, jnp.zeros_like(q)), jnp.where(lane < DIFF_DH, jnp.zeros_like(q), q))
    acc_ref[...] = jnp.zeros_like(acc_ref)
    if fixed_stabiliser:
        shift = bound_ref[0, 0]
    else:
        m_ref, = stat_refs
        m_ref[...] = jnp.full_like(m_ref, NEG_INF)
    qc = lax.broadcasted_iota(jnp.int32, (tq, tk), 0) // CHUNK
    kc = lax.broadcasted_iota(jnp.int32, (tq, tk), 1) // CHUNK

    def tile(j, diag):
        rows = pl.ds(pl.multiple_of(j * tk, tk), tk)
        kt = k_ref[rows, :]
        vt = v_ref[rows, :]
        v_ones = jnp.concatenate([vt, jnp.ones_like(vt)], axis=1)
        r0 = 0 if diag is None else diag * tk
        visible = None if diag is None else ((kc + diag * (tk // CHUNK)) <= qc)[r0:]
        for mp in range(2):
            s = lax.dot_general(qs[mp][r0:], kt, NT_DIMS, preferred_element_type=F32)
            if fixed_stabiliser:
                p = jnp.exp2(s - shift)
                if visible is not None:
                    p = jnp.where(visible, p, 0.0)
                acc_ref[mp, r0:] += jnp.dot(p.astype(BF16), v_ones, preferred_element_type=F32)
            else:
                if visible is not None:
                    s = jnp.where(visible, s, NEG_INF)
                m_prev = m_ref[mp, r0:]
                m_new = jnp.maximum(m_prev, jnp.max(s, axis=-1, keepdims=True))
                p = jnp.exp2(s - m_new)
                acc_ref[mp, r0:] = (jnp.exp2(m_prev - m_new) * acc_ref[mp, r0:]
                                    + jnp.dot(p.astype(BF16), v_ones, preferred_element_type=F32))
                m_ref[mp, r0:] = m_new

    def body(jj, carry):
        for u in range(n_sub):
            tile(jj * n_sub + u, None)
        return carry

    lax.fori_loop(0, i, body, 0)
    for d in range(n_sub):
        tile(i * n_sub + d, d)

    lam = (jnp.exp(jnp.sum(lq1_ref[...] * lk1_ref[...], axis=-1, keepdims=True))
           - jnp.exp(jnp.sum(lq2_ref[...] * lk2_ref[...], axis=-1, keepdims=True)) + lambda_init)
    o = (acc_ref[0, :, :HEAD_W] / acc_ref[0, :, HEAD_W:]
         - lam * (acc_ref[1, :, :HEAD_W] / acc_ref[1, :, HEAD_W:]))
    o_ref[...] = (_rms(o) * hg_ref[...] * (1.0 - lambda_init)).astype(o_ref.dtype)


def _diff_attn(dq, dk, proj, qg, kg, lq1, lk1, lq2, lk2, hg, batch, seq, tq, tk, lambda_init):
    n = dq.shape[0]
    nq = seq // tq
    bound = (1.01 * DIFF_DH * DIFF_Q_SCALE * jnp.max(jnp.abs(qg)) * jnp.max(jnp.abs(kg))).reshape(1, 1)
    vec64 = pl.BlockSpec((1, DIFF_DH), lambda b, h, i: (0, 0))

    def call(fixed_stabiliser):
        stat_scratch = [] if fixed_stabiliser else [pltpu.VMEM((2, tq, 1), F32)]
        return pl.pallas_call(
            functools.partial(_diff_attn_kernel, tq=tq, tk=tk, lambda_init=lambda_init,
                              fixed_stabiliser=fixed_stabiliser),
            grid=(batch, HEADS, nq),
            in_specs=[pl.BlockSpec(memory_space=pltpu.SMEM),
                      pl.BlockSpec((tq, HEAD_W), lambda b, h, i: (b * nq + i, h)),
                      pl.BlockSpec((seq, HEAD_W), lambda b, h, i: (b, h)),
                      pl.BlockSpec((seq, HEAD_W), lambda b, h, i: (b, P_DIF_V * HEADS + h)),
                      vec64, vec64, vec64, vec64,
                      pl.BlockSpec((1, HEAD_W), lambda b, h, i: (0, 0))],
            out_specs=pl.BlockSpec((tq, HEAD_W), lambda b, h, i: (b * nq + i, h)),
            out_shape=jax.ShapeDtypeStruct((n, BRANCH_W), BF16),
            scratch_shapes=[pltpu.VMEM((2, tq, 2 * HEAD_W), F32)] + stat_scratch,
            compiler_params=_params("parallel", "parallel", "arbitrary"),
            name="diff_attn_fixed" if fixed_stabiliser else "diff_attn_online",
        )(bound, dq, dk, proj, lq1, lk1, lq2, lk2, hg)

    return lax.cond(bound[0, 0] * 2.0 <= MAX_SCORE_RANGE_LOG2, lambda: call(True), lambda: call(False))


def _mem_kv_kernel(mem_ref, g_ref, w_ref, kg_ref, k_ref, v_ref):
    h = (_rms(mem_ref[...]) * g_ref[...]).astype(BF16)
    kv = jnp.dot(h, w_ref[...], preferred_element_type=F32)
    for hd in range(HEADS):
        sl = slice(hd * HEAD_W, (hd + 1) * HEAD_W)
        k_ref[:, sl] = (_rms(kv[:, sl]) * kg_ref[...]).astype(BF16)
    v_ref[...] = kv[:, BRANCH_W:].astype(BF16)


def _mem_kv(mem2, g, w_bf, kg, mem_len):
    n, d = mem2.shape
    out = pl.BlockSpec((mem_len, BRANCH_W), lambda i: (i, 0))
    return pl.pallas_call(
        _mem_kv_kernel,
        grid=(n // mem_len,),
        in_specs=[pl.BlockSpec((mem_len, d), lambda i: (i, 0)),
                  pl.BlockSpec((1, d), lambda i: (0, 0)),
                  pl.BlockSpec((d, 2 * BRANCH_W), lambda i: (0, 0)),
                  pl.BlockSpec((1, HEAD_W), lambda i: (0, 0))],
        out_specs=[out, out],
        out_shape=[jax.ShapeDtypeStruct((n, BRANCH_W), BF16)] * 2,
        compiler_params=_params("parallel"),
        name="mem_kv",
    )(mem2, g, w_bf, kg)


def _merge_kernel(x_ref, y_ref, sg_ref, ret_ref, rg_ref, dif_ref, dg_ref, mq_ref, mg_ref, mk_ref, mv_ref,
                  mrg0_ref, mrg1_ref, mrg2_ref, mrg3_ref, wglu_ref, bglu_ref, mqg_ref, wbr_ref, bm_ref, wout_ref,
                  o_ref):
    z = jax.nn.gelu(y_ref[...])
    glu = jnp.dot(wglu_ref[...], z.astype(BF16), preferred_element_type=F32) + bglu_ref[...]
    a_t = (z * _sigmoid(glu)).astype(BF16) * _silu_of_half(sg_ref[...])
    branches = [lax.dot_general(a_t, wbr_ref[0], TN_DIMS, preferred_element_type=F32)]
    b_out = ret_ref[...] * _silu_of_half(rg_ref[...])
    branches.append(jnp.dot(b_out, wbr_ref[1], preferred_element_type=F32))
    c_out = dif_ref[...] * _silu_of_half(dg_ref[...])
    branches.append(jnp.dot(c_out, wbr_ref[2], preferred_element_type=F32))
    heads = []
    for hd in range(HEADS):
        sl = slice(hd * HEAD_W, (hd + 1) * HEAD_W)
        qh = (_rms(mq_ref[:, sl].astype(F32)) * mqg_ref[...] * (HEAD_W ** -0.5)).astype(BF16)
        s = lax.dot_general(qh, mk_ref[:, sl], NT_DIMS, preferred_element_type=F32)
        p = jnp.exp(s - jnp.max(s, axis=-1, keepdims=True))
        o = jnp.dot(p.astype(BF16), mv_ref[:, sl], preferred_element_type=F32)
        heads.append(o / jnp.sum(p, axis=-1, keepdims=True))
    m_out = jnp.concatenate(heads, axis=-1).astype(BF16) * _silu_of_half(mg_ref[...])
    branches.append(jnp.dot(m_out, wbr_ref[3], preferred_element_type=F32))
    merged = None
    for nb, mrg_ref in enumerate((mrg0_ref, mrg1_ref, mrg2_ref, mrg3_ref)):
        gate = 1.0 + jnp.tanh(mrg_ref[...] + bm_ref[nb:nb + 1, :].astype(BF16))
        term = gate * branches[nb].astype(BF16)
        merged = term if merged is None else merged + term
    o_ref[...] = x_ref[...] + jnp.dot(merged, wout_ref[...], preferred_element_type=F32)


def _merge(x2, y_t, u_t, proj, ret_o, dif_o, mk, mv, wglu_t, bglu, mqg, wbr, bm, wout, seq, mem_len, tm):
    n, d = x2.shape

    def piece(p):
        return pl.BlockSpec((tm, BRANCH_W), lambda i: (i, p))

    def gate_piece(nb):
        return pl.BlockSpec((tm, D_MODEL), lambda i: (i, P_MERGE * BRANCH_W // D_MODEL + nb))

    tok = pl.BlockSpec((tm, BRANCH_W), lambda i: (i, 0))
    mem = pl.BlockSpec((mem_len, BRANCH_W), lambda i: ((i * tm) // seq, 0))
    full = lambda a: pl.BlockSpec(a.shape, lambda i: (0,) * a.ndim)
    return pl.pallas_call(
        _merge_kernel,
        grid=(n // tm,),
        in_specs=[pl.BlockSpec((tm, d), lambda i: (i, 0)),
                  pl.BlockSpec((BRANCH_W, tm), lambda i: (0, i)),
                  pl.BlockSpec((BRANCH_W, tm), lambda i: (1, i)),
                  tok, piece(P_RET_G), tok, piece(P_DIF_G), piece(P_MEM_Q), piece(P_MEM_G), mem, mem,
                  gate_piece(0), gate_piece(1), gate_piece(2), gate_piece(3),
                  full(wglu_t), full(bglu), full(mqg), full(wbr), full(bm), full(wout)],
        out_specs=pl.BlockSpec((tm, d), lambda i: (i, 0)),
        out_shape=jax.ShapeDtypeStruct((n, d), F32),
        compiler_params=_params("parallel"),
        name="merge",
    )(x2, y_t, u_t, ret_o, proj, dif_o, proj, proj, proj, mk, mv, proj, proj, proj, proj,
      wglu_t, bglu, mqg, wbr, bm, wout)


def _tile(n, pref):
    return pref if n % pref == 0 else n


def kernel(x, mem, norm_g, w_in, ssm_lambda_re, ssm_lambda_im, ssm_log_dt, ssm_b_re, ssm_b_im, ssm_c_re,
           ssm_c_im, ssm_d, ssm_w_glu, ssm_b_glu, diff_q_norm_g, diff_k_norm_g, diff_lambda_q1,
           diff_lambda_k1, diff_lambda_q2, diff_lambda_k2, diff_head_norm_g, mem_norm_g, w_mem_kv,
           mem_q_norm_g, mem_k_norm_g, w_branch, b_merge, w_out):
    batch, seq, d = x.shape
    mem_len = mem.shape[1]
    depth = w_in.shape[0]
    n = batch * seq
    assert d == D_MODEL and seq % S5_CHUNK == 0

    diff_inv = ROPE_THETA ** (-jnp.arange(0, DIFF_DH, 2, dtype=F32) / DIFF_DH)
    ret_inv = 1.0 / (ROPE_THETA ** jnp.linspace(0.0, 1.0, HEAD_W // 2, dtype=F32))
    inv = jnp.stack([jnp.tile(diff_inv, LANES // diff_inv.shape[0]),
                     jnp.tile(ret_inv, LANES // ret_inv.shape[0])])[:, None, :]
    lane = np.arange(LANES)
    sgn = np.stack([np.where(lane % DIFF_DH < DIFF_DH // 2, -1.0, 1.0),
                    np.where(lane < HEAD_W // 2, -1.0, 1.0)])[:, None, :]
    cos_t, sin_t = _rope_tables(seq, inv, jnp.asarray(sgn, F32), _tile(seq, 1024))

    piece = np.arange(w_in.shape[2]) // BRANCH_W
    halved = (piece >= N_IN_PIECES) | np.isin(piece, IN_GATE_PIECES)
    in_col_scale = jnp.asarray(np.where(halved, GATE_FOLD, 1.0), F32)

    x2 = x.reshape(n, d)
    mem2 = mem.reshape(batch * mem_len, d)
    for l in range(depth):
        lambda_init = 0.8 - 0.6 * math.exp(-0.3 * l)
        g = norm_g[l][None, :]
        w = w_in[l] * in_col_scale[None, :]
        proj = _in_proj(x2, g, w[:, 2 * BRANCH_W:].astype(BF16), _tile(n, 2048), 1536)
        u_t = _in_proj_t(x2, g, w[:, :2 * BRANCH_W].T.astype(BF16), _tile(n, 1024))

        kern_op, p_op, q_op, dec = _s5_prep(ssm_lambda_re[l], ssm_lambda_im[l], ssm_log_dt[l], ssm_b_re[l],
                                         ssm_b_im[l], ssm_c_re[l], ssm_c_im[l])
        y_t = _s5_conv(u_t, ssm_d[l], kern_op, p_op, q_op, dec, batch, seq)

        rq, rk, dq, dk = _qk_prep(proj, cos_t, sin_t, diff_q_norm_g[l], diff_k_norm_g[l], seq, _tile(seq, 1024))
        ret_o = _retention(rq, rk, proj, batch, seq, _tile(seq, 256), 2 if batch % 2 == 0 else 1)
        dif_o = _diff_attn(dq, dk, proj, diff_q_norm_g[l], diff_k_norm_g[l], diff_lambda_q1[l][None],
                           diff_lambda_k1[l][None], diff_lambda_q2[l][None], diff_lambda_k2[l][None],
                           diff_head_norm_g[l][None], batch, seq, _tile(seq, 2048), _tile(seq, 512), lambda_init)
        mk, mv = _mem_kv(mem2, mem_norm_g[l][None, :], w_mem_kv[l].astype(BF16), mem_k_norm_g[l][None], mem_len)
        x2 = _merge(x2, y_t, u_t, proj, ret_o, dif_o, mk, mv, ssm_w_glu[l].T.astype(BF16),
                    ssm_b_glu[l][:, None], mem_q_norm_g[l][None], w_branch[l].astype(BF16), GATE_FOLD * b_merge[l],
                    (GATE_FOLD * w_out[l]).astype(BF16), seq, mem_len, _tile(seq, 512))
    return x2.reshape(batch, seq, d)
```

```python
import functools
import math

import numpy as np
import jax
import jax.numpy as jnp
from jax import lax
from jax.experimental import pallas as pl
from jax.experimental.pallas import tpu as pltpu

F32 = jnp.float32
BF16 = jnp.bfloat16

D_MODEL = 1024
BRANCH_W = D_MODEL // 2
N_BRANCH = 4
N_IN_PIECES = 12
CHUNK = 64
SSM_GROUP = 16
SSM_GROUPS = BRANCH_W // SSM_GROUP
SSM_STATE = 64
HEADS = 4
HEAD_W = BRANCH_W // HEADS
DIFF_DH = HEAD_W // 2
ROPE_THETA = 10000.0
EPS = 1e-6
NEG_INF = -1e30
GATE_FOLD = 0.5
IN_GATE_PIECES = (1, 5, 9, 11)
LOG2_E = math.log2(math.e)
DIFF_Q_SCALE = DIFF_DH ** -0.5 * LOG2_E
MAX_SCORE_RANGE_LOG2 = 100.0

LANES = 128
S5_CHUNK = LANES
S5_FLAT = SSM_GROUP * S5_CHUNK
PROJ_COLS = (N_IN_PIECES - 2) * BRANCH_W + N_BRANCH * D_MODEL
P_RET_Q, P_RET_K, P_RET_V, P_RET_G, P_DIF_Q, P_DIF_K, P_DIF_V, P_DIF_G, P_MEM_Q, P_MEM_G, P_MERGE = range(11)

VMEM_LIMIT = 56 * 1024 * 1024

NT_DIMS = (((1,), (1,)), ((), ()))
TN_DIMS = (((0,), (0,)), ((), ()))


def _params(*sem):
    return pltpu.CompilerParams(dimension_semantics=sem, vmem_limit_bytes=VMEM_LIMIT)


def _rms(x, eps=EPS):
    return x * lax.rsqrt(jnp.mean(x * x, axis=-1, keepdims=True) + eps)


def _sigmoid(x):
    return 0.5 * jnp.tanh(0.5 * x) + 0.5


def _silu_of_half(xh):
    return xh * (1.0 + jnp.tanh(xh))


def _in_proj_kernel(x_ref, g_ref, w_ref, o_ref, h_ref):
    @pl.when(pl.program_id(1) == 0)
    def _():
        h_ref[...] = (_rms(x_ref[...]) * g_ref[...]).astype(BF16)

    o_ref[...] = jnp.dot(h_ref[...], w_ref[...], preferred_element_type=F32).astype(o_ref.dtype)


def _in_proj(x2, g, w_bf, tm, tn):
    n, d = x2.shape
    cols = w_bf.shape[1]
    return pl.pallas_call(
        _in_proj_kernel,
        grid=(n // tm, cols // tn),
        in_specs=[pl.BlockSpec((tm, d), lambda i, j: (i, 0)),
                  pl.BlockSpec((1, d), lambda i, j: (0, 0)),
                  pl.BlockSpec((d, tn), lambda i, j: (0, j))],
        out_specs=pl.BlockSpec((tm, tn), lambda i, j: (i, j)),
        out_shape=jax.ShapeDtypeStruct((n, cols), BF16),
        scratch_shapes=[pltpu.VMEM((tm, d), BF16)],
        compiler_params=_params("parallel", "arbitrary"),
        name="in_proj",
    )(x2, g, w_bf)


def _in_proj_t_kernel(x_ref, g_ref, wt_ref, u_ref, gate_ref):
    h = (_rms(x_ref[...]) * g_ref[...]).astype(BF16)
    res = lax.dot_general(wt_ref[...], h, NT_DIMS, preferred_element_type=F32)
    tm = res.shape[1]
    u_ref[...] = res[:BRANCH_W].reshape(BRANCH_W, tm // S5_CHUNK, S5_CHUNK).astype(u_ref.dtype)
    gate_ref[...] = res[BRANCH_W:].astype(gate_ref.dtype)


def _in_proj_t(x2, g, wt_bf, tm):
    n, d = x2.shape
    rows = wt_bf.shape[0]
    return pl.pallas_call(
        _in_proj_t_kernel,
        grid=(n // tm,),
        in_specs=[pl.BlockSpec((tm, d), lambda i: (i, 0)),
                  pl.BlockSpec((1, d), lambda i: (0, 0)),
                  pl.BlockSpec((rows, d), lambda i: (0, 0))],
        out_specs=[pl.BlockSpec((BRANCH_W, tm // S5_CHUNK, S5_CHUNK), lambda i: (0, i, 0)),
                   pl.BlockSpec((BRANCH_W, tm), lambda i: (0, i))],
        out_shape=[jax.ShapeDtypeStruct((BRANCH_W, n // S5_CHUNK, S5_CHUNK), BF16),
                   jax.ShapeDtypeStruct((BRANCH_W, n), BF16)],
        compiler_params=_params("parallel"),
        name="in_proj_t",
    )(x2, g, wt_bf)


def _rope_table_kernel(inv_ref, sgn_ref, cos_ref, sin_ref):
    ts = cos_ref.shape[0]
    pos = (lax.broadcasted_iota(jnp.int32, (ts, LANES), 0) + pl.program_id(1) * ts).astype(F32)
    ang = pos * inv_ref[...]
    cos_ref[...] = jnp.cos(ang)
    sin_ref[...] = jnp.sin(ang) * sgn_ref[...]


def _rope_tables(seq, inv, sgn, ts):
    spec_in = pl.BlockSpec((None, 1, LANES), lambda t, i: (t, 0, 0))
    spec_out = pl.BlockSpec((None, ts, LANES), lambda t, i: (t, i, 0))
    return pl.pallas_call(
        _rope_table_kernel,
        grid=(2, seq // ts),
        in_specs=[spec_in, spec_in],
        out_specs=[spec_out, spec_out],
        out_shape=[jax.ShapeDtypeStruct((2, seq, LANES), F32)] * 2,
        compiler_params=_params("parallel", "parallel"),
        name="rope_tables",
    )(inv, sgn)


def _qk_prep_kernel(rq_ref, rk_ref, dq_ref, dk_ref, cos_ref, sin_ref, qg_ref, kg_ref, perm_ref,
                    orq_ref, ork_ref, odq_ref, odk_ref):
    two = lambda t: jnp.concatenate([t, t], axis=1)
    cos_d, sin_d, cos_r, sin_r = two(cos_ref[0]), two(sin_ref[0]), two(cos_ref[1]), two(sin_ref[1])
    k_scale = HEAD_W ** -0.5
    cos_rk, sin_rk = cos_r * k_scale, sin_r * k_scale
    q_cos, q_sin = cos_d * (two(qg_ref[0]) * DIFF_Q_SCALE), sin_d * (two(qg_ref[1]) * DIFF_Q_SCALE)
    k_cos, k_sin = cos_d * two(kg_ref[0]), sin_d * two(kg_ref[1])

    def shuffled(x, which):
        return jnp.dot(x, perm_ref[which], preferred_element_type=F32)

    def rope_r(x, c, s):
        return (x.astype(F32) * c + shuffled(x, 0) * s).astype(BF16)

    def norm_rope_d(x, c, s):
        xf = x.astype(F32)
        ms = shuffled((xf * xf).astype(BF16), 2) * (1.0 / DIFF_DH)
        return (lax.rsqrt(ms + EPS) * (xf * c + shuffled(x, 1) * s)).astype(BF16)

    for hp in range(HEADS // 2):
        sl = slice(hp * 2 * HEAD_W, (hp + 1) * 2 * HEAD_W)
        orq_ref[:, sl] = rope_r(rq_ref[:, sl], cos_r, sin_r)
        ork_ref[:, sl] = rope_r(rk_ref[:, sl], cos_rk, sin_rk)
        odq_ref[:, sl] = norm_rope_d(dq_ref[:, sl], q_cos, q_sin)
        odk_ref[:, sl] = norm_rope_d(dk_ref[:, sl], k_cos, k_sin)


def _qk_prep_consts():
    lane = np.arange(LANES)
    half_rot = (lane + HEAD_W // 2) % LANES
    partner = np.where(lane % DIFF_DH < DIFF_DH // 2, lane + DIFF_DH // 2, lane - DIFF_DH // 2)
    perms = np.zeros((3, LANES, LANES), np.float32)
    perms[0, half_rot, lane] = 1.0
    perms[1, partner, lane] = 1.0
    perms[2] = (lane[:, None] // DIFF_DH) == (lane[None, :] // DIFF_DH)
    two_heads = np.zeros((3, 2 * LANES, 2 * LANES), np.float32)
    two_heads[:, :LANES, :LANES] = perms
    two_heads[:, LANES:, LANES:] = perms
    return jnp.asarray(two_heads, BF16), partner


def _qk_prep(proj, cos_t, sin_t, qg, kg, seq, ts):
    n = proj.shape[0]
    nsb = seq // ts
    perms, partner = _qk_prep_consts()
    with_partner = lambda g: jnp.stack([jnp.tile(g, 2), jnp.tile(g, 2)[partner]])[:, None, :]

    def piece(p):
        return pl.BlockSpec((ts, BRANCH_W), lambda i: (i, p))

    tab = pl.BlockSpec((2, ts, LANES), lambda i: (0, i % nsb, 0))
    vec = pl.BlockSpec((2, 1, LANES), lambda i: (0, 0, 0))
    out = pl.BlockSpec((ts, BRANCH_W), lambda i: (i, 0))
    qg, kg = with_partner(qg), with_partner(kg)
    return pl.pallas_call(
        _qk_prep_kernel,
        grid=(n // ts,),
        in_specs=[piece(P_RET_Q), piece(P_RET_K), piece(P_DIF_Q), piece(P_DIF_K), tab, tab, vec, vec,
                  pl.BlockSpec(perms.shape, lambda i: (0, 0, 0))],
        out_specs=[out] * 4,
        out_shape=[jax.ShapeDtypeStruct((n, BRANCH_W), BF16)] * 4,
        compiler_params=_params("parallel"),
        name="qk_prep",
    )(proj, proj, proj, proj, cos_t, sin_t, qg, kg, perms)


def _s5_prep_kernel(lam_c_ref, lam_r_ref, b_ref, bt_ref, c_ref, ct_ref,
                    kern_ref, p_ref, q_ref, dec_ref):
    T = S5_CHUNK

    def disc(lam_re, lam_im, log_dt):
        lr = jnp.minimum(lam_re, -1e-4)
        li = lam_im
        dt = jnp.exp(log_dt)
        mag = jnp.exp(lr * dt)
        ab_re = mag * jnp.cos(li * dt)
        ab_im = mag * jnp.sin(li * dt)
        nr = ab_re - 1.0
        ni = ab_im
        den = lr * lr + li * li
        f_re = (nr * lr + ni * li) / den
        f_im = (ni * lr - nr * li) / den
        return lr * dt, li * dt, f_re, f_im

    def powers(ldt_re, ldt_im, k):
        mag = jnp.exp(k * ldt_re)
        return mag * jnp.cos(k * ldt_im), mag * jnp.sin(k * ldt_im)

    ldr, ldi, f_re, f_im = disc(lam_c_ref[0], lam_c_ref[1], lam_c_ref[2])
    b_re, b_im = b_ref[0], b_ref[1]
    bb_re = f_re * b_re - f_im * b_im
    bb_im = f_re * b_im + f_im * b_re
    k_row = lax.broadcasted_iota(jnp.int32, (1, T), 1).astype(F32)
    pw_re, pw_im = powers(ldr, ldi, k_row)
    pw1_re, pw1_im = powers(ldr, ldi, k_row + 1.0)
    c_re, c_im = c_ref[0], c_ref[1]
    ct_re, ct_im = ct_ref[0], ct_ref[1]

    for c in range(SSM_GROUP):
        w_re = pw_re * bb_re[:, c:c + 1] - pw_im * bb_im[:, c:c + 1]
        w_im = pw_re * bb_im[:, c:c + 1] + pw_im * bb_re[:, c:c + 1]
        kern = (jnp.dot(c_re, w_re, preferred_element_type=F32, precision=lax.Precision.HIGHEST)
                - jnp.dot(c_im, w_im, preferred_element_type=F32, precision=lax.Precision.HIGHEST))
        for co in range(SSM_GROUP):
            kern_ref[c, :, co * T:(co + 1) * T] = kern[co:co + 1, :]
        q_ref[0, :, c * T:(c + 1) * T] = (ct_re[:, c:c + 1] * pw1_re - ct_im[:, c:c + 1] * pw1_im).astype(q_ref.dtype)
        q_ref[1, :, c * T:(c + 1) * T] = (-ct_re[:, c:c + 1] * pw1_im - ct_im[:, c:c + 1] * pw1_re).astype(q_ref.dtype)

    ldr_r, ldi_r, fr_r, fi_r = disc(lam_r_ref[0], lam_r_ref[1], lam_r_ref[2])
    bt_re, bt_im = bt_ref[0], bt_ref[1]
    bbt_re = fr_r * bt_re - fi_r * bt_im
    bbt_im = fr_r * bt_im + fi_r * bt_re
    k_col = (T - 1.0) - lax.broadcasted_iota(jnp.int32, (T, 1), 0).astype(F32)
    pr_re, pr_im = powers(ldr_r, ldi_r, k_col)
    for c in range(SSM_GROUP):
        p_ref[0, c * T:(c + 1) * T, :] = (pr_re * bbt_re[c:c + 1, :] - pr_im * bbt_im[c:c + 1, :]).astype(p_ref.dtype)
        p_ref[1, c * T:(c + 1) * T, :] = (pr_re * bbt_im[c:c + 1, :] + pr_im * bbt_re[c:c + 1, :]).astype(p_ref.dtype)
    d_re, d_im = powers(ldr_r, ldi_r, float(T))
    dec_ref[0] = d_re
    dec_ref[1] = d_im


def _s5_prep(lam_re, lam_im, log_dt, b_re, b_im, c_re, c_im):
    g, p = lam_re.shape
    ldt = jnp.broadcast_to(log_dt[:, None], (g, p))
    lam = jnp.stack([lam_re, lam_im, ldt], axis=1)
    lam_c = lam[..., None]
    lam_r = lam[:, :, None, :]
    b = jnp.stack([b_re, b_im], axis=1)
    bt = jnp.swapaxes(b, 2, 3)
    c = jnp.stack([c_re, c_im], axis=1)
    ct = jnp.swapaxes(c, 2, 3)

    def spec(shape):
        return pl.BlockSpec((None,) + shape, lambda i: (i,) + (0,) * len(shape))

    return pl.pallas_call(
        _s5_prep_kernel,
        grid=(g,),
        in_specs=[spec((3, p, 1)), spec((3, 1, p)), spec((2, p, SSM_GROUP)), spec((2, SSM_GROUP, p)),
                  spec((2, SSM_GROUP, p)), spec((2, p, SSM_GROUP))],
        out_specs=[spec((SSM_GROUP, 1, S5_FLAT)), spec((2, S5_FLAT, p)), spec((2, p, S5_FLAT)), spec((2, 1, p))],
        out_shape=[jax.ShapeDtypeStruct((g, SSM_GROUP, 1, S5_FLAT), F32),
                   jax.ShapeDtypeStruct((g, 2, S5_FLAT, p), BF16),
                   jax.ShapeDtypeStruct((g, 2, p, S5_FLAT), BF16),
                   jax.ShapeDtypeStruct((g, 2, 1, p), F32)],
        compiler_params=_params("parallel"),
        name="s5_prep",
    )(lam_c, lam_r, b, bt, c, ct)


def _s5_conv_kernel(d_ref, u_ref, kern_ref, p_ref, q_ref, dec_ref, y_ref, v_ref, s_ref, *, batch, n_chunks):
    g = pl.program_id(0)
    T = S5_CHUNK
    u_flat = jnp.concatenate([u_ref[c] for c in range(SSM_GROUP)], axis=-1)
    jj = lax.broadcasted_iota(jnp.int32, (T, S5_FLAT), 0)
    causal = (lax.broadcasted_iota(jnp.int32, (T, S5_FLAT), 1) % T) >= jj

    def toeplitz_rows(c):
        row = jnp.broadcast_to(kern_ref[c], (T, S5_FLAT))
        toep = pltpu.roll(row, 0, 1, stride=1, stride_axis=0)
        return jnp.where(causal, toep, 0.0).astype(BF16)

    rows = batch * n_chunks
    for ri in range(2):
        v_ref[ri] = jnp.dot(u_flat, p_ref[ri], preferred_element_type=F32).reshape(batch, n_chunks, SSM_STATE)
    dr, di = dec_ref[0], dec_ref[1]

    def body(ch, carry):
        sr, si = carry
        s_ref[0, :, pl.ds(ch, 1), :] = sr
        s_ref[1, :, pl.ds(ch, 1), :] = si
        vr = v_ref[0, :, pl.ds(ch, 1), :]
        vi = v_ref[1, :, pl.ds(ch, 1), :]
        return dr * sr - di * si + vr, dr * si + di * sr + vi

    zero = jnp.zeros((batch, 1, SSM_STATE), F32)
    lax.fori_loop(0, n_chunks, body, (zero, zero))
    y = None
    for c in range(0, SSM_GROUP, 2):
        lhs = jnp.concatenate([u_ref[c], u_ref[c + 1]], axis=-1)
        rhs = jnp.concatenate([toeplitz_rows(c), toeplitz_rows(c + 1)], axis=0)
        part = jnp.dot(lhs, rhs, preferred_element_type=F32)
        y = part if y is None else y + part
    for ri in range(2):
        s_start = s_ref[ri].reshape(rows, SSM_STATE).astype(BF16)
        y += jnp.dot(s_start, q_ref[ri], preferred_element_type=F32)
    slabs = [y[:, c * T:(c + 1) * T] + d_ref[g * SSM_GROUP + c] * u_ref[c].astype(F32) for c in range(SSM_GROUP)]
    y_ref[...] = jnp.stack(slabs, axis=0).reshape(SSM_GROUP, rows * T)


def _s5_conv(u3, d_skip, kern, p, q, dec, batch, seq):
    r = u3.shape[1]
    n = r * S5_CHUNK
    g = SSM_GROUPS

    def spec(shape):
        return pl.BlockSpec((None,) + shape, lambda i: (i,) + (0,) * len(shape))

    return pl.pallas_call(
        functools.partial(_s5_conv_kernel, batch=batch, n_chunks=seq // S5_CHUNK),
        grid=(g,),
        in_specs=[pl.BlockSpec(memory_space=pltpu.SMEM),
                  pl.BlockSpec((SSM_GROUP, r, S5_CHUNK), lambda i: (i, 0, 0)),
                  spec((SSM_GROUP, 1, S5_FLAT)), spec((2, S5_FLAT, SSM_STATE)), spec((2, SSM_STATE, S5_FLAT)),
                  spec((2, 1, SSM_STATE))],
        out_specs=pl.BlockSpec((SSM_GROUP, n), lambda i: (i, 0)),
        out_shape=jax.ShapeDtypeStruct((BRANCH_W, n), F32),
        scratch_shapes=[pltpu.VMEM((2, batch, seq // S5_CHUNK, SSM_STATE), F32)] * 2,
        compiler_params=_params("parallel"),
        name="s5_conv",
    )(d_skip, u3, kern, p, q, dec)


def _retention_kernel(q_ref, k_ref, v_ref, mask_ref, qd_ref, kd_ref, cd_ref, o_ref, st_ref):
    @pl.when(pl.program_id(1) == 0)
    def _():
        st_ref[...] = jnp.zeros_like(st_ref)

    for bi in range(q_ref.shape[0]):
        for h in range(HEADS):
            sl = slice(h * HEAD_W, (h + 1) * HEAD_W)
            q, k, v = q_ref[bi, :, sl], k_ref[bi, :, sl], v_ref[bi, :, sl]
            scores = lax.dot_general(q, k, NT_DIMS, preferred_element_type=F32) * mask_ref[h]
            st = st_ref[bi, h]
            o = (jnp.dot(scores.astype(BF16), v, preferred_element_type=F32)
                 + jnp.dot(q, st.astype(BF16), preferred_element_type=F32) * qd_ref[h])
            kd = (k.astype(F32) * kd_ref[h]).astype(BF16)
            st_ref[bi, h] = st * cd_ref[h] + lax.dot_general(kd, v, TN_DIMS, preferred_element_type=F32)
            o_ref[bi, :, sl] = _rms(o).astype(o_ref.dtype)


def _retention_consts(blk):
    log_gamma = np.log1p(-(2.0 ** (-5.0 - np.arange(HEADS, dtype=np.float64))))
    idx = np.arange(blk, dtype=np.float64)
    diff = idx[:, None] - idx[None, :]
    same = (np.arange(blk)[:, None] // CHUNK) == (np.arange(blk)[None, :] // CHUNK)
    past = (np.arange(blk)[None, :] // CHUNK) < (np.arange(blk)[:, None] // CHUNK)
    expo = np.where(same, np.abs(diff), diff)
    mask = np.where(same | past, np.exp(expo[None] * log_gamma[:, None, None]), 0.0)
    q_dec = np.exp((idx + 1.0)[None, :] * log_gamma[:, None])
    k_dec = np.exp((blk - 1.0 - idx)[None, :] * log_gamma[:, None])
    c_dec = np.exp(blk * log_gamma)
    bc = lambda a: np.broadcast_to(a[:, :, None], (HEADS, blk, HEAD_W))
    return (jnp.asarray(mask, F32), jnp.asarray(bc(q_dec), F32), jnp.asarray(bc(k_dec), F32),
            jnp.asarray(np.broadcast_to(c_dec[:, None, None], (HEADS, 1, HEAD_W)), F32))


def _retention(rq, rk, proj, batch, seq, blk, bpb):
    n = rq.shape[0]
    nb = seq // blk
    mask, q_dec, k_dec, c_dec = _retention_consts(blk)
    tok = pl.BlockSpec((bpb, blk, BRANCH_W), lambda b, l: (b, l, 0))
    full = lambda a: pl.BlockSpec(a.shape, lambda b, l: (0,) * a.ndim)
    as3 = lambda a: a.reshape(batch, seq, a.shape[-1])
    out = pl.pallas_call(
        _retention_kernel,
        grid=(batch // bpb, nb),
        in_specs=[tok, tok, pl.BlockSpec((bpb, blk, BRANCH_W), lambda b, l: (b, l, P_RET_V)),
                  full(mask), full(q_dec), full(k_dec), full(c_dec)],
        out_specs=tok,
        out_shape=jax.ShapeDtypeStruct((batch, seq, BRANCH_W), BF16),
        scratch_shapes=[pltpu.VMEM((bpb, HEADS, HEAD_W, HEAD_W), F32)],
        compiler_params=_params("parallel", "arbitrary"),
        name="retention",
    )(as3(rq), as3(rk), as3(proj), mask, q_dec, k_dec, c_dec)
    return out.reshape(n, BRANCH_W)


def _diff_attn_kernel(bound_ref, q_ref, k_ref, v_ref, lq1_ref, lk1_ref, lq2_ref, lk2_ref, hg_ref, o_ref,
                      acc_ref, *stat_refs, tq, tk, lambda_init, fixed_stabiliser):
    i = pl.program_id(2)
    n_sub = tq // tk
    q = q_ref[...]
    lane = lax.broadcasted_iota(jnp.int32, (tq, HEAD_W), 1)
    qs = (jnp.where(lane < DIFF_DH, q, jnp.zeros_like(q)), jnp.where(lane < DIFF_DH, jnp.zeros_like(q), q))
    acc_ref[...] = jnp.zeros_like(acc_ref)
    if fixed_stabiliser:
        shift = bound_ref[0, 0]
    else:
        m_ref, = stat_refs
        m_ref[...] = jnp.full_like(m_ref, NEG_INF)
    qc = lax.broadcasted_iota(jnp.int32, (tq, tk), 0) // CHUNK
    kc = lax.broadcasted_iota(jnp.int32, (tq, tk), 1) // CHUNK

    def tile(j, diag):
        rows = pl.ds(pl.multiple_of(j * tk, tk), tk)
        kt = k_ref[rows, :]
        vt = v_ref[rows, :]
        v_ones = jnp.concatenate([vt, jnp.ones_like(vt)], axis=1)
        r0 = 0 if diag is None else diag * tk
        visible = None if diag is None else ((kc + diag * (tk // CHUNK)) <= qc)[r0:]
        for mp in range(2):
            s = lax.dot_general(qs[mp][r0:], kt, NT_DIMS, preferred_element_type=F32)
            if fixed_stabiliser:
                p = jnp.exp2(s - shift)
                if visible is not None:
                    p = jnp.where(visible, p, 0.0)
                acc_ref[mp, r0:] += jnp.dot(p.astype(BF16), v_ones, preferred_element_type=F32)
            else:
                if visible is not None:
                    s = jnp.where(visible, s, NEG_INF)
                m_prev = m_ref[mp, r0:]
                m_new = jnp.maximum(m_prev, jnp.max(s, axis=-1, keepdims=True))
                p = jnp.exp2(s - m_new)
                acc_ref[mp, r0:] = (jnp.exp2(m_prev - m_new) * acc_ref[mp, r0:]
                                    + jnp.dot(p.astype(BF16), v_ones, preferred_element_type=F32))
                m_ref[mp, r0:] = m_new

    def body(jj, carry):
        for u in range(n_sub):
            tile(jj * n_sub + u, None)
        return carry

    lax.fori_loop(0, i, body, 0)
    for d in range(n_sub):
        tile(i * n_sub + d, d)

    lam = (jnp.exp(jnp.sum(lq1_ref[...] * lk1_ref[...], axis=-1, keepdims=True))
           - jnp.exp(jnp.sum(lq2_ref[...] * lk2_ref[...], axis=-1, keepdims=True)) + lambda_init)
    o = (acc_ref[0, :, :HEAD_W] / acc_ref[0, :, HEAD_W:]
         - lam * (acc_ref[1, :, :HEAD_W] / acc_ref[1, :, HEAD_W:]))
    o_ref[...] = (_rms(o) * hg_ref[...] * (1.0 - lambda_init)).astype(o_ref.dtype)


def _diff_attn(dq, dk, proj, qg, kg, lq1, lk1, lq2, lk2, hg, batch, seq, tq, tk, lambda_init):
    n = dq.shape[0]
    nq = seq // tq
    bound = (1.01 * DIFF_DH * DIFF_Q_SCALE * jnp.max(jnp.abs(qg)) * jnp.max(jnp.abs(kg))).reshape(1, 1)
    vec64 = pl.BlockSpec((1, DIFF_DH), lambda b, h, i: (0, 0))

    def call(fixed_stabiliser):
        stat_scratch = [] if fixed_stabiliser else [pltpu.VMEM((2, tq, 1), F32)]
        return pl.pallas_call(
            functools.partial(_diff_attn_kernel, tq=tq, tk=tk, lambda_init=lambda_init,
                              fixed_stabiliser=fixed_stabiliser),
            grid=(batch, HEADS, nq),
            in_specs=[pl.BlockSpec(memory_space=pltpu.SMEM),
                      pl.BlockSpec((tq, HEAD_W), lambda b, h, i: (b * nq + i, h)),
                      pl.BlockSpec((seq, HEAD_W), lambda b, h, i: (b, h)),
                      pl.BlockSpec((seq, HEAD_W), lambda b, h, i: (b, P_DIF_V * HEADS + h)),
                      vec64, vec64, vec64, vec64,
                      pl.BlockSpec((1, HEAD_W), lambda b, h, i: (0, 0))],
            out_specs=pl.BlockSpec((tq, HEAD_W), lambda b, h, i: (b * nq + i, h)),
            out_shape=jax.ShapeDtypeStruct((n, BRANCH_W), BF16),
            scratch_shapes=[pltpu.VMEM((2, tq, 2 * HEAD_W), F32)] + stat_scratch,
            compiler_params=_params("parallel", "parallel", "arbitrary"),
            name="diff_attn_fixed" if fixed_stabiliser else "diff_attn_online",
        )(bound, dq, dk, proj, lq1, lk1, lq2, lk2, hg)

    return lax.cond(bound[0, 0] * 2.0 <= MAX_SCORE_RANGE_LOG2, lambda: call(True), lambda: call(False))


def _mem_kv_kernel(mem_ref, g_ref, w_ref, kg_ref, k_ref, v_ref):
    h = (_rms(mem_ref[...]) * g_ref[...]).astype(BF16)
    kv = jnp.dot(h, w_ref[...], preferred_element_type=F32)
    for hd in range(HEADS):
        sl = slice(hd * HEAD_W, (hd + 1) * HEAD_W)
        k_ref[:, sl] = (_rms(kv[:, sl]) * kg_ref[...]).astype(BF16)
    v_ref[...] = kv[:, BRANCH_W:].astype(BF16)


def _mem_kv(mem2, g, w_bf, kg, mem_len):
    n, d = mem2.shape
    out = pl.BlockSpec((mem_len, BRANCH_W), lambda i: (i, 0))
    return pl.pallas_call(
        _mem_kv_kernel,
        grid=(n // mem_len,),
        in_specs=[pl.BlockSpec((mem_len, d), lambda i: (i, 0)),
                  pl.BlockSpec((1, d), lambda i: (0, 0)),
                  pl.BlockSpec((d, 2 * BRANCH_W), lambda i: (0, 0)),
                  pl.BlockSpec((1, HEAD_W), lambda i: (0, 0))],
        out_specs=[out, out],
        out_shape=[jax.ShapeDtypeStruct((n, BRANCH_W), BF16)] * 2,
        compiler_params=_params("parallel"),
        name="mem_kv",
    )(mem2, g, w_bf, kg)


def _merge_kernel(x_ref, y_ref, sg_ref, ret_ref, rg_ref, dif_ref, dg_ref, mq_ref, mg_ref, mk_ref, mv_ref,
                  mrg0_ref, mrg1_ref, mrg2_ref, mrg3_ref, wglu_ref, bglu_ref, mqg_ref, wbr_ref, bm_ref, wout_ref,
                  o_ref):
    z = jax.nn.gelu(y_ref[...])
    glu = jnp.dot(wglu_ref[...], z.astype(BF16), preferred_element_type=F32) + bglu_ref[...]
    a_t = (z * _sigmoid(glu)).astype(BF16) * _silu_of_half(sg_ref[...])
    branches = [lax.dot_general(a_t, wbr_ref[0], TN_DIMS, preferred_element_type=F32)]
    b_out = ret_ref[...] * _silu_of_half(rg_ref[...])
    branches.append(jnp.dot(b_out, wbr_ref[1], preferred_element_type=F32))
    c_out = dif_ref[...] * _silu_of_half(dg_ref[...])
    branches.append(jnp.dot(c_out, wbr_ref[2], preferred_element_type=F32))
    heads = []
    for hd in range(HEADS):
        sl = slice(hd * HEAD_W, (hd + 1) * HEAD_W)
        qh = (_rms(mq_ref[:, sl].astype(F32)) * mqg_ref[...] * (HEAD_W ** -0.5)).astype(BF16)
        s = lax.dot_general(qh, mk_ref[:, sl], NT_DIMS, preferred_element_type=F32)
        p = jnp.exp(s - jnp.max(s, axis=-1, keepdims=True))
        o = jnp.dot(p.astype(BF16), mv_ref[:, sl], preferred_element_type=F32)
        heads.append(o / jnp.sum(p, axis=-1, keepdims=True))
    m_out = jnp.concatenate(heads, axis=-1).astype(BF16) * _silu_of_half(mg_ref[...])
    branches.append(jnp.dot(m_out, wbr_ref[3], preferred_element_type=F32))
    merged = None
    for nb, mrg_ref in enumerate((mrg0_ref, mrg1_ref, mrg2_ref, mrg3_ref)):
        gate = 1.0 + jnp.tanh(mrg_ref[...] + bm_ref[nb:nb + 1, :].astype(BF16))
        term = gate * branches[nb].astype(BF16)
        merged = term if merged is None else merged + term
    o_ref[...] = x_ref[...] + jnp.dot(merged, wout_ref[...], preferred_element_type=F32)


def _merge(x2, y_t, sg_t, proj, ret_o, dif_o, mk, mv, wglu_t, bglu, mqg, wbr, bm, wout, seq, mem_len, tm):
    n, d = x2.shape

    def piece(p):
        return pl.BlockSpec((tm, BRANCH_W), lambda i: (i, p))

    def gate_piece(nb):
        return pl.BlockSpec((tm, D_MODEL), lambda i: (i, P_MERGE * BRANCH_W // D_MODEL + nb))

    tok = pl.BlockSpec((tm, BRANCH_W), lambda i: (i, 0))
    mem = pl.BlockSpec((mem_len, BRANCH_W), lambda i: ((i * tm) // seq, 0))
    full = lambda a: pl.BlockSpec(a.shape, lambda i: (0,) * a.ndim)
    return pl.pallas_call(
        _merge_kernel,
        grid=(n // tm,),
        in_specs=[pl.BlockSpec((tm, d), lambda i: (i, 0)),
                  pl.BlockSpec((BRANCH_W, tm), lambda i: (0, i)),
                  pl.BlockSpec((BRANCH_W, tm), lambda i: (0, i)),
                  tok, piece(P_RET_G), tok, piece(P_DIF_G), piece(P_MEM_Q), piece(P_MEM_G), mem, mem,
                  gate_piece(0), gate_piece(1), gate_piece(2), gate_piece(3),
                  full(wglu_t), full(bglu), full(mqg), full(wbr), full(bm), full(wout)],
        out_specs=pl.BlockSpec((tm, d), lambda i: (i, 0)),
        out_shape=jax.ShapeDtypeStruct((n, d), F32),
        compiler_params=_params("parallel"),
        name="merge",
    )(x2, y_t, sg_t, ret_o, proj, dif_o, proj, proj, proj, mk, mv, proj, proj, proj, proj,
      wglu_t, bglu, mqg, wbr, bm, wout)


def _tile(n, pref):
    return pref if n % pref == 0 else n


def kernel(x, mem, norm_g, w_in, ssm_lambda_re, ssm_lambda_im, ssm_log_dt, ssm_b_re, ssm_b_im, ssm_c_re,
           ssm_c_im, ssm_d, ssm_w_glu, ssm_b_glu, diff_q_norm_g, diff_k_norm_g, diff_lambda_q1,
           diff_lambda_k1, diff_lambda_q2, diff_lambda_k2, diff_head_norm_g, mem_norm_g, w_mem_kv,
           mem_q_norm_g, mem_k_norm_g, w_branch, b_merge, w_out):
    batch, seq, d = x.shape
    mem_len = mem.shape[1]
    depth = w_in.shape[0]
    n = batch * seq
    assert d == D_MODEL and seq % S5_CHUNK == 0

    diff_inv = ROPE_THETA ** (-jnp.arange(0, DIFF_DH, 2, dtype=F32) / DIFF_DH)
    ret_inv = 1.0 / (ROPE_THETA ** jnp.linspace(0.0, 1.0, HEAD_W // 2, dtype=F32))
    inv = jnp.stack([jnp.tile(diff_inv, LANES // diff_inv.shape[0]),
                     jnp.tile(ret_inv, LANES // ret_inv.shape[0])])[:, None, :]
    lane = np.arange(LANES)
    sgn = np.stack([np.where(lane % DIFF_DH < DIFF_DH // 2, -1.0, 1.0),
                    np.where(lane < HEAD_W // 2, -1.0, 1.0)])[:, None, :]
    cos_t, sin_t = _rope_tables(seq, inv, jnp.asarray(sgn, F32), _tile(seq, 1024))

    piece = np.arange(w_in.shape[2]) // BRANCH_W
    halved = (piece >= N_IN_PIECES) | np.isin(piece, IN_GATE_PIECES)
    in_col_scale = jnp.asarray(np.where(halved, GATE_FOLD, 1.0), F32)

    x2 = x.reshape(n, d)
    mem2 = mem.reshape(batch * mem_len, d)
    for l in range(depth):
        lambda_init = 0.8 - 0.6 * math.exp(-0.3 * l)
        g = norm_g[l][None, :]
        w = w_in[l] * in_col_scale[None, :]
        proj = _in_proj(x2, g, w[:, 2 * BRANCH_W:].astype(BF16), _tile(n, 2048), 1536)
        u3, sg_t = _in_proj_t(x2, g, w[:, :2 * BRANCH_W].T.astype(BF16), _tile(n, 2048))

        kern_op, p_op, q_op, dec = _s5_prep(ssm_lambda_re[l], ssm_lambda_im[l], ssm_log_dt[l], ssm_b_re[l],
                                         ssm_b_im[l], ssm_c_re[l], ssm_c_im[l])
        y_t = _s5_conv(u3, ssm_d[l], kern_op, p_op, q_op, dec, batch, seq)

        rq, rk, dq, dk = _qk_prep(proj, cos_t, sin_t, diff_q_norm_g[l], diff_k_norm_g[l], seq, _tile(seq, 1024))
        ret_o = _retention(rq, rk, proj, batch, seq, _tile(seq, 256), 2 if batch % 2 == 0 else 1)
        dif_o = _diff_attn(dq, dk, proj, diff_q_norm_g[l], diff_k_norm_g[l], diff_lambda_q1[l][None],
                           diff_lambda_k1[l][None], diff_lambda_q2[l][None], diff_lambda_k2[l][None],
                           diff_head_norm_g[l][None], batch, seq, _tile(seq, 2048), _tile(seq, 512), lambda_init)
        mk, mv = _mem_kv(mem2, mem_norm_g[l][None, :], w_mem_kv[l].astype(BF16), mem_k_norm_g[l][None], mem_len)
        x2 = _merge(x2, y_t, sg_t, proj, ret_o, dif_o, mk, mv, ssm_w_glu[l].T.astype(BF16),
                    ssm_b_glu[l][:, None], mem_q_norm_g[l][None], w_branch[l].astype(BF16), GATE_FOLD * b_merge[l],
                    (GATE_FOLD * w_out[l]).astype(BF16), seq, mem_len, _tile(seq, 512))
    return x2.reshape(batch, seq, d)
```

```python
import functools
import math

import numpy as np
import jax
import jax.numpy as jnp
from jax import lax
from jax.experimental import pallas as pl
from jax.experimental.pallas import tpu as pltpu

F32 = jnp.float32
BF16 = jnp.bfloat16

D_MODEL = 1024
BRANCH_W = D_MODEL // 2
N_BRANCH = 4
N_IN_PIECES = 12
CHUNK = 64
SSM_GROUP = 16
SSM_GROUPS = BRANCH_W // SSM_GROUP
SSM_STATE = 64
HEADS = 4
HEAD_W = BRANCH_W // HEADS
DIFF_DH = HEAD_W // 2
ROPE_THETA = 10000.0
EPS = 1e-6
NEG_INF = -1e30
GATE_FOLD = 0.5
IN_GATE_PIECES = (1, 5, 9, 11)
LOG2_E = math.log2(math.e)
DIFF_Q_SCALE = DIFF_DH ** -0.5 * LOG2_E
MAX_SCORE_RANGE_LOG2 = 100.0

LANES = 128
S5_CHUNK = LANES
S5_FLAT = SSM_GROUP * S5_CHUNK
P_RET_Q, P_RET_K, P_RET_V, P_RET_G, P_DIF_Q, P_DIF_K, P_DIF_V, P_DIF_G, P_MEM_Q, P_MEM_G, P_MERGE = range(11)

V7X_VMEM_BYTES = 64 * 1024 * 1024
VMEM_LIMIT = V7X_VMEM_BYTES * 7 // 8

NT_DIMS = (((1,), (1,)), ((), ()))
TN_DIMS = (((0,), (0,)), ((), ()))


def _params(*sem):
    return pltpu.CompilerParams(dimension_semantics=sem, vmem_limit_bytes=VMEM_LIMIT)


def _rms(x, eps=EPS):
    return x * lax.rsqrt(jnp.mean(x * x, axis=-1, keepdims=True) + eps)


def _sigmoid(x):
    return 0.5 * jnp.tanh(0.5 * x) + 0.5


def _silu_of_half(xh):
    return xh * (1.0 + jnp.tanh(xh))


def _in_proj_kernel(x_ref, g_ref, w_ref, o_ref, h_ref):
    @pl.when(pl.program_id(1) == 0)
    def _():
        h_ref[...] = (_rms(x_ref[...]) * g_ref[...]).astype(BF16)

    o_ref[...] = jnp.dot(h_ref[...], w_ref[...], preferred_element_type=F32).astype(o_ref.dtype)


def _in_proj(x2, g, w_bf, tm, tn):
    n, d = x2.shape
    cols = w_bf.shape[1]
    return pl.pallas_call(
        _in_proj_kernel,
        grid=(n // tm, cols // tn),
        in_specs=[pl.BlockSpec((tm, d), lambda i, j: (i, 0)),
                  pl.BlockSpec((1, d), lambda i, j: (0, 0)),
                  pl.BlockSpec((d, tn), lambda i, j: (0, j))],
        out_specs=pl.BlockSpec((tm, tn), lambda i, j: (i, j)),
        out_shape=jax.ShapeDtypeStruct((n, cols), BF16),
        scratch_shapes=[pltpu.VMEM((tm, d), BF16)],
        compiler_params=_params("parallel", "arbitrary"),
        name="in_proj",
    )(x2, g, w_bf)


def _in_proj_t_kernel(x_ref, g_ref, wt_ref, u_ref, gate_ref):
    h = (_rms(x_ref[...]) * g_ref[...]).astype(BF16)
    res = lax.dot_general(wt_ref[...], h, NT_DIMS, preferred_element_type=F32)
    tm = res.shape[1]
    u_ref[...] = res[:BRANCH_W].reshape(BRANCH_W, tm // S5_CHUNK, S5_CHUNK).astype(u_ref.dtype)
    gate_ref[...] = res[BRANCH_W:].astype(gate_ref.dtype)


def _in_proj_t(x2, g, wt_bf, tm):
    n, d = x2.shape
    rows = wt_bf.shape[0]
    return pl.pallas_call(
        _in_proj_t_kernel,
        grid=(n // tm,),
        in_specs=[pl.BlockSpec((tm, d), lambda i: (i, 0)),
                  pl.BlockSpec((1, d), lambda i: (0, 0)),
                  pl.BlockSpec((rows, d), lambda i: (0, 0))],
        out_specs=[pl.BlockSpec((BRANCH_W, tm // S5_CHUNK, S5_CHUNK), lambda i: (0, i, 0)),
                   pl.BlockSpec((BRANCH_W, tm), lambda i: (0, i))],
        out_shape=[jax.ShapeDtypeStruct((BRANCH_W, n // S5_CHUNK, S5_CHUNK), BF16),
                   jax.ShapeDtypeStruct((BRANCH_W, n), BF16)],
        compiler_params=_params("parallel"),
        name="in_proj_t",
    )(x2, g, wt_bf)


def _rope_table_kernel(inv_ref, sgn_ref, cos_ref, sin_ref):
    ts = cos_ref.shape[0]
    pos = (lax.broadcasted_iota(jnp.int32, (ts, LANES), 0) + pl.program_id(1) * ts).astype(F32)
    ang = pos * inv_ref[...]
    cos_ref[...] = jnp.cos(ang)
    sin_ref[...] = jnp.sin(ang) * sgn_ref[...]


def _rope_tables(seq, inv, sgn, ts):
    spec_in = pl.BlockSpec((None, 1, LANES), lambda t, i: (t, 0, 0))
    spec_out = pl.BlockSpec((None, ts, LANES), lambda t, i: (t, i, 0))
    return pl.pallas_call(
        _rope_table_kernel,
        grid=(2, seq // ts),
        in_specs=[spec_in, spec_in],
        out_specs=[spec_out, spec_out],
        out_shape=[jax.ShapeDtypeStruct((2, seq, LANES), F32)] * 2,
        compiler_params=_params("parallel", "parallel"),
        name="rope_tables",
    )(inv, sgn)


def _qk_prep_kernel(rq_ref, rk_ref, dq_ref, dk_ref, cos_ref, sin_ref, qg_ref, kg_ref, perm_ref,
                    orq_ref, ork_ref, odq_ref, odk_ref):
    two = lambda t: jnp.concatenate([t, t], axis=1)
    cos_d, sin_d, cos_r, sin_r = two(cos_ref[0]), two(sin_ref[0]), two(cos_ref[1]), two(sin_ref[1])
    k_scale = HEAD_W ** -0.5
    cos_rk, sin_rk = cos_r * k_scale, sin_r * k_scale
    q_cos, q_sin = cos_d * (two(qg_ref[0]) * DIFF_Q_SCALE), sin_d * (two(qg_ref[1]) * DIFF_Q_SCALE)
    k_cos, k_sin = cos_d * two(kg_ref[0]), sin_d * two(kg_ref[1])

    def shuffled(x, which):
        return jnp.dot(x, perm_ref[which], preferred_element_type=F32)

    def rope_r(x, c, s):
        return (x.astype(F32) * c + shuffled(x, 0) * s).astype(BF16)

    def norm_rope_d(x, c, s):
        xf = x.astype(F32)
        ms = shuffled((xf * xf).astype(BF16), 2)
        return (lax.rsqrt(ms + EPS) * (xf * c + shuffled(x, 1) * s)).astype(BF16)

    for hp in range(HEADS // 2):
        sl = slice(hp * 2 * HEAD_W, (hp + 1) * 2 * HEAD_W)
        orq_ref[:, sl] = rope_r(rq_ref[:, sl], cos_r, sin_r)
        ork_ref[:, sl] = rope_r(rk_ref[:, sl], cos_rk, sin_rk)
        odq_ref[:, sl] = norm_rope_d(dq_ref[:, sl], q_cos, q_sin)
        odk_ref[:, sl] = norm_rope_d(dk_ref[:, sl], k_cos, k_sin)


def _qk_prep_consts():
    lane = np.arange(LANES)
    half_rot = (lane + HEAD_W // 2) % LANES
    partner = np.where(lane % DIFF_DH < DIFF_DH // 2, lane + DIFF_DH // 2, lane - DIFF_DH // 2)
    perms = np.zeros((3, LANES, LANES), np.float32)
    perms[0, half_rot, lane] = 1.0
    perms[1, partner, lane] = 1.0
    perms[2] = ((lane[:, None] // DIFF_DH) == (lane[None, :] // DIFF_DH)) / DIFF_DH
    two_heads = np.zeros((3, 2 * LANES, 2 * LANES), np.float32)
    two_heads[:, :LANES, :LANES] = perms
    two_heads[:, LANES:, LANES:] = perms
    return jnp.asarray(two_heads, BF16), partner


def _qk_prep(proj, cos_t, sin_t, qg, kg, seq, ts):
    n = proj.shape[0]
    nsb = seq // ts
    perms, partner = _qk_prep_consts()
    with_partner = lambda g: jnp.stack([jnp.tile(g, 2), jnp.tile(g, 2)[partner]])[:, None, :]

    def piece(p):
        return pl.BlockSpec((ts, BRANCH_W), lambda i: (i, p))

    tab = pl.BlockSpec((2, ts, LANES), lambda i: (0, i % nsb, 0))
    vec = pl.BlockSpec((2, 1, LANES), lambda i: (0, 0, 0))
    out = pl.BlockSpec((ts, BRANCH_W), lambda i: (i, 0))
    qg, kg = with_partner(qg), with_partner(kg)
    return pl.pallas_call(
        _qk_prep_kernel,
        grid=(n // ts,),
        in_specs=[piece(P_RET_Q), piece(P_RET_K), piece(P_DIF_Q), piece(P_DIF_K), tab, tab, vec, vec,
                  pl.BlockSpec(perms.shape, lambda i: (0, 0, 0))],
        out_specs=[out] * 4,
        out_shape=[jax.ShapeDtypeStruct((n, BRANCH_W), BF16)] * 4,
        compiler_params=_params("parallel"),
        name="qk_prep",
    )(proj, proj, proj, proj, cos_t, sin_t, qg, kg, perms)


def _s5_prep_kernel(lam_c_ref, lam_r_ref, b_ref, bt_ref, c_ref, ct_ref,
                    kern_ref, p_ref, q_ref, dec_ref):
    T = S5_CHUNK

    def disc(lam_re, lam_im, log_dt):
        lr = jnp.minimum(lam_re, -1e-4)
        li = lam_im
        dt = jnp.exp(log_dt)
        mag = jnp.exp(lr * dt)
        ab_re = mag * jnp.cos(li * dt)
        ab_im = mag * jnp.sin(li * dt)
        nr = ab_re - 1.0
        ni = ab_im
        den = lr * lr + li * li
        f_re = (nr * lr + ni * li) / den
        f_im = (ni * lr - nr * li) / den
        return lr * dt, li * dt, f_re, f_im

    def powers(ldt_re, ldt_im, k):
        mag = jnp.exp(k * ldt_re)
        return mag * jnp.cos(k * ldt_im), mag * jnp.sin(k * ldt_im)

    ldr, ldi, f_re, f_im = disc(lam_c_ref[0], lam_c_ref[1], lam_c_ref[2])
    b_re, b_im = b_ref[0], b_ref[1]
    bb_re = f_re * b_re - f_im * b_im
    bb_im = f_re * b_im + f_im * b_re
    k_row = lax.broadcasted_iota(jnp.int32, (1, T), 1).astype(F32)
    pw_re, pw_im = powers(ldr, ldi, k_row)
    pw1_re, pw1_im = powers(ldr, ldi, k_row + 1.0)
    c_re, c_im = c_ref[0], c_ref[1]
    ct_re, ct_im = ct_ref[0], ct_ref[1]

    for c in range(SSM_GROUP):
        w_re = pw_re * bb_re[:, c:c + 1] - pw_im * bb_im[:, c:c + 1]
        w_im = pw_re * bb_im[:, c:c + 1] + pw_im * bb_re[:, c:c + 1]
        kern = (jnp.dot(c_re, w_re, preferred_element_type=F32, precision=lax.Precision.HIGHEST)
                - jnp.dot(c_im, w_im, preferred_element_type=F32, precision=lax.Precision.HIGHEST))
        for co in range(SSM_GROUP):
            kern_ref[c, :, co * T:(co + 1) * T] = kern[co:co + 1, :]
        q_ref[0, :, c * T:(c + 1) * T] = (ct_re[:, c:c + 1] * pw1_re - ct_im[:, c:c + 1] * pw1_im).astype(q_ref.dtype)
        q_ref[1, :, c * T:(c + 1) * T] = (-ct_re[:, c:c + 1] * pw1_im - ct_im[:, c:c + 1] * pw1_re).astype(q_ref.dtype)

    ldr_r, ldi_r, fr_r, fi_r = disc(lam_r_ref[0], lam_r_ref[1], lam_r_ref[2])
    bt_re, bt_im = bt_ref[0], bt_ref[1]
    bbt_re = fr_r * bt_re - fi_r * bt_im
    bbt_im = fr_r * bt_im + fi_r * bt_re
    k_col = (T - 1.0) - lax.broadcasted_iota(jnp.int32, (T, 1), 0).astype(F32)
    pr_re, pr_im = powers(ldr_r, ldi_r, k_col)
    for c in range(SSM_GROUP):
        p_ref[0, c * T:(c + 1) * T, :] = (pr_re * bbt_re[c:c + 1, :] - pr_im * bbt_im[c:c + 1, :]).astype(p_ref.dtype)
        p_ref[1, c * T:(c + 1) * T, :] = (pr_re * bbt_im[c:c + 1, :] + pr_im * bbt_re[c:c + 1, :]).astype(p_ref.dtype)
    d_re, d_im = powers(ldr_r, ldi_r, float(T))
    dec_ref[0] = d_re
    dec_ref[1] = d_im


def _s5_prep(lam_re, lam_im, log_dt, b_re, b_im, c_re, c_im):
    g, p = lam_re.shape
    ldt = jnp.broadcast_to(log_dt[:, None], (g, p))
    lam = jnp.stack([lam_re, lam_im, ldt], axis=1)
    lam_c = lam[..., None]
    lam_r = lam[:, :, None, :]
    b = jnp.stack([b_re, b_im], axis=1)
    bt = jnp.swapaxes(b, 2, 3)
    c = jnp.stack([c_re, c_im], axis=1)
    ct = jnp.swapaxes(c, 2, 3)

    def spec(shape):
        return pl.BlockSpec((None,) + shape, lambda i: (i,) + (0,) * len(shape))

    return pl.pallas_call(
        _s5_prep_kernel,
        grid=(g,),
        in_specs=[spec((3, p, 1)), spec((3, 1, p)), spec((2, p, SSM_GROUP)), spec((2, SSM_GROUP, p)),
                  spec((2, SSM_GROUP, p)), spec((2, p, SSM_GROUP))],
        out_specs=[spec((SSM_GROUP, 1, S5_FLAT)), spec((2, S5_FLAT, p)), spec((2, p, S5_FLAT)), spec((2, 1, p))],
        out_shape=[jax.ShapeDtypeStruct((g, SSM_GROUP, 1, S5_FLAT), F32),
                   jax.ShapeDtypeStruct((g, 2, S5_FLAT, p), BF16),
                   jax.ShapeDtypeStruct((g, 2, p, S5_FLAT), BF16),
                   jax.ShapeDtypeStruct((g, 2, 1, p), F32)],
        compiler_params=_params("parallel"),
        name="s5_prep",
    )(lam_c, lam_r, b, bt, c, ct)


def _s5_conv_kernel(d_ref, u_ref, kern_ref, p_ref, q_ref, dec_ref, y_ref, v_ref, s_ref, *, batch, n_chunks):
    g = pl.program_id(0)
    T = S5_CHUNK
    u_flat = jnp.concatenate([u_ref[c] for c in range(SSM_GROUP)], axis=-1)
    jj = lax.broadcasted_iota(jnp.int32, (T, S5_FLAT), 0)
    causal = (lax.broadcasted_iota(jnp.int32, (T, S5_FLAT), 1) % T) >= jj

    def toeplitz_rows(c):
        row = jnp.broadcast_to(kern_ref[c], (T, S5_FLAT))
        toep = pltpu.roll(row, 0, 1, stride=1, stride_axis=0)
        return jnp.where(causal, toep, 0.0).astype(BF16)

    rows = batch * n_chunks
    for ri in range(2):
        v_ref[ri] = jnp.dot(u_flat, p_ref[ri], preferred_element_type=F32).reshape(batch, n_chunks, SSM_STATE)
    dr, di = dec_ref[0], dec_ref[1]

    def body(ch, carry):
        sr, si = carry
        s_ref[0, :, pl.ds(ch, 1), :] = sr
        s_ref[1, :, pl.ds(ch, 1), :] = si
        vr = v_ref[0, :, pl.ds(ch, 1), :]
        vi = v_ref[1, :, pl.ds(ch, 1), :]
        return dr * sr - di * si + vr, dr * si + di * sr + vi

    zero = jnp.zeros((batch, 1, SSM_STATE), F32)
    lax.fori_loop(0, n_chunks, body, (zero, zero))
    y = None
    for c in range(0, SSM_GROUP, 2):
        lhs = jnp.concatenate([u_ref[c], u_ref[c + 1]], axis=-1)
        rhs = jnp.concatenate([toeplitz_rows(c), toeplitz_rows(c + 1)], axis=0)
        part = jnp.dot(lhs, rhs, preferred_element_type=F32)
        y = part if y is None else y + part
    for ri in range(2):
        s_start = s_ref[ri].reshape(rows, SSM_STATE).astype(BF16)
        y += jnp.dot(s_start, q_ref[ri], preferred_element_type=F32)
    slabs = [y[:, c * T:(c + 1) * T] + d_ref[g * SSM_GROUP + c] * u_ref[c].astype(F32) for c in range(SSM_GROUP)]
    y_ref[...] = jnp.stack(slabs, axis=0).reshape(SSM_GROUP, rows * T)


def _s5_conv(u3, d_skip, kern, p, q, dec, batch, seq):
    r = u3.shape[1]
    n = r * S5_CHUNK
    g = SSM_GROUPS

    def spec(shape):
        return pl.BlockSpec((None,) + shape, lambda i: (i,) + (0,) * len(shape))

    return pl.pallas_call(
        functools.partial(_s5_conv_kernel, batch=batch, n_chunks=seq // S5_CHUNK),
        grid=(g,),
        in_specs=[pl.BlockSpec(memory_space=pltpu.SMEM),
                  pl.BlockSpec((SSM_GROUP, r, S5_CHUNK), lambda i: (i, 0, 0)),
                  spec((SSM_GROUP, 1, S5_FLAT)), spec((2, S5_FLAT, SSM_STATE)), spec((2, SSM_STATE, S5_FLAT)),
                  spec((2, 1, SSM_STATE))],
        out_specs=pl.BlockSpec((SSM_GROUP, n), lambda i: (i, 0)),
        out_shape=jax.ShapeDtypeStruct((BRANCH_W, n), F32),
        scratch_shapes=[pltpu.VMEM((2, batch, seq // S5_CHUNK, SSM_STATE), F32)] * 2,
        compiler_params=_params("parallel"),
        name="s5_conv",
    )(d_skip, u3, kern, p, q, dec)


def _retention_kernel(q_ref, k_ref, v_ref, mask_ref, qd_ref, kd_ref, cd_ref, o_ref, st_ref):
    @pl.when(pl.program_id(1) == 0)
    def _():
        st_ref[...] = jnp.zeros_like(st_ref)

    for bi in range(q_ref.shape[0]):
        for h in range(HEADS):
            sl = slice(h * HEAD_W, (h + 1) * HEAD_W)
            q, k, v = q_ref[bi, :, sl], k_ref[bi, :, sl], v_ref[bi, :, sl]
            scores = lax.dot_general(q, k, NT_DIMS, preferred_element_type=F32) * mask_ref[h]
            st = st_ref[bi, h]
            o = (jnp.dot(scores.astype(BF16), v, preferred_element_type=F32)
                 + jnp.dot(q, st.astype(BF16), preferred_element_type=F32) * qd_ref[h])
            kd = (k.astype(F32) * kd_ref[h]).astype(BF16)
            st_ref[bi, h] = st * cd_ref[h] + lax.dot_general(kd, v, TN_DIMS, preferred_element_type=F32)
            o_ref[bi, :, sl] = _rms(o).astype(o_ref.dtype)


def _retention_consts(blk):
    log_gamma = np.log1p(-(2.0 ** (-5.0 - np.arange(HEADS, dtype=np.float64))))
    idx = np.arange(blk, dtype=np.float64)
    diff = idx[:, None] - idx[None, :]
    same = (np.arange(blk)[:, None] // CHUNK) == (np.arange(blk)[None, :] // CHUNK)
    past = (np.arange(blk)[None, :] // CHUNK) < (np.arange(blk)[:, None] // CHUNK)
    expo = np.where(same, np.abs(diff), diff)
    mask = np.where(same | past, np.exp(expo[None] * log_gamma[:, None, None]), 0.0)
    q_dec = np.exp((idx + 1.0)[None, :] * log_gamma[:, None])
    k_dec = np.exp((blk - 1.0 - idx)[None, :] * log_gamma[:, None])
    c_dec = np.exp(blk * log_gamma)
    bc = lambda a: np.broadcast_to(a[:, :, None], (HEADS, blk, HEAD_W))
    return (jnp.asarray(mask, F32), jnp.asarray(bc(q_dec), F32), jnp.asarray(bc(k_dec), F32),
            jnp.asarray(np.broadcast_to(c_dec[:, None, None], (HEADS, 1, HEAD_W)), F32))


def _retention(rq, rk, proj, batch, seq, blk, bpb):
    n = rq.shape[0]
    nb = seq // blk
    mask, q_dec, k_dec, c_dec = _retention_consts(blk)
    tok = pl.BlockSpec((bpb, blk, BRANCH_W), lambda b, l: (b, l, 0))
    full = lambda a: pl.BlockSpec(a.shape, lambda b, l: (0,) * a.ndim)
    as3 = lambda a: a.reshape(batch, seq, a.shape[-1])
    out = pl.pallas_call(
        _retention_kernel,
        grid=(batch // bpb, nb),
        in_specs=[tok, tok, pl.BlockSpec((bpb, blk, BRANCH_W), lambda b, l: (b, l, P_RET_V)),
                  full(mask), full(q_dec), full(k_dec), full(c_dec)],
        out_specs=tok,
        out_shape=jax.ShapeDtypeStruct((batch, seq, BRANCH_W), BF16),
        scratch_shapes=[pltpu.VMEM((bpb, HEADS, HEAD_W, HEAD_W), F32)],
        compiler_params=_params("parallel", "arbitrary"),
        name="retention",
    )(as3(rq), as3(rk), as3(proj), mask, q_dec, k_dec, c_dec)
    return out.reshape(n, BRANCH_W)


def _diff_attn_kernel(bound_ref, q_ref, k_ref, v_ref, lq1_ref, lk1_ref, lq2_ref, lk2_ref, hg_ref, o_ref,
                      acc_ref, *stat_refs, tq, tk, lambda_init, fixed_stabiliser):
    i = pl.program_id(2)
    n_sub = tq // tk
    q = q_ref[...]
    lane = lax.broadcasted_iota(jnp.int32, (tq, HEAD_W), 1)
    qs = (jnp.where(lane < DIFF_DH, q, jnp.zeros_like(q)), jnp.where(lane < DIFF_DH, jnp.zeros_like(q), q))
    acc_ref[...] = jnp.zeros_like(acc_ref)
    if fixed_stabiliser:
        shift = bound_ref[0, 0]
    else:
        m_ref, = stat_refs
        m_ref[...] = jnp.full_like(m_ref, NEG_INF)
    qc = lax.broadcasted_iota(jnp.int32, (tq, tk), 0) // CHUNK
    kc = lax.broadcasted_iota(jnp.int32, (tq, tk), 1) // CHUNK

    def tile(j, diag):
        rows = pl.ds(pl.multiple_of(j * tk, tk), tk)
        kt = k_ref[rows, :]
        vt = v_ref[rows, :]
        v_ones = jnp.concatenate([vt, jnp.ones_like(vt)], axis=1)
        r0 = 0 if diag is None else diag * tk
        visible = None if diag is None else ((kc + diag * (tk // CHUNK)) <= qc)[r0:]
        for mp in range(2):
            s = lax.dot_general(qs[mp][r0:], kt, NT_DIMS, preferred_element_type=F32)
            if fixed_stabiliser:
                p = jnp.exp2(s - shift)
                if visible is not None:
                    p = jnp.where(visible, p, 0.0)
                acc_ref[mp, r0:] += jnp.dot(p.astype(BF16), v_ones, preferred_element_type=F32)
            else:
                if visible is not None:
                    s = jnp.where(visible, s, NEG_INF)
                m_prev = m_ref[mp, r0:]
                m_new = jnp.maximum(m_prev, jnp.max(s, axis=-1, keepdims=True))
                p = jnp.exp2(s - m_new)
                acc_ref[mp, r0:] = (jnp.exp2(m_prev - m_new) * acc_ref[mp, r0:]
                                    + jnp.dot(p.astype(BF16), v_ones, preferred_element_type=F32))
                m_ref[mp, r0:] = m_new

    def body(jj, carry):
        for u in range(n_sub):
            tile(jj * n_sub + u, None)
        return carry

    lax.fori_loop(0, i, body, 0)
    for d in range(n_sub):
        tile(i * n_sub + d, d)

    lam = (jnp.exp(jnp.sum(lq1_ref[...] * lk1_ref[...], axis=-1, keepdims=True))
           - jnp.exp(jnp.sum(lq2_ref[...] * lk2_ref[...], axis=-1, keepdims=True)) + lambda_init)
    o = (acc_ref[0, :, :HEAD_W] / acc_ref[0, :, HEAD_W:]
         - lam * (acc_ref[1, :, :HEAD_W] / acc_ref[1, :, HEAD_W:]))
    o_ref[...] = (_rms(o) * hg_ref[...] * (1.0 - lambda_init)).astype(o_ref.dtype)


def _diff_attn(dq, dk, proj, qg, kg, lq1, lk1, lq2, lk2, hg, batch, seq, tq, tk, lambda_init):
    n = dq.shape[0]
    nq = seq // tq
    bound = (1.01 * DIFF_DH * DIFF_Q_SCALE * jnp.max(jnp.abs(qg)) * jnp.max(jnp.abs(kg))).reshape(1, 1)
    vec64 = pl.BlockSpec((1, DIFF_DH), lambda b, h, i: (0, 0))

    def call(fixed_stabiliser):
        stat_scratch = [] if fixed_stabiliser else [pltpu.VMEM((2, tq, 1), F32)]
        return pl.pallas_call(
            functools.partial(_diff_attn_kernel, tq=tq, tk=tk, lambda_init=lambda_init,
                              fixed_stabiliser=fixed_stabiliser),
            grid=(batch, HEADS, nq),
            in_specs=[pl.BlockSpec(memory_space=pltpu.SMEM),
                      pl.BlockSpec((tq, HEAD_W), lambda b, h, i: (b * nq + i, h)),
                      pl.BlockSpec((seq, HEAD_W), lambda b, h, i: (b, h)),
                      pl.BlockSpec((seq, HEAD_W), lambda b, h, i: (b, P_DIF_V * HEADS + h)),
                      vec64, vec64, vec64, vec64,
                      pl.BlockSpec((1, HEAD_W), lambda b, h, i: (0, 0))],
            out_specs=pl.BlockSpec((tq, HEAD_W), lambda b, h, i: (b * nq + i, h)),
            out_shape=jax.ShapeDtypeStruct((n, BRANCH_W), BF16),
            scratch_shapes=[pltpu.VMEM((2, tq, 2 * HEAD_W), F32)] + stat_scratch,
            compiler_params=_params("parallel", "parallel", "arbitrary"),
            name="diff_attn_fixed" if fixed_stabiliser else "diff_attn_online",
        )(bound, dq, dk, proj, lq1, lk1, lq2, lk2, hg)

    return lax.cond(bound[0, 0] * 2.0 <= MAX_SCORE_RANGE_LOG2, lambda: call(True), lambda: call(False))


def _mem_kv_kernel(mem_ref, g_ref, w_ref, kg_ref, k_ref, v_ref):
    h = (_rms(mem_ref[...]) * g_ref[...]).astype(BF16)
    kv = jnp.dot(h, w_ref[...], preferred_element_type=F32)
    for hd in range(HEADS):
        sl = slice(hd * HEAD_W, (hd + 1) * HEAD_W)
        k_ref[:, sl] = (_rms(kv[:, sl]) * kg_ref[...]).astype(BF16)
    v_ref[...] = kv[:, BRANCH_W:].astype(BF16)


def _mem_kv(mem2, g, w_bf, kg, mem_len):
    n, d = mem2.shape
    out = pl.BlockSpec((mem_len, BRANCH_W), lambda i: (i, 0))
    return pl.pallas_call(
        _mem_kv_kernel,
        grid=(n // mem_len,),
        in_specs=[pl.BlockSpec((mem_len, d), lambda i: (i, 0)),
                  pl.BlockSpec((1, d), lambda i: (0, 0)),
                  pl.BlockSpec((d, 2 * BRANCH_W), lambda i: (0, 0)),
                  pl.BlockSpec((1, HEAD_W), lambda i: (0, 0))],
        out_specs=[out, out],
        out_shape=[jax.ShapeDtypeStruct((n, BRANCH_W), BF16)] * 2,
        compiler_params=_params("parallel"),
        name="mem_kv",
    )(mem2, g, w_bf, kg)


def _merge_kernel(x_ref, y_ref, sg_ref, ret_ref, rg_ref, dif_ref, dg_ref, mq_ref, mg_ref, mk_ref, mv_ref,
                  mrg0_ref, mrg1_ref, mrg2_ref, mrg3_ref, wglu_ref, bglu_ref, mqg_ref, wbr_ref, bm_ref, wout_ref,
                  o_ref):
    z = jax.nn.gelu(y_ref[...])
    glu = jnp.dot(wglu_ref[...], z.astype(BF16), preferred_element_type=F32) + bglu_ref[...]
    a_t = (z * _sigmoid(glu)).astype(BF16) * _silu_of_half(sg_ref[...])
    branches = [lax.dot_general(a_t, wbr_ref[0], TN_DIMS, preferred_element_type=F32)]
    b_out = ret_ref[...] * _silu_of_half(rg_ref[...])
    branches.append(jnp.dot(b_out, wbr_ref[1], preferred_element_type=F32))
    c_out = dif_ref[...] * _silu_of_half(dg_ref[...])
    branches.append(jnp.dot(c_out, wbr_ref[2], preferred_element_type=F32))
    heads = []
    for hd in range(HEADS):
        sl = slice(hd * HEAD_W, (hd + 1) * HEAD_W)
        qh = (_rms(mq_ref[:, sl].astype(F32)) * mqg_ref[...] * (HEAD_W ** -0.5)).astype(BF16)
        s = lax.dot_general(qh, mk_ref[:, sl], NT_DIMS, preferred_element_type=F32)
        p = jnp.exp(s - jnp.max(s, axis=-1, keepdims=True))
        o = jnp.dot(p.astype(BF16), mv_ref[:, sl], preferred_element_type=F32)
        heads.append(o / jnp.sum(p, axis=-1, keepdims=True))
    m_out = jnp.concatenate(heads, axis=-1).astype(BF16) * _silu_of_half(mg_ref[...])
    branches.append(jnp.dot(m_out, wbr_ref[3], preferred_element_type=F32))
    merged = None
    for nb, mrg_ref in enumerate((mrg0_ref, mrg1_ref, mrg2_ref, mrg3_ref)):
        gate = 1.0 + jnp.tanh(mrg_ref[...] + bm_ref[nb:nb + 1, :].astype(BF16))
        term = gate * branches[nb].astype(BF16)
        merged = term if merged is None else merged + term
    o_ref[...] = x_ref[...] + jnp.dot(merged, wout_ref[...], preferred_element_type=F32)


def _merge(x2, y_t, sg_t, proj, ret_o, dif_o, mk, mv, wglu_t, bglu, mqg, wbr, bm, wout, seq, mem_len, tm):
    n, d = x2.shape

    def piece(p):
        return pl.BlockSpec((tm, BRANCH_W), lambda i: (i, p))

    def gate_piece(nb):
        return pl.BlockSpec((tm, D_MODEL), lambda i: (i, P_MERGE * BRANCH_W // D_MODEL + nb))

    tok = pl.BlockSpec((tm, BRANCH_W), lambda i: (i, 0))
    mem = pl.BlockSpec((mem_len, BRANCH_W), lambda i: ((i * tm) // seq, 0))
    full = lambda a: pl.BlockSpec(a.shape, lambda i: (0,) * a.ndim)
    return pl.pallas_call(
        _merge_kernel,
        grid=(n // tm,),
        in_specs=[pl.BlockSpec((tm, d), lambda i: (i, 0)),
                  pl.BlockSpec((BRANCH_W, tm), lambda i: (0, i)),
                  pl.BlockSpec((BRANCH_W, tm), lambda i: (0, i)),
                  tok, piece(P_RET_G), tok, piece(P_DIF_G), piece(P_MEM_Q), piece(P_MEM_G), mem, mem,
                  gate_piece(0), gate_piece(1), gate_piece(2), gate_piece(3),
                  full(wglu_t), full(bglu), full(mqg), full(wbr), full(bm), full(wout)],
        out_specs=pl.BlockSpec((tm, d), lambda i: (i, 0)),
        out_shape=jax.ShapeDtypeStruct((n, d), F32),
        compiler_params=_params("parallel"),
        name="merge",
    )(x2, y_t, sg_t, ret_o, proj, dif_o, proj, proj, proj, mk, mv, proj, proj, proj, proj,
      wglu_t, bglu, mqg, wbr, bm, wout)


def _tile(n, pref):
    return pref if n % pref == 0 else n


def kernel(x, mem, norm_g, w_in, ssm_lambda_re, ssm_lambda_im, ssm_log_dt, ssm_b_re, ssm_b_im, ssm_c_re,
           ssm_c_im, ssm_d, ssm_w_glu, ssm_b_glu, diff_q_norm_g, diff_k_norm_g, diff_lambda_q1,
           diff_lambda_k1, diff_lambda_q2, diff_lambda_k2, diff_head_norm_g, mem_norm_g, w_mem_kv,
           mem_q_norm_g, mem_k_norm_g, w_branch, b_merge, w_out):
    batch, seq, d = x.shape
    mem_len = mem.shape[1]
    depth = w_in.shape[0]
    n = batch * seq
    assert d == D_MODEL and seq % S5_CHUNK == 0

    diff_inv = ROPE_THETA ** (-jnp.arange(0, DIFF_DH, 2, dtype=F32) / DIFF_DH)
    ret_inv = 1.0 / (ROPE_THETA ** jnp.linspace(0.0, 1.0, HEAD_W // 2, dtype=F32))
    inv = jnp.stack([jnp.tile(diff_inv, LANES // diff_inv.shape[0]),
                     jnp.tile(ret_inv, LANES // ret_inv.shape[0])])[:, None, :]
    lane = np.arange(LANES)
    sgn = np.stack([np.where(lane % DIFF_DH < DIFF_DH // 2, -1.0, 1.0),
                    np.where(lane < HEAD_W // 2, -1.0, 1.0)])[:, None, :]
    cos_t, sin_t = _rope_tables(seq, inv, jnp.asarray(sgn, F32), _tile(seq, 1024))

    piece = np.arange(w_in.shape[2]) // BRANCH_W
    halved = (piece >= N_IN_PIECES) | np.isin(piece, IN_GATE_PIECES)
    in_col_scale = jnp.asarray(np.where(halved, GATE_FOLD, 1.0), F32)

    x2 = x.reshape(n, d)
    mem2 = mem.reshape(batch * mem_len, d)
    for l in range(depth):
        lambda_init = 0.8 - 0.6 * math.exp(-0.3 * l)
        g = norm_g[l][None, :]
        w = w_in[l] * in_col_scale[None, :]
        proj = _in_proj(x2, g, w[:, 2 * BRANCH_W:].astype(BF16), _tile(n, 2048), 1536)
        u3, sg_t = _in_proj_t(x2, g, w[:, :2 * BRANCH_W].T.astype(BF16), _tile(n, 2048))

        kern_op, p_op, q_op, dec = _s5_prep(ssm_lambda_re[l], ssm_lambda_im[l], ssm_log_dt[l], ssm_b_re[l],
                                         ssm_b_im[l], ssm_c_re[l], ssm_c_im[l])
        y_t = _s5_conv(u3, ssm_d[l], kern_op, p_op, q_op, dec, batch, seq)

        rq, rk, dq, dk = _qk_prep(proj, cos_t, sin_t, diff_q_norm_g[l], diff_k_norm_g[l], seq, _tile(seq, 1024))
        ret_o = _retention(rq, rk, proj, batch, seq, _tile(seq, 256), 2 if batch % 2 == 0 else 1)
        dif_o = _diff_attn(dq, dk, proj, diff_q_norm_g[l], diff_k_norm_g[l], diff_lambda_q1[l][None],
                           diff_lambda_k1[l][None], diff_lambda_q2[l][None], diff_lambda_k2[l][None],
                           diff_head_norm_g[l][None], batch, seq, _tile(seq, 2048), _tile(seq, 512), lambda_init)
        mk, mv = _mem_kv(mem2, mem_norm_g[l][None, :], w_mem_kv[l].astype(BF16), mem_k_norm_g[l][None], mem_len)
        x2 = _merge(x2, y_t, sg_t, proj, ret_o, dif_o, mk, mv, ssm_w_glu[l].T.astype(BF16),
                    ssm_b_glu[l][:, None], mem_q_norm_g[l][None], w_branch[l].astype(BF16), GATE_FOLD * b_merge[l],
                    (GATE_FOLD * w_out[l]).astype(BF16), seq, mem_len, _tile(seq, 512))
    return x2.reshape(batch, seq, d)
```

```python
import functools
import math

import numpy as np
import jax
import jax.numpy as jnp
from jax import lax
from jax.experimental import pallas as pl
from jax.experimental.pallas import tpu as pltpu

F32 = jnp.float32
BF16 = jnp.bfloat16

D_MODEL = 1024
BRANCH_W = D_MODEL // 2
N_BRANCH = 4
N_IN_PIECES = 12
CHUNK = 64
SSM_GROUP = 16
SSM_GROUPS = BRANCH_W // SSM_GROUP
SSM_STATE = 64
HEADS = 4
HEAD_W = BRANCH_W // HEADS
DIFF_DH = HEAD_W // 2
ROPE_THETA = 10000.0
EPS = 1e-6
NEG_INF = -1e30
GATE_FOLD = 0.5
IN_GATE_PIECES = (1, 5, 9, 11)
LOG2_E = math.log2(math.e)
DIFF_Q_SCALE = DIFF_DH ** -0.5 * LOG2_E
MAX_SCORE_RANGE_LOG2 = 100.0

LANES = 128
S5_CHUNK = LANES
S5_FLAT = SSM_GROUP * S5_CHUNK
P_RET_Q, P_RET_K, P_RET_V, P_RET_G, P_DIF_Q, P_DIF_K, P_DIF_V, P_DIF_G, P_MEM_Q, P_MEM_G, P_MERGE = range(11)

V7X_VMEM_BYTES = 64 * 1024 * 1024
VMEM_LIMIT = V7X_VMEM_BYTES * 7 // 8

NT_DIMS = (((1,), (1,)), ((), ()))
TN_DIMS = (((0,), (0,)), ((), ()))


def _params(*sem):
    return pltpu.CompilerParams(dimension_semantics=sem, vmem_limit_bytes=VMEM_LIMIT)


def _rms(x, eps=EPS):
    return x * lax.rsqrt(jnp.mean(x * x, axis=-1, keepdims=True) + eps)


def _sigmoid(x):
    return 0.5 * jnp.tanh(0.5 * x) + 0.5


def _silu_of_half(xh):
    return xh * (1.0 + jnp.tanh(xh))


def _in_proj_kernel(x_ref, g_ref, w_ref, o_ref, h_ref):
    @pl.when(pl.program_id(1) == 0)
    def _():
        h_ref[...] = (_rms(x_ref[...]) * g_ref[...]).astype(BF16)

    o_ref[...] = jnp.dot(h_ref[...], w_ref[...], preferred_element_type=F32).astype(o_ref.dtype)


def _in_proj(x2, g, w_bf, tm, tn):
    n, d = x2.shape
    cols = w_bf.shape[1]
    return pl.pallas_call(
        _in_proj_kernel,
        grid=(n // tm, cols // tn),
        in_specs=[pl.BlockSpec((tm, d), lambda i, j: (i, 0)),
                  pl.BlockSpec((1, d), lambda i, j: (0, 0)),
                  pl.BlockSpec((d, tn), lambda i, j: (0, j))],
        out_specs=pl.BlockSpec((tm, tn), lambda i, j: (i, j)),
        out_shape=jax.ShapeDtypeStruct((n, cols), BF16),
        scratch_shapes=[pltpu.VMEM((tm, d), BF16)],
        compiler_params=_params("parallel", "arbitrary"),
        name="in_proj",
    )(x2, g, w_bf)


def _in_proj_t_kernel(x_ref, g_ref, wt_ref, u_ref, gate_ref):
    h = (_rms(x_ref[...]) * g_ref[...]).astype(BF16)
    res = lax.dot_general(wt_ref[...], h, NT_DIMS, preferred_element_type=F32)
    tm = res.shape[1]
    u_ref[...] = res[:BRANCH_W].reshape(BRANCH_W, tm // S5_CHUNK, S5_CHUNK).astype(u_ref.dtype)
    gate_ref[...] = res[BRANCH_W:].astype(gate_ref.dtype)


def _in_proj_t(x2, g, wt_bf, tm):
    n, d = x2.shape
    rows = wt_bf.shape[0]
    return pl.pallas_call(
        _in_proj_t_kernel,
        grid=(n // tm,),
        in_specs=[pl.BlockSpec((tm, d), lambda i: (i, 0)),
                  pl.BlockSpec((1, d), lambda i: (0, 0)),
                  pl.BlockSpec((rows, d), lambda i: (0, 0))],
        out_specs=[pl.BlockSpec((BRANCH_W, tm // S5_CHUNK, S5_CHUNK), lambda i: (0, i, 0)),
                   pl.BlockSpec((BRANCH_W, tm), lambda i: (0, i))],
        out_shape=[jax.ShapeDtypeStruct((BRANCH_W, n // S5_CHUNK, S5_CHUNK), BF16),
                   jax.ShapeDtypeStruct((BRANCH_W, n), BF16)],
        compiler_params=_params("parallel"),
        name="in_proj_t",
    )(x2, g, wt_bf)


def _rope_table_kernel(inv_ref, sgn_ref, cos_ref, sin_ref):
    ts = cos_ref.shape[0]
    pos = (lax.broadcasted_iota(jnp.int32, (ts, LANES), 0) + pl.program_id(1) * ts).astype(F32)
    ang = pos * inv_ref[...]
    cos_ref[...] = jnp.cos(ang)
    sin_ref[...] = jnp.sin(ang) * sgn_ref[...]


def _rope_tables(seq, inv, sgn, ts):
    spec_in = pl.BlockSpec((None, 1, LANES), lambda t, i: (t, 0, 0))
    spec_out = pl.BlockSpec((None, ts, LANES), lambda t, i: (t, i, 0))
    return pl.pallas_call(
        _rope_table_kernel,
        grid=(2, seq // ts),
        in_specs=[spec_in, spec_in],
        out_specs=[spec_out, spec_out],
        out_shape=[jax.ShapeDtypeStruct((2, seq, LANES), F32)] * 2,
        compiler_params=_params("parallel", "parallel"),
        name="rope_tables",
    )(inv, sgn)


def _qk_prep_kernel(rq_ref, rk_ref, dq_ref, dk_ref, cos_ref, sin_ref, qg_ref, kg_ref, perm_ref,
                    orq_ref, ork_ref, odq_ref, odk_ref):
    two = lambda t: jnp.concatenate([t, t], axis=1)
    cos_d, sin_d, cos_r, sin_r = two(cos_ref[0]), two(sin_ref[0]), two(cos_ref[1]), two(sin_ref[1])
    k_scale = HEAD_W ** -0.5
    cos_rk, sin_rk = cos_r * k_scale, sin_r * k_scale
    q_cos, q_sin = cos_d * (two(qg_ref[0]) * DIFF_Q_SCALE), sin_d * (two(qg_ref[1]) * DIFF_Q_SCALE)
    k_cos, k_sin = cos_d * two(kg_ref[0]), sin_d * two(kg_ref[1])

    def shuffled(x, which):
        return jnp.dot(x, perm_ref[which], preferred_element_type=F32)

    def rope_r(x, c, s):
        return (x.astype(F32) * c + shuffled(x, 0) * s).astype(BF16)

    def norm_rope_d(x, c, s):
        xf = x.astype(F32)
        ms = shuffled((xf * xf).astype(BF16), 2)
        return (lax.rsqrt(ms + EPS) * (xf * c + shuffled(x, 1) * s)).astype(BF16)

    for hp in range(HEADS // 2):
        sl = slice(hp * 2 * HEAD_W, (hp + 1) * 2 * HEAD_W)
        orq_ref[:, sl] = rope_r(rq_ref[:, sl], cos_r, sin_r)
        ork_ref[:, sl] = rope_r(rk_ref[:, sl], cos_rk, sin_rk)
        odq_ref[:, sl] = norm_rope_d(dq_ref[:, sl], q_cos, q_sin)
        odk_ref[:, sl] = norm_rope_d(dk_ref[:, sl], k_cos, k_sin)


def _qk_prep_consts():
    lane = np.arange(LANES)
    half_rot = (lane + HEAD_W // 2) % LANES
    partner = np.where(lane % DIFF_DH < DIFF_DH // 2, lane + DIFF_DH // 2, lane - DIFF_DH // 2)
    perms = np.zeros((3, LANES, LANES), np.float32)
    perms[0, half_rot, lane] = 1.0
    perms[1, partner, lane] = 1.0
    perms[2] = ((lane[:, None] // DIFF_DH) == (lane[None, :] // DIFF_DH)) / DIFF_DH
    two_heads = np.zeros((3, 2 * LANES, 2 * LANES), np.float32)
    two_heads[:, :LANES, :LANES] = perms
    two_heads[:, LANES:, LANES:] = perms
    return jnp.asarray(two_heads, BF16), partner


def _qk_prep(proj, cos_t, sin_t, qg, kg, seq, ts):
    n = proj.shape[0]
    nsb = seq // ts
    perms, partner = _qk_prep_consts()
    with_partner = lambda g: jnp.stack([jnp.tile(g, 2), jnp.tile(g, 2)[partner]])[:, None, :]

    def piece(p):
        return pl.BlockSpec((ts, BRANCH_W), lambda i: (i, p))

    tab = pl.BlockSpec((2, ts, LANES), lambda i: (0, i % nsb, 0))
    vec = pl.BlockSpec((2, 1, LANES), lambda i: (0, 0, 0))
    out = pl.BlockSpec((ts, BRANCH_W), lambda i: (i, 0))
    qg, kg = with_partner(qg), with_partner(kg)
    return pl.pallas_call(
        _qk_prep_kernel,
        grid=(n // ts,),
        in_specs=[piece(P_RET_Q), piece(P_RET_K), piece(P_DIF_Q), piece(P_DIF_K), tab, tab, vec, vec,
                  pl.BlockSpec(perms.shape, lambda i: (0, 0, 0))],
        out_specs=[out] * 4,
        out_shape=[jax.ShapeDtypeStruct((n, BRANCH_W), BF16)] * 4,
        compiler_params=_params("parallel"),
        name="qk_prep",
    )(proj, proj, proj, proj, cos_t, sin_t, qg, kg, perms)


def _s5_prep_kernel(lam_c_ref, lam_r_ref, b_ref, bt_ref, c_ref, ct_ref,
                    kern_ref, p_ref, q_ref, dec_ref):
    T = S5_CHUNK

    def disc(lam_re, lam_im, log_dt):
        lr = jnp.minimum(lam_re, -1e-4)
        li = lam_im
        dt = jnp.exp(log_dt)
        mag = jnp.exp(lr * dt)
        ab_re = mag * jnp.cos(li * dt)
        ab_im = mag * jnp.sin(li * dt)
        nr = ab_re - 1.0
        ni = ab_im
        den = lr * lr + li * li
        f_re = (nr * lr + ni * li) / den
        f_im = (ni * lr - nr * li) / den
        return lr * dt, li * dt, f_re, f_im

    def powers(ldt_re, ldt_im, k):
        mag = jnp.exp(k * ldt_re)
        return mag * jnp.cos(k * ldt_im), mag * jnp.sin(k * ldt_im)

    ldr, ldi, f_re, f_im = disc(lam_c_ref[0], lam_c_ref[1], lam_c_ref[2])
    b_re, b_im = b_ref[0], b_ref[1]
    bb_re = f_re * b_re - f_im * b_im
    bb_im = f_re * b_im + f_im * b_re
    k_row = lax.broadcasted_iota(jnp.int32, (1, T), 1).astype(F32)
    pw_re, pw_im = powers(ldr, ldi, k_row)
    pw1_re, pw1_im = powers(ldr, ldi, k_row + 1.0)
    c_re, c_im = c_ref[0], c_ref[1]
    ct_re, ct_im = ct_ref[0], ct_ref[1]

    for c in range(SSM_GROUP):
        w_re = pw_re * bb_re[:, c:c + 1] - pw_im * bb_im[:, c:c + 1]
        w_im = pw_re * bb_im[:, c:c + 1] + pw_im * bb_re[:, c:c + 1]
        kern = (jnp.dot(c_re, w_re, preferred_element_type=F32, precision=lax.Precision.HIGHEST)
                - jnp.dot(c_im, w_im, preferred_element_type=F32, precision=lax.Precision.HIGHEST))
        for co in range(SSM_GROUP):
            kern_ref[c, :, co * T:(co + 1) * T] = kern[co:co + 1, :]
        q_ref[0, :, c * T:(c + 1) * T] = (ct_re[:, c:c + 1] * pw1_re - ct_im[:, c:c + 1] * pw1_im).astype(q_ref.dtype)
        q_ref[1, :, c * T:(c + 1) * T] = (-ct_re[:, c:c + 1] * pw1_im - ct_im[:, c:c + 1] * pw1_re).astype(q_ref.dtype)

    ldr_r, ldi_r, fr_r, fi_r = disc(lam_r_ref[0], lam_r_ref[1], lam_r_ref[2])
    bt_re, bt_im = bt_ref[0], bt_ref[1]
    bbt_re = fr_r * bt_re - fi_r * bt_im
    bbt_im = fr_r * bt_im + fi_r * bt_re
    k_col = (T - 1.0) - lax.broadcasted_iota(jnp.int32, (T, 1), 0).astype(F32)
    pr_re, pr_im = powers(ldr_r, ldi_r, k_col)
    for c in range(SSM_GROUP):
        p_ref[0, c * T:(c + 1) * T, :] = (pr_re * bbt_re[c:c + 1, :] - pr_im * bbt_im[c:c + 1, :]).astype(p_ref.dtype)
        p_ref[1, c * T:(c + 1) * T, :] = (pr_re * bbt_im[c:c + 1, :] + pr_im * bbt_re[c:c + 1, :]).astype(p_ref.dtype)
    d_re, d_im = powers(ldr_r, ldi_r, float(T))
    dec_ref[0] = d_re
    dec_ref[1] = d_im


def _s5_prep(lam_re, lam_im, log_dt, b_re, b_im, c_re, c_im):
    g, p = lam_re.shape
    ldt = jnp.broadcast_to(log_dt[:, None], (g, p))
    lam = jnp.stack([lam_re, lam_im, ldt], axis=1)
    lam_c = lam[..., None]
    lam_r = lam[:, :, None, :]
    b = jnp.stack([b_re, b_im], axis=1)
    bt = jnp.swapaxes(b, 2, 3)
    c = jnp.stack([c_re, c_im], axis=1)
    ct = jnp.swapaxes(c, 2, 3)

    def spec(shape):
        return pl.BlockSpec((None,) + shape, lambda i: (i,) + (0,) * len(shape))

    return pl.pallas_call(
        _s5_prep_kernel,
        grid=(g,),
        in_specs=[spec((3, p, 1)), spec((3, 1, p)), spec((2, p, SSM_GROUP)), spec((2, SSM_GROUP, p)),
                  spec((2, SSM_GROUP, p)), spec((2, p, SSM_GROUP))],
        out_specs=[spec((SSM_GROUP, 1, S5_FLAT)), spec((2, S5_FLAT, p)), spec((2, p, S5_FLAT)), spec((2, 1, p))],
        out_shape=[jax.ShapeDtypeStruct((g, SSM_GROUP, 1, S5_FLAT), F32),
                   jax.ShapeDtypeStruct((g, 2, S5_FLAT, p), BF16),
                   jax.ShapeDtypeStruct((g, 2, p, S5_FLAT), BF16),
                   jax.ShapeDtypeStruct((g, 2, 1, p), F32)],
        compiler_params=_params("parallel"),
        name="s5_prep",
    )(lam_c, lam_r, b, bt, c, ct)


def _s5_conv_kernel(d_ref, u_ref, kern_ref, p_ref, q_ref, dec_ref, y_ref, v_ref, s_ref, *, batch, n_chunks):
    g = pl.program_id(0)
    T = S5_CHUNK
    u_flat = jnp.concatenate([u_ref[c] for c in range(SSM_GROUP)], axis=-1)
    jj = lax.broadcasted_iota(jnp.int32, (T, S5_FLAT), 0)
    causal = (lax.broadcasted_iota(jnp.int32, (T, S5_FLAT), 1) % T) >= jj

    def toeplitz_rows(c):
        row = jnp.broadcast_to(kern_ref[c], (T, S5_FLAT))
        toep = pltpu.roll(row, 0, 1, stride=1, stride_axis=0)
        return jnp.where(causal, toep, 0.0).astype(BF16)

    rows = batch * n_chunks
    for ri in range(2):
        v_ref[ri] = jnp.dot(u_flat, p_ref[ri], preferred_element_type=F32).reshape(batch, n_chunks, SSM_STATE)
    dr, di = dec_ref[0], dec_ref[1]

    def body(ch, carry):
        sr, si = carry
        s_ref[0, :, pl.ds(ch, 1), :] = sr
        s_ref[1, :, pl.ds(ch, 1), :] = si
        vr = v_ref[0, :, pl.ds(ch, 1), :]
        vi = v_ref[1, :, pl.ds(ch, 1), :]
        return dr * sr - di * si + vr, dr * si + di * sr + vi

    zero = jnp.zeros((batch, 1, SSM_STATE), F32)
    lax.fori_loop(0, n_chunks, body, (zero, zero))
    y = None
    for c in range(0, SSM_GROUP, 2):
        lhs = jnp.concatenate([u_ref[c], u_ref[c + 1]], axis=-1)
        rhs = jnp.concatenate([toeplitz_rows(c), toeplitz_rows(c + 1)], axis=0)
        part = jnp.dot(lhs, rhs, preferred_element_type=F32)
        y = part if y is None else y + part
    for ri in range(2):
        s_start = s_ref[ri].reshape(rows, SSM_STATE).astype(BF16)
        y += jnp.dot(s_start, q_ref[ri], preferred_element_type=F32)
    slabs = [y[:, c * T:(c + 1) * T] + d_ref[g * SSM_GROUP + c] * u_ref[c].astype(F32) for c in range(SSM_GROUP)]
    y_ref[...] = jnp.stack(slabs, axis=0).reshape(SSM_GROUP, rows * T)


def _s5_conv(u3, d_skip, kern, p, q, dec, batch, seq):
    r = u3.shape[1]
    n = r * S5_CHUNK
    g = SSM_GROUPS

    def spec(shape):
        return pl.BlockSpec((None,) + shape, lambda i: (i,) + (0,) * len(shape))

    return pl.pallas_call(
        functools.partial(_s5_conv_kernel, batch=batch, n_chunks=seq // S5_CHUNK),
        grid=(g,),
        in_specs=[pl.BlockSpec(memory_space=pltpu.SMEM),
                  pl.BlockSpec((SSM_GROUP, r, S5_CHUNK), lambda i: (i, 0, 0)),
                  spec((SSM_GROUP, 1, S5_FLAT)), spec((2, S5_FLAT, SSM_STATE)), spec((2, SSM_STATE, S5_FLAT)),
                  spec((2, 1, SSM_STATE))],
        out_specs=pl.BlockSpec((SSM_GROUP, n), lambda i: (i, 0)),
        out_shape=jax.ShapeDtypeStruct((BRANCH_W, n), F32),
        scratch_shapes=[pltpu.VMEM((2, batch, seq // S5_CHUNK, SSM_STATE), F32)] * 2,
        compiler_params=_params("parallel"),
        name="s5_conv",
    )(d_skip, u3, kern, p, q, dec)


def _retention_kernel(q_ref, k_ref, v_ref, mask_ref, qd_ref, kd_ref, cd_ref, o_ref, st_ref):
    @pl.when(pl.program_id(1) == 0)
    def _():
        st_ref[...] = jnp.zeros_like(st_ref)

    for bi in range(q_ref.shape[0]):
        for h in range(HEADS):
            sl = slice(h * HEAD_W, (h + 1) * HEAD_W)
            q, k, v = q_ref[bi, :, sl], k_ref[bi, :, sl], v_ref[bi, :, sl]
            scores = lax.dot_general(q, k, NT_DIMS, preferred_element_type=F32) * mask_ref[h]
            st = st_ref[bi, h]
            o = (jnp.dot(scores.astype(BF16), v, preferred_element_type=F32)
                 + jnp.dot(q, st.astype(BF16), preferred_element_type=F32) * qd_ref[h])
            kd = (k.astype(F32) * kd_ref[h]).astype(BF16)
            st_ref[bi, h] = st * cd_ref[h] + lax.dot_general(kd, v, TN_DIMS, preferred_element_type=F32)
            o_ref[bi, :, sl] = _rms(o).astype(o_ref.dtype)


def _retention_consts(blk):
    log_gamma = np.log1p(-(2.0 ** (-5.0 - np.arange(HEADS, dtype=np.float64))))
    idx = np.arange(blk, dtype=np.float64)
    diff = idx[:, None] - idx[None, :]
    same = (np.arange(blk)[:, None] // CHUNK) == (np.arange(blk)[None, :] // CHUNK)
    past = (np.arange(blk)[None, :] // CHUNK) < (np.arange(blk)[:, None] // CHUNK)
    expo = np.where(same, np.abs(diff), diff)
    mask = np.where(same | past, np.exp(expo[None] * log_gamma[:, None, None]), 0.0)
    q_dec = np.exp((idx + 1.0)[None, :] * log_gamma[:, None])
    k_dec = np.exp((blk - 1.0 - idx)[None, :] * log_gamma[:, None])
    c_dec = np.exp(blk * log_gamma)
    bc = lambda a: np.broadcast_to(a[:, :, None], (HEADS, blk, HEAD_W))
    return (jnp.asarray(mask, F32), jnp.asarray(bc(q_dec), F32), jnp.asarray(bc(k_dec), F32),
            jnp.asarray(np.broadcast_to(c_dec[:, None, None], (HEADS, 1, HEAD_W)), F32))


def _retention(rq, rk, proj, batch, seq, blk, bpb):
    n = rq.shape[0]
    nb = seq // blk
    mask, q_dec, k_dec, c_dec = _retention_consts(blk)
    tok = pl.BlockSpec((bpb, blk, BRANCH_W), lambda b, l: (b, l, 0))
    full = lambda a: pl.BlockSpec(a.shape, lambda b, l: (0,) * a.ndim)
    as3 = lambda a: a.reshape(batch, seq, a.shape[-1])
    out = pl.pallas_call(
        _retention_kernel,
        grid=(batch // bpb, nb),
        in_specs=[tok, tok, pl.BlockSpec((bpb, blk, BRANCH_W), lambda b, l: (b, l, P_RET_V)),
                  full(mask), full(q_dec), full(k_dec), full(c_dec)],
        out_specs=tok,
        out_shape=jax.ShapeDtypeStruct((batch, seq, BRANCH_W), BF16),
        scratch_shapes=[pltpu.VMEM((bpb, HEADS, HEAD_W, HEAD_W), F32)],
        compiler_params=_params("parallel", "arbitrary"),
        name="retention",
    )(as3(rq), as3(rk), as3(proj), mask, q_dec, k_dec, c_dec)
    return out.reshape(n, BRANCH_W)


def _diff_attn_kernel(bound_ref, q_ref, k_ref, v_ref, lq1_ref, lk1_ref, lq2_ref, lk2_ref, hg_ref, o_ref,
                      acc_ref, *stat_refs, tq, tk, lambda_init, fixed_stabiliser):
    i = pl.program_id(2)
    n_sub = tq // tk
    q = q_ref[...]
    lane = lax.broadcasted_iota(jnp.int32, (tq, HEAD_W), 1)
    qs = (jnp.where(lane < DIFF_DH, q, jnp.zeros_like(q)), jnp.where(lane < DIFF_DH, jnp.zeros_like(q), q))
    acc_ref[...] = jnp.zeros_like(acc_ref)
    if fixed_stabiliser:
        shift = bound_ref[0, 0]
    else:
        m_ref, = stat_refs
        m_ref[...] = jnp.full_like(m_ref, NEG_INF)
    qc = lax.broadcasted_iota(jnp.int32, (tq, tk), 0) // CHUNK
    kc = lax.broadcasted_iota(jnp.int32, (tq, tk), 1) // CHUNK

    def tile(j, diag):
        rows = pl.ds(pl.multiple_of(j * tk, tk), tk)
        kt = k_ref[rows, :]
        vt = v_ref[rows, :]
        v_ones = jnp.concatenate([vt, jnp.ones_like(vt)], axis=1)
        r0 = 0 if diag is None else diag * tk
        visible = None if diag is None else ((kc + diag * (tk // CHUNK)) <= qc)[r0:]
        for mp in range(2):
            s = lax.dot_general(qs[mp][r0:], kt, NT_DIMS, preferred_element_type=F32)
            if fixed_stabiliser:
                p = jnp.exp2(s - shift)
                if visible is not None:
                    p = jnp.where(visible, p, 0.0)
                acc_ref[mp, r0:] += jnp.dot(p.astype(BF16), v_ones, preferred_element_type=F32)
            else:
                if visible is not None:
                    s = jnp.where(visible, s, NEG_INF)
                m_prev = m_ref[mp, r0:]
                m_new = jnp.maximum(m_prev, jnp.max(s, axis=-1, keepdims=True))
                p = jnp.exp2(s - m_new)
                acc_ref[mp, r0:] = (jnp.exp2(m_prev - m_new) * acc_ref[mp, r0:]
                                    + jnp.dot(p.astype(BF16), v_ones, preferred_element_type=F32))
                m_ref[mp, r0:] = m_new

    def body(jj, carry):
        for u in range(n_sub):
            tile(jj * n_sub + u, None)
        return carry

    lax.fori_loop(0, i, body, 0)
    for d in range(n_sub):
        tile(i * n_sub + d, d)

    lam = (jnp.exp(jnp.sum(lq1_ref[...] * lk1_ref[...], axis=-1, keepdims=True))
           - jnp.exp(jnp.sum(lq2_ref[...] * lk2_ref[...], axis=-1, keepdims=True)) + lambda_init)
    o = (acc_ref[0, :, :HEAD_W] / acc_ref[0, :, HEAD_W:]
         - lam * (acc_ref[1, :, :HEAD_W] / acc_ref[1, :, HEAD_W:]))
    o_ref[...] = (_rms(o) * hg_ref[...] * (1.0 - lambda_init)).astype(o_ref.dtype)


def _diff_attn(dq, dk, proj, qg, kg, lq1, lk1, lq2, lk2, hg, batch, seq, tq, tk, lambda_init):
    n = dq.shape[0]
    nq = seq // tq
    bound = (1.01 * DIFF_DH * DIFF_Q_SCALE * jnp.max(jnp.abs(qg)) * jnp.max(jnp.abs(kg))).reshape(1, 1)
    vec64 = pl.BlockSpec((1, DIFF_DH), lambda b, h, i: (0, 0))

    def call(fixed_stabiliser):
        stat_scratch = [] if fixed_stabiliser else [pltpu.VMEM((2, tq, 1), F32)]
        return pl.pallas_call(
            functools.partial(_diff_attn_kernel, tq=tq, tk=tk, lambda_init=lambda_init,
                              fixed_stabiliser=fixed_stabiliser),
            grid=(batch, HEADS, nq),
            in_specs=[pl.BlockSpec(memory_space=pltpu.SMEM),
                      pl.BlockSpec((tq, HEAD_W), lambda b, h, i: (b * nq + i, h)),
                      pl.BlockSpec((seq, HEAD_W), lambda b, h, i: (b, h)),
                      pl.BlockSpec((seq, HEAD_W), lambda b, h, i: (b, P_DIF_V * HEADS + h)),
                      vec64, vec64, vec64, vec64,
                      pl.BlockSpec((1, HEAD_W), lambda b, h, i: (0, 0))],
            out_specs=pl.BlockSpec((tq, HEAD_W), lambda b, h, i: (b * nq + i, h)),
            out_shape=jax.ShapeDtypeStruct((n, BRANCH_W), BF16),
            scratch_shapes=[pltpu.VMEM((2, tq, 2 * HEAD_W), F32)] + stat_scratch,
            compiler_params=_params("parallel", "parallel", "arbitrary"),
            name="diff_attn_fixed" if fixed_stabiliser else "diff_attn_online",
        )(bound, dq, dk, proj, lq1, lk1, lq2, lk2, hg)

    return lax.cond(bound[0, 0] * 2.0 <= MAX_SCORE_RANGE_LOG2, lambda: call(True), lambda: call(False))


def _mem_kv_kernel(mem_ref, g_ref, w_ref, kg_ref, k_ref, v_ref):
    h = (_rms(mem_ref[...]) * g_ref[...]).astype(BF16)
    kv = jnp.dot(h, w_ref[...], preferred_element_type=F32)
    for hd in range(HEADS):
        sl = slice(hd * HEAD_W, (hd + 1) * HEAD_W)
        k_ref[:, sl] = (_rms(kv[:, sl]) * kg_ref[...]).astype(BF16)
    v_ref[...] = kv[:, BRANCH_W:].astype(BF16)


def _mem_kv(mem2, g, w_bf, kg, mem_len):
    n, d = mem2.shape
    out = pl.BlockSpec((mem_len, BRANCH_W), lambda i: (i, 0))
    return pl.pallas_call(
        _mem_kv_kernel,
        grid=(n // mem_len,),
        in_specs=[pl.BlockSpec((mem_len, d), lambda i: (i, 0)),
                  pl.BlockSpec((1, d), lambda i: (0, 0)),
                  pl.BlockSpec((d, 2 * BRANCH_W), lambda i: (0, 0)),
                  pl.BlockSpec((1, HEAD_W), lambda i: (0, 0))],
        out_specs=[out, out],
        out_shape=[jax.ShapeDtypeStruct((n, BRANCH_W), BF16)] * 2,
        compiler_params=_params("parallel"),
        name="mem_kv",
    )(mem2, g, w_bf, kg)


def _merge_kernel(x_ref, y_ref, sg_ref, ret_ref, rg_ref, dif_ref, dg_ref, mq_ref, mg_ref, mk_ref, mv_ref,
                  mrg0_ref, mrg1_ref, mrg2_ref, mrg3_ref, wglu_ref, bglu_ref, mqg_ref, wbr_ref, bm_ref, wout_ref,
                  o_ref):
    z = jax.nn.gelu(y_ref[...])
    glu = jnp.dot(wglu_ref[...], z.astype(BF16), preferred_element_type=F32) + bglu_ref[...]
    mrg_refs = (mrg0_ref, mrg1_ref, mrg2_ref, mrg3_ref)

    def gated(nb, projected):
        gate = 1.0 + jnp.tanh(mrg_refs[nb][...] + bm_ref[nb:nb + 1, :].astype(BF16))
        return gate * projected.astype(BF16)

    a_t = (z * _sigmoid(glu)).astype(BF16) * _silu_of_half(sg_ref[...])
    merged = gated(0, lax.dot_general(a_t, wbr_ref[0], TN_DIMS, preferred_element_type=F32))
    b_out = ret_ref[...] * _silu_of_half(rg_ref[...])
    merged += gated(1, jnp.dot(b_out, wbr_ref[1], preferred_element_type=F32))
    c_out = dif_ref[...] * _silu_of_half(dg_ref[...])
    merged += gated(2, jnp.dot(c_out, wbr_ref[2], preferred_element_type=F32))
    heads = []
    for hd in range(HEADS):
        sl = slice(hd * HEAD_W, (hd + 1) * HEAD_W)
        qh = (_rms(mq_ref[:, sl].astype(F32)) * mqg_ref[...] * (HEAD_W ** -0.5)).astype(BF16)
        s = lax.dot_general(qh, mk_ref[:, sl], NT_DIMS, preferred_element_type=F32)
        p = jnp.exp(s - jnp.max(s, axis=-1, keepdims=True))
        o = jnp.dot(p.astype(BF16), mv_ref[:, sl], preferred_element_type=F32)
        heads.append(o / jnp.sum(p, axis=-1, keepdims=True))
    m_out = jnp.concatenate(heads, axis=-1).astype(BF16) * _silu_of_half(mg_ref[...])
    merged += gated(3, jnp.dot(m_out, wbr_ref[3], preferred_element_type=F32))
    o_ref[...] = x_ref[...] + jnp.dot(merged, wout_ref[...], preferred_element_type=F32)


def _merge(x2, y_t, sg_t, proj, ret_o, dif_o, mk, mv, wglu_t, bglu, mqg, wbr, bm, wout, seq, mem_len, tm):
    n, d = x2.shape

    def piece(p):
        return pl.BlockSpec((tm, BRANCH_W), lambda i: (i, p))

    def gate_piece(nb):
        return pl.BlockSpec((tm, D_MODEL), lambda i: (i, P_MERGE * BRANCH_W // D_MODEL + nb))

    tok = pl.BlockSpec((tm, BRANCH_W), lambda i: (i, 0))
    mem = pl.BlockSpec((mem_len, BRANCH_W), lambda i: ((i * tm) // seq, 0))
    full = lambda a: pl.BlockSpec(a.shape, lambda i: (0,) * a.ndim)
    return pl.pallas_call(
        _merge_kernel,
        grid=(n // tm,),
        in_specs=[pl.BlockSpec((tm, d), lambda i: (i, 0)),
                  pl.BlockSpec((BRANCH_W, tm), lambda i: (0, i)),
                  pl.BlockSpec((BRANCH_W, tm), lambda i: (0, i)),
                  tok, piece(P_RET_G), tok, piece(P_DIF_G), piece(P_MEM_Q), piece(P_MEM_G), mem, mem,
                  gate_piece(0), gate_piece(1), gate_piece(2), gate_piece(3),
                  full(wglu_t), full(bglu), full(mqg), full(wbr), full(bm), full(wout)],
        out_specs=pl.BlockSpec((tm, d), lambda i: (i, 0)),
        out_shape=jax.ShapeDtypeStruct((n, d), F32),
        compiler_params=_params("parallel"),
        name="merge",
    )(x2, y_t, sg_t, ret_o, proj, dif_o, proj, proj, proj, mk, mv, proj, proj, proj, proj,
      wglu_t, bglu, mqg, wbr, bm, wout)


def _tile(n, pref):
    return pref if n % pref == 0 else n


def kernel(x, mem, norm_g, w_in, ssm_lambda_re, ssm_lambda_im, ssm_log_dt, ssm_b_re, ssm_b_im, ssm_c_re,
           ssm_c_im, ssm_d, ssm_w_glu, ssm_b_glu, diff_q_norm_g, diff_k_norm_g, diff_lambda_q1,
           diff_lambda_k1, diff_lambda_q2, diff_lambda_k2, diff_head_norm_g, mem_norm_g, w_mem_kv,
           mem_q_norm_g, mem_k_norm_g, w_branch, b_merge, w_out):
    batch, seq, d = x.shape
    mem_len = mem.shape[1]
    depth = w_in.shape[0]
    n = batch * seq
    assert d == D_MODEL and seq % S5_CHUNK == 0

    diff_inv = ROPE_THETA ** (-jnp.arange(0, DIFF_DH, 2, dtype=F32) / DIFF_DH)
    ret_inv = 1.0 / (ROPE_THETA ** jnp.linspace(0.0, 1.0, HEAD_W // 2, dtype=F32))
    inv = jnp.stack([jnp.tile(diff_inv, LANES // diff_inv.shape[0]),
                     jnp.tile(ret_inv, LANES // ret_inv.shape[0])])[:, None, :]
    lane = np.arange(LANES)
    sgn = np.stack([np.where(lane % DIFF_DH < DIFF_DH // 2, -1.0, 1.0),
                    np.where(lane < HEAD_W // 2, -1.0, 1.0)])[:, None, :]
    cos_t, sin_t = _rope_tables(seq, inv, jnp.asarray(sgn, F32), _tile(seq, 1024))

    piece = np.arange(w_in.shape[2]) // BRANCH_W
    halved = (piece >= N_IN_PIECES) | np.isin(piece, IN_GATE_PIECES)
    in_col_scale = jnp.asarray(np.where(halved, GATE_FOLD, 1.0), F32)

    x2 = x.reshape(n, d)
    mem2 = mem.reshape(batch * mem_len, d)
    for l in range(depth):
        lambda_init = 0.8 - 0.6 * math.exp(-0.3 * l)
        g = norm_g[l][None, :]
        w = w_in[l] * in_col_scale[None, :]
        proj = _in_proj(x2, g, w[:, 2 * BRANCH_W:].astype(BF16), _tile(n, 2048), 1536)
        u3, sg_t = _in_proj_t(x2, g, w[:, :2 * BRANCH_W].T.astype(BF16), _tile(n, 2048))

        kern_op, p_op, q_op, dec = _s5_prep(ssm_lambda_re[l], ssm_lambda_im[l], ssm_log_dt[l], ssm_b_re[l],
                                         ssm_b_im[l], ssm_c_re[l], ssm_c_im[l])
        y_t = _s5_conv(u3, ssm_d[l], kern_op, p_op, q_op, dec, batch, seq)

        rq, rk, dq, dk = _qk_prep(proj, cos_t, sin_t, diff_q_norm_g[l], diff_k_norm_g[l], seq, _tile(seq, 1024))
        ret_o = _retention(rq, rk, proj, batch, seq, _tile(seq, 256), math.gcd(batch, 4))
        dif_o = _diff_attn(dq, dk, proj, diff_q_norm_g[l], diff_k_norm_g[l], diff_lambda_q1[l][None],
                           diff_lambda_k1[l][None], diff_lambda_q2[l][None], diff_lambda_k2[l][None],
                           diff_head_norm_g[l][None], batch, seq, _tile(seq, 2048), _tile(seq, 512), lambda_init)
        mk, mv = _mem_kv(mem2, mem_norm_g[l][None, :], w_mem_kv[l].astype(BF16), mem_k_norm_g[l][None], mem_len)
        x2 = _merge(x2, y_t, sg_t, proj, ret_o, dif_o, mk, mv, ssm_w_glu[l].T.astype(BF16),
                    ssm_b_glu[l][:, None], mem_q_norm_g[l][None], w_branch[l].astype(BF16), GATE_FOLD * b_merge[l],
                    (GATE_FOLD * w_out[l]).astype(BF16), seq, mem_len, _tile(seq, 512))
    return x2.reshape(batch, seq, d)
```

```python
import functools
import math

import numpy as np
import jax
import jax.numpy as jnp
from jax import lax
from jax.experimental import pallas as pl
from jax.experimental.pallas import tpu as pltpu

F32 = jnp.float32
BF16 = jnp.bfloat16

D_MODEL = 1024
BRANCH_W = D_MODEL // 2
N_BRANCH = 4
N_IN_PIECES = 12
CHUNK = 64
SSM_GROUP = 16
SSM_GROUPS = BRANCH_W // SSM_GROUP
SSM_STATE = 64
HEADS = 4
HEAD_W = BRANCH_W // HEADS
DIFF_DH = HEAD_W // 2
ROPE_THETA = 10000.0
EPS = 1e-6
NEG_INF = -1e30
GATE_FOLD = 0.5
IN_GATE_PIECES = (1, 5, 9, 11)
LOG2_E = math.log2(math.e)
DIFF_Q_SCALE = DIFF_DH ** -0.5 * LOG2_E
MAX_SCORE_RANGE_LOG2 = 100.0

LANES = 128
S5_CHUNK = LANES
S5_FLAT = SSM_GROUP * S5_CHUNK
S5_ROWS_PER_DOT = 4
P_RET_Q, P_RET_K, P_RET_V, P_RET_G, P_DIF_Q, P_DIF_K, P_DIF_V, P_DIF_G, P_MEM_Q, P_MEM_G, P_MERGE = range(11)

V7X_VMEM_BYTES = 64 * 1024 * 1024
VMEM_LIMIT = V7X_VMEM_BYTES * 7 // 8

NT_DIMS = (((1,), (1,)), ((), ()))
TN_DIMS = (((0,), (0,)), ((), ()))


def _params(*sem):
    return pltpu.CompilerParams(dimension_semantics=sem, vmem_limit_bytes=VMEM_LIMIT)


def _rms(x, eps=EPS):
    return x * lax.rsqrt(jnp.mean(x * x, axis=-1, keepdims=True) + eps)


def _sigmoid(x):
    return 0.5 * jnp.tanh(0.5 * x) + 0.5


def _silu_of_half(xh):
    return xh * (1.0 + jnp.tanh(xh))


def _in_proj_kernel(x_ref, g_ref, w_ref, o_ref, h_ref):
    @pl.when(pl.program_id(1) == 0)
    def _():
        h_ref[...] = (_rms(x_ref[...]) * g_ref[...]).astype(BF16)

    o_ref[...] = jnp.dot(h_ref[...], w_ref[...], preferred_element_type=F32).astype(o_ref.dtype)


def _in_proj(x2, g, w_bf, tm, tn):
    n, d = x2.shape
    cols = w_bf.shape[1]
    return pl.pallas_call(
        _in_proj_kernel,
        grid=(n // tm, cols // tn),
        in_specs=[pl.BlockSpec((tm, d), lambda i, j: (i, 0)),
                  pl.BlockSpec((1, d), lambda i, j: (0, 0)),
                  pl.BlockSpec((d, tn), lambda i, j: (0, j))],
        out_specs=pl.BlockSpec((tm, tn), lambda i, j: (i, j)),
        out_shape=jax.ShapeDtypeStruct((n, cols), BF16),
        scratch_shapes=[pltpu.VMEM((tm, d), BF16)],
        compiler_params=_params("parallel", "arbitrary"),
        name="in_proj",
    )(x2, g, w_bf)


def _in_proj_t_kernel(x_ref, g_ref, wt_ref, u_ref, gate_ref):
    h = (_rms(x_ref[...]) * g_ref[...]).astype(BF16)
    res = lax.dot_general(wt_ref[...], h, NT_DIMS, preferred_element_type=F32)
    tm = res.shape[1]
    u_ref[...] = res[:BRANCH_W].reshape(BRANCH_W, tm // S5_CHUNK, S5_CHUNK).astype(u_ref.dtype)
    gate_ref[...] = res[BRANCH_W:].astype(gate_ref.dtype)


def _in_proj_t(x2, g, wt_bf, tm):
    n, d = x2.shape
    rows = wt_bf.shape[0]
    return pl.pallas_call(
        _in_proj_t_kernel,
        grid=(n // tm,),
        in_specs=[pl.BlockSpec((tm, d), lambda i: (i, 0)),
                  pl.BlockSpec((1, d), lambda i: (0, 0)),
                  pl.BlockSpec((rows, d), lambda i: (0, 0))],
        out_specs=[pl.BlockSpec((BRANCH_W, tm // S5_CHUNK, S5_CHUNK), lambda i: (0, i, 0)),
                   pl.BlockSpec((BRANCH_W, tm), lambda i: (0, i))],
        out_shape=[jax.ShapeDtypeStruct((BRANCH_W, n // S5_CHUNK, S5_CHUNK), BF16),
                   jax.ShapeDtypeStruct((BRANCH_W, n), BF16)],
        compiler_params=_params("parallel"),
        name="in_proj_t",
    )(x2, g, wt_bf)


def _rope_table_kernel(inv_ref, sgn_ref, cos_ref, sin_ref):
    ts = cos_ref.shape[0]
    pos = (lax.broadcasted_iota(jnp.int32, (ts, LANES), 0) + pl.program_id(1) * ts).astype(F32)
    ang = pos * inv_ref[...]
    cos_ref[...] = jnp.cos(ang)
    sin_ref[...] = jnp.sin(ang) * sgn_ref[...]


def _rope_tables(seq, inv, sgn, ts):
    spec_in = pl.BlockSpec((None, 1, LANES), lambda t, i: (t, 0, 0))
    spec_out = pl.BlockSpec((None, ts, LANES), lambda t, i: (t, i, 0))
    return pl.pallas_call(
        _rope_table_kernel,
        grid=(2, seq // ts),
        in_specs=[spec_in, spec_in],
        out_specs=[spec_out, spec_out],
        out_shape=[jax.ShapeDtypeStruct((2, seq, LANES), F32)] * 2,
        compiler_params=_params("parallel", "parallel"),
        name="rope_tables",
    )(inv, sgn)


def _qk_prep_kernel(rq_ref, rk_ref, dq_ref, dk_ref, cos_ref, sin_ref, qg_ref, kg_ref, perm_ref,
                    orq_ref, ork_ref, odq_ref, odk_ref):
    two = lambda t: jnp.concatenate([t, t], axis=1)
    cos_d, sin_d, cos_r, sin_r = two(cos_ref[0]), two(sin_ref[0]), two(cos_ref[1]), two(sin_ref[1])
    k_scale = HEAD_W ** -0.5
    cos_rk, sin_rk = cos_r * k_scale, sin_r * k_scale
    q_cos, q_sin = cos_d * (two(qg_ref[0]) * DIFF_Q_SCALE), sin_d * (two(qg_ref[1]) * DIFF_Q_SCALE)
    k_cos, k_sin = cos_d * two(kg_ref[0]), sin_d * two(kg_ref[1])

    def shuffled(x, which):
        return jnp.dot(x, perm_ref[which], preferred_element_type=F32)

    def rope_r(x, c, s):
        return (x.astype(F32) * c + shuffled(x, 0) * s).astype(BF16)

    def norm_rope_d(x, c, s):
        xf = x.astype(F32)
        ms = shuffled((xf * xf).astype(BF16), 2)
        return (lax.rsqrt(ms + EPS) * (xf * c + shuffled(x, 1) * s)).astype(BF16)

    for hp in range(HEADS // 2):
        sl = slice(hp * 2 * HEAD_W, (hp + 1) * 2 * HEAD_W)
        orq_ref[:, sl] = rope_r(rq_ref[:, sl], cos_r, sin_r)
        ork_ref[:, sl] = rope_r(rk_ref[:, sl], cos_rk, sin_rk)
        odq_ref[:, sl] = norm_rope_d(dq_ref[:, sl], q_cos, q_sin)
        odk_ref[:, sl] = norm_rope_d(dk_ref[:, sl], k_cos, k_sin)


def _qk_prep_consts():
    lane = np.arange(LANES)
    half_rot = (lane + HEAD_W // 2) % LANES
    partner = np.where(lane % DIFF_DH < DIFF_DH // 2, lane + DIFF_DH // 2, lane - DIFF_DH // 2)
    perms = np.zeros((3, LANES, LANES), np.float32)
    perms[0, half_rot, lane] = 1.0
    perms[1, partner, lane] = 1.0
    perms[2] = ((lane[:, None] // DIFF_DH) == (lane[None, :] // DIFF_DH)) / DIFF_DH
    two_heads = np.zeros((3, 2 * LANES, 2 * LANES), np.float32)
    two_heads[:, :LANES, :LANES] = perms
    two_heads[:, LANES:, LANES:] = perms
    return jnp.asarray(two_heads, BF16), partner


def _qk_prep(proj, cos_t, sin_t, qg, kg, seq, ts):
    n = proj.shape[0]
    nsb = seq // ts
    perms, partner = _qk_prep_consts()
    with_partner = lambda g: jnp.stack([jnp.tile(g, 2), jnp.tile(g, 2)[partner]])[:, None, :]

    def piece(p):
        return pl.BlockSpec((ts, BRANCH_W), lambda i: (i, p))

    tab = pl.BlockSpec((2, ts, LANES), lambda i: (0, i % nsb, 0))
    vec = pl.BlockSpec((2, 1, LANES), lambda i: (0, 0, 0))
    out = pl.BlockSpec((ts, BRANCH_W), lambda i: (i, 0))
    qg, kg = with_partner(qg), with_partner(kg)
    return pl.pallas_call(
        _qk_prep_kernel,
        grid=(n // ts,),
        in_specs=[piece(P_RET_Q), piece(P_RET_K), piece(P_DIF_Q), piece(P_DIF_K), tab, tab, vec, vec,
                  pl.BlockSpec(perms.shape, lambda i: (0, 0, 0))],
        out_specs=[out] * 4,
        out_shape=[jax.ShapeDtypeStruct((n, BRANCH_W), BF16)] * 4,
        compiler_params=_params("parallel"),
        name="qk_prep",
    )(proj, proj, proj, proj, cos_t, sin_t, qg, kg, perms)


def _s5_prep_kernel(lam_c_ref, lam_r_ref, b_ref, bt_ref, c_ref, ct_ref,
                    kern_ref, p_ref, q_ref, dec_ref):
    T = S5_CHUNK

    def disc(lam_re, lam_im, log_dt):
        lr = jnp.minimum(lam_re, -1e-4)
        li = lam_im
        dt = jnp.exp(log_dt)
        mag = jnp.exp(lr * dt)
        ab_re = mag * jnp.cos(li * dt)
        ab_im = mag * jnp.sin(li * dt)
        nr = ab_re - 1.0
        ni = ab_im
        den = lr * lr + li * li
        f_re = (nr * lr + ni * li) / den
        f_im = (ni * lr - nr * li) / den
        return lr * dt, li * dt, f_re, f_im

    def powers(ldt_re, ldt_im, k):
        mag = jnp.exp(k * ldt_re)
        return mag * jnp.cos(k * ldt_im), mag * jnp.sin(k * ldt_im)

    ldr, ldi, f_re, f_im = disc(lam_c_ref[0], lam_c_ref[1], lam_c_ref[2])
    b_re, b_im = b_ref[0], b_ref[1]
    bb_re = f_re * b_re - f_im * b_im
    bb_im = f_re * b_im + f_im * b_re
    k_row = lax.broadcasted_iota(jnp.int32, (1, T), 1).astype(F32)
    pw_re, pw_im = powers(ldr, ldi, k_row)
    pw1_re, pw1_im = powers(ldr, ldi, k_row + 1.0)
    c_re, c_im = c_ref[0], c_ref[1]
    ct_re, ct_im = ct_ref[0], ct_ref[1]

    for c in range(SSM_GROUP):
        w_re = pw_re * bb_re[:, c:c + 1] - pw_im * bb_im[:, c:c + 1]
        w_im = pw_re * bb_im[:, c:c + 1] + pw_im * bb_re[:, c:c + 1]
        kern = (jnp.dot(c_re, w_re, preferred_element_type=F32, precision=lax.Precision.HIGHEST)
                - jnp.dot(c_im, w_im, preferred_element_type=F32, precision=lax.Precision.HIGHEST))
        for co in range(SSM_GROUP):
            kern_ref[c, :, co * T:(co + 1) * T] = kern[co:co + 1, :]
        q_ref[0, :, c * T:(c + 1) * T] = (ct_re[:, c:c + 1] * pw1_re - ct_im[:, c:c + 1] * pw1_im).astype(q_ref.dtype)
        q_ref[1, :, c * T:(c + 1) * T] = (-ct_re[:, c:c + 1] * pw1_im - ct_im[:, c:c + 1] * pw1_re).astype(q_ref.dtype)

    ldr_r, ldi_r, fr_r, fi_r = disc(lam_r_ref[0], lam_r_ref[1], lam_r_ref[2])
    bt_re, bt_im = bt_ref[0], bt_ref[1]
    bbt_re = fr_r * bt_re - fi_r * bt_im
    bbt_im = fr_r * bt_im + fi_r * bt_re
    k_col = (T - 1.0) - lax.broadcasted_iota(jnp.int32, (T, 1), 0).astype(F32)
    pr_re, pr_im = powers(ldr_r, ldi_r, k_col)
    for c in range(SSM_GROUP):
        p_ref[0, c * T:(c + 1) * T, :] = (pr_re * bbt_re[c:c + 1, :] - pr_im * bbt_im[c:c + 1, :]).astype(p_ref.dtype)
        p_ref[1, c * T:(c + 1) * T, :] = (pr_re * bbt_im[c:c + 1, :] + pr_im * bbt_re[c:c + 1, :]).astype(p_ref.dtype)
    d_re, d_im = powers(ldr_r, ldi_r, float(T))
    dec_ref[0] = d_re
    dec_ref[1] = d_im


def _s5_prep(lam_re, lam_im, log_dt, b_re, b_im, c_re, c_im):
    g, p = lam_re.shape
    ldt = jnp.broadcast_to(log_dt[:, None], (g, p))
    lam = jnp.stack([lam_re, lam_im, ldt], axis=1)
    lam_c = lam[..., None]
    lam_r = lam[:, :, None, :]
    b = jnp.stack([b_re, b_im], axis=1)
    bt = jnp.swapaxes(b, 2, 3)
    c = jnp.stack([c_re, c_im], axis=1)
    ct = jnp.swapaxes(c, 2, 3)

    def spec(shape):
        return pl.BlockSpec((None,) + shape, lambda i: (i,) + (0,) * len(shape))

    return pl.pallas_call(
        _s5_prep_kernel,
        grid=(g,),
        in_specs=[spec((3, p, 1)), spec((3, 1, p)), spec((2, p, SSM_GROUP)), spec((2, SSM_GROUP, p)),
                  spec((2, SSM_GROUP, p)), spec((2, p, SSM_GROUP))],
        out_specs=[spec((SSM_GROUP, 1, S5_FLAT)), spec((2, S5_FLAT, p)), spec((2, p, S5_FLAT)), spec((2, 1, p))],
        out_shape=[jax.ShapeDtypeStruct((g, SSM_GROUP, 1, S5_FLAT), F32),
                   jax.ShapeDtypeStruct((g, 2, S5_FLAT, p), BF16),
                   jax.ShapeDtypeStruct((g, 2, p, S5_FLAT), BF16),
                   jax.ShapeDtypeStruct((g, 2, 1, p), F32)],
        compiler_params=_params("parallel"),
        name="s5_prep",
    )(lam_c, lam_r, b, bt, c, ct)


def _s5_conv_kernel(d_ref, u_ref, kern_ref, p_ref, q_ref, dec_ref, y_ref, v_ref, s_ref, *, batch, n_chunks):
    g = pl.program_id(0)
    T = S5_CHUNK
    u_flat = jnp.concatenate([u_ref[c] for c in range(SSM_GROUP)], axis=-1)
    jj = lax.broadcasted_iota(jnp.int32, (T, S5_FLAT), 0)
    causal = (lax.broadcasted_iota(jnp.int32, (T, S5_FLAT), 1) % T) >= jj

    def toeplitz_rows(c):
        row = jnp.broadcast_to(kern_ref[c], (T, S5_FLAT))
        toep = pltpu.roll(row, 0, 1, stride=1, stride_axis=0)
        return jnp.where(causal, toep, 0.0).astype(BF16)

    rows = batch * n_chunks
    for ri in range(2):
        v_ref[ri] = jnp.dot(u_flat, p_ref[ri], preferred_element_type=F32).reshape(batch, n_chunks, SSM_STATE)
    dr, di = dec_ref[0], dec_ref[1]

    def body(ch, carry):
        sr, si = carry
        s_ref[0, :, pl.ds(ch, 1), :] = sr
        s_ref[1, :, pl.ds(ch, 1), :] = si
        vr = v_ref[0, :, pl.ds(ch, 1), :]
        vi = v_ref[1, :, pl.ds(ch, 1), :]
        return dr * sr - di * si + vr, dr * si + di * sr + vi

    zero = jnp.zeros((batch, 1, SSM_STATE), F32)
    lax.fori_loop(0, n_chunks, body, (zero, zero))
    y = None
    for c in range(0, SSM_GROUP, S5_ROWS_PER_DOT):
        lhs = jnp.concatenate([u_ref[c + k] for k in range(S5_ROWS_PER_DOT)], axis=-1)
        rhs = jnp.concatenate([toeplitz_rows(c + k) for k in range(S5_ROWS_PER_DOT)], axis=0)
        part = jnp.dot(lhs, rhs, preferred_element_type=F32)
        y = part if y is None else y + part
    for ri in range(2):
        s_start = s_ref[ri].reshape(rows, SSM_STATE).astype(BF16)
        y += jnp.dot(s_start, q_ref[ri], preferred_element_type=F32)
    slabs = [y[:, c * T:(c + 1) * T] + d_ref[g * SSM_GROUP + c] * u_ref[c].astype(F32) for c in range(SSM_GROUP)]
    y_ref[...] = jnp.stack(slabs, axis=0).reshape(SSM_GROUP, rows * T)


def _s5_conv(u3, d_skip, kern, p, q, dec, batch, seq):
    r = u3.shape[1]
    n = r * S5_CHUNK
    g = SSM_GROUPS

    def spec(shape):
        return pl.BlockSpec((None,) + shape, lambda i: (i,) + (0,) * len(shape))

    return pl.pallas_call(
        functools.partial(_s5_conv_kernel, batch=batch, n_chunks=seq // S5_CHUNK),
        grid=(g,),
        in_specs=[pl.BlockSpec(memory_space=pltpu.SMEM),
                  pl.BlockSpec((SSM_GROUP, r, S5_CHUNK), lambda i: (i, 0, 0)),
                  spec((SSM_GROUP, 1, S5_FLAT)), spec((2, S5_FLAT, SSM_STATE)), spec((2, SSM_STATE, S5_FLAT)),
                  spec((2, 1, SSM_STATE))],
        out_specs=pl.BlockSpec((SSM_GROUP, n), lambda i: (i, 0)),
        out_shape=jax.ShapeDtypeStruct((BRANCH_W, n), F32),
        scratch_shapes=[pltpu.VMEM((2, batch, seq // S5_CHUNK, SSM_STATE), F32)] * 2,
        compiler_params=_params("parallel"),
        name="s5_conv",
    )(d_skip, u3, kern, p, q, dec)


def _retention_kernel(q_ref, k_ref, v_ref, mask_ref, qd_ref, kd_ref, cd_ref, o_ref, st_ref):
    @pl.when(pl.program_id(1) == 0)
    def _():
        st_ref[...] = jnp.zeros_like(st_ref)

    for bi in range(q_ref.shape[0]):
        for h in range(HEADS):
            sl = slice(h * HEAD_W, (h + 1) * HEAD_W)
            q, k, v = q_ref[bi, :, sl], k_ref[bi, :, sl], v_ref[bi, :, sl]
            scores = lax.dot_general(q, k, NT_DIMS, preferred_element_type=F32) * mask_ref[h]
            st = st_ref[bi, h]
            o = (jnp.dot(scores.astype(BF16), v, preferred_element_type=F32)
                 + jnp.dot(q, st.astype(BF16), preferred_element_type=F32) * qd_ref[h])
            kd = (k.astype(F32) * kd_ref[h]).astype(BF16)
            st_ref[bi, h] = st * cd_ref[h] + lax.dot_general(kd, v, TN_DIMS, preferred_element_type=F32)
            o_ref[bi, :, sl] = _rms(o).astype(o_ref.dtype)


def _retention_consts(blk):
    log_gamma = np.log1p(-(2.0 ** (-5.0 - np.arange(HEADS, dtype=np.float64))))
    idx = np.arange(blk, dtype=np.float64)
    diff = idx[:, None] - idx[None, :]
    same = (np.arange(blk)[:, None] // CHUNK) == (np.arange(blk)[None, :] // CHUNK)
    past = (np.arange(blk)[None, :] // CHUNK) < (np.arange(blk)[:, None] // CHUNK)
    expo = np.where(same, np.abs(diff), diff)
    mask = np.where(same | past, np.exp(expo[None] * log_gamma[:, None, None]), 0.0)
    q_dec = np.exp((idx + 1.0)[None, :] * log_gamma[:, None])
    k_dec = np.exp((blk - 1.0 - idx)[None, :] * log_gamma[:, None])
    c_dec = np.exp(blk * log_gamma)
    bc = lambda a: np.broadcast_to(a[:, :, None], (HEADS, blk, HEAD_W))
    return (jnp.asarray(mask, F32), jnp.asarray(bc(q_dec), F32), jnp.asarray(bc(k_dec), F32),
            jnp.asarray(np.broadcast_to(c_dec[:, None, None], (HEADS, 1, HEAD_W)), F32))


def _retention(rq, rk, proj, batch, seq, blk, bpb):
    n = rq.shape[0]
    nb = seq // blk
    mask, q_dec, k_dec, c_dec = _retention_consts(blk)
    tok = pl.BlockSpec((bpb, blk, BRANCH_W), lambda b, l: (b, l, 0))
    full = lambda a: pl.BlockSpec(a.shape, lambda b, l: (0,) * a.ndim)
    as3 = lambda a: a.reshape(batch, seq, a.shape[-1])
    out = pl.pallas_call(
        _retention_kernel,
        grid=(batch // bpb, nb),
        in_specs=[tok, tok, pl.BlockSpec((bpb, blk, BRANCH_W), lambda b, l: (b, l, P_RET_V)),
                  full(mask), full(q_dec), full(k_dec), full(c_dec)],
        out_specs=tok,
        out_shape=jax.ShapeDtypeStruct((batch, seq, BRANCH_W), BF16),
        scratch_shapes=[pltpu.VMEM((bpb, HEADS, HEAD_W, HEAD_W), F32)],
        compiler_params=_params("parallel", "arbitrary"),
        name="retention",
    )(as3(rq), as3(rk), as3(proj), mask, q_dec, k_dec, c_dec)
    return out.reshape(n, BRANCH_W)


def _diff_attn_kernel(bound_ref, q_ref, k_ref, v_ref, lq1_ref, lk1_ref, lq2_ref, lk2_ref, hg_ref, o_ref,
                      acc_ref, *stat_refs, tq, tk, lambda_init, fixed_stabiliser):
    i = pl.program_id(2)
    n_sub = tq // tk
    q = q_ref[...]
    lane = lax.broadcasted_iota(jnp.int32, (tq, HEAD_W), 1)
    qs = (jnp.where(lane < DIFF_DH, q, jnp.zeros_like(q)), jnp.where(lane < DIFF_DH, jnp.zeros_like(q), q))
    acc_ref[...] = jnp.zeros_like(acc_ref)
    if fixed_stabiliser:
        shift = bound_ref[0, 0]
    else:
        m_ref, = stat_refs
        m_ref[...] = jnp.full_like(m_ref, NEG_INF)
    qc = lax.broadcasted_iota(jnp.int32, (tq, tk), 0) // CHUNK
    kc = lax.broadcasted_iota(jnp.int32, (tq, tk), 1) // CHUNK

    def tile(j, diag):
        rows = pl.ds(pl.multiple_of(j * tk, tk), tk)
        kt = k_ref[rows, :]
        vt = v_ref[rows, :]
        v_ones = jnp.concatenate([vt, jnp.ones_like(vt)], axis=1)
        r0 = 0 if diag is None else diag * tk
        visible = None if diag is None else ((kc + diag * (tk // CHUNK)) <= qc)[r0:]
        for mp in range(2):
            s = lax.dot_general(qs[mp][r0:], kt, NT_DIMS, preferred_element_type=F32)
            if fixed_stabiliser:
                p = jnp.exp2(s - shift)
                if visible is not None:
                    p = jnp.where(visible, p, 0.0)
                acc_ref[mp, r0:] += jnp.dot(p.astype(BF16), v_ones, preferred_element_type=F32)
            else:
                if visible is not None:
                    s = jnp.where(visible, s, NEG_INF)
                m_prev = m_ref[mp, r0:]
                m_new = jnp.maximum(m_prev, jnp.max(s, axis=-1, keepdims=True))
                p = jnp.exp2(s - m_new)
                acc_ref[mp, r0:] = (jnp.exp2(m_prev - m_new) * acc_ref[mp, r0:]
                                    + jnp.dot(p.astype(BF16), v_ones, preferred_element_type=F32))
                m_ref[mp, r0:] = m_new

    def body(jj, carry):
        for u in range(n_sub):
            tile(jj * n_sub + u, None)
        return carry

    lax.fori_loop(0, i, body, 0)
    for d in range(n_sub):
        tile(i * n_sub + d, d)

    lam = (jnp.exp(jnp.sum(lq1_ref[...] * lk1_ref[...], axis=-1, keepdims=True))
           - jnp.exp(jnp.sum(lq2_ref[...] * lk2_ref[...], axis=-1, keepdims=True)) + lambda_init)
    o = (acc_ref[0, :, :HEAD_W] / acc_ref[0, :, HEAD_W:]
         - lam * (acc_ref[1, :, :HEAD_W] / acc_ref[1, :, HEAD_W:]))
    o_ref[...] = (_rms(o) * hg_ref[...] * (1.0 - lambda_init)).astype(o_ref.dtype)


def _diff_attn(dq, dk, proj, qg, kg, lq1, lk1, lq2, lk2, hg, batch, seq, tq, tk, lambda_init):
    n = dq.shape[0]
    nq = seq // tq
    bound = (1.01 * DIFF_DH * DIFF_Q_SCALE * jnp.max(jnp.abs(qg)) * jnp.max(jnp.abs(kg))).reshape(1, 1)
    vec64 = pl.BlockSpec((1, DIFF_DH), lambda b, h, i: (0, 0))

    def call(fixed_stabiliser):
        stat_scratch = [] if fixed_stabiliser else [pltpu.VMEM((2, tq, 1), F32)]
        return pl.pallas_call(
            functools.partial(_diff_attn_kernel, tq=tq, tk=tk, lambda_init=lambda_init,
                              fixed_stabiliser=fixed_stabiliser),
            grid=(batch, HEADS, nq),
            in_specs=[pl.BlockSpec(memory_space=pltpu.SMEM),
                      pl.BlockSpec((tq, HEAD_W), lambda b, h, i: (b * nq + i, h)),
                      pl.BlockSpec((seq, HEAD_W), lambda b, h, i: (b, h)),
                      pl.BlockSpec((seq, HEAD_W), lambda b, h, i: (b, P_DIF_V * HEADS + h)),
                      vec64, vec64, vec64, vec64,
                      pl.BlockSpec((1, HEAD_W), lambda b, h, i: (0, 0))],
            out_specs=pl.BlockSpec((tq, HEAD_W), lambda b, h, i: (b * nq + i, h)),
            out_shape=jax.ShapeDtypeStruct((n, BRANCH_W), BF16),
            scratch_shapes=[pltpu.VMEM((2, tq, 2 * HEAD_W), F32)] + stat_scratch,
            compiler_params=_params("parallel", "parallel", "arbitrary"),
            name="diff_attn_fixed" if fixed_stabiliser else "diff_attn_online",
        )(bound, dq, dk, proj, lq1, lk1, lq2, lk2, hg)

    return lax.cond(bound[0, 0] * 2.0 <= MAX_SCORE_RANGE_LOG2, lambda: call(True), lambda: call(False))


def _mem_kv_kernel(mem_ref, g_ref, w_ref, kg_ref, k_ref, v_ref):
    h = (_rms(mem_ref[...]) * g_ref[...]).astype(BF16)
    kv = jnp.dot(h, w_ref[...], preferred_element_type=F32)
    for hd in range(HEADS):
        sl = slice(hd * HEAD_W, (hd + 1) * HEAD_W)
        k_ref[:, sl] = (_rms(kv[:, sl]) * kg_ref[...]).astype(BF16)
    v_ref[...] = kv[:, BRANCH_W:].astype(BF16)


def _mem_kv(mem2, g, w_bf, kg, mem_len):
    n, d = mem2.shape
    out = pl.BlockSpec((mem_len, BRANCH_W), lambda i: (i, 0))
    return pl.pallas_call(
        _mem_kv_kernel,
        grid=(n // mem_len,),
        in_specs=[pl.BlockSpec((mem_len, d), lambda i: (i, 0)),
                  pl.BlockSpec((1, d), lambda i: (0, 0)),
                  pl.BlockSpec((d, 2 * BRANCH_W), lambda i: (0, 0)),
                  pl.BlockSpec((1, HEAD_W), lambda i: (0, 0))],
        out_specs=[out, out],
        out_shape=[jax.ShapeDtypeStruct((n, BRANCH_W), BF16)] * 2,
        compiler_params=_params("parallel"),
        name="mem_kv",
    )(mem2, g, w_bf, kg)


def _merge_kernel(x_ref, y_ref, sg_ref, ret_ref, rg_ref, dif_ref, dg_ref, mq_ref, mg_ref, mk_ref, mv_ref,
                  mrg0_ref, mrg1_ref, mrg2_ref, mrg3_ref, wglu_ref, bglu_ref, mqg_ref, wbr_ref, bm_ref, wout_ref,
                  o_ref):
    z = jax.nn.gelu(y_ref[...])
    glu = jnp.dot(wglu_ref[...], z.astype(BF16), preferred_element_type=F32) + bglu_ref[...]
    mrg_refs = (mrg0_ref, mrg1_ref, mrg2_ref, mrg3_ref)

    def gated(nb, projected):
        gate = 1.0 + jnp.tanh(mrg_refs[nb][...] + bm_ref[nb:nb + 1, :].astype(BF16))
        return gate * projected.astype(BF16)

    a_t = (z * _sigmoid(glu)).astype(BF16) * _silu_of_half(sg_ref[...])
    merged = gated(0, lax.dot_general(a_t, wbr_ref[0], TN_DIMS, preferred_element_type=F32))
    b_out = ret_ref[...] * _silu_of_half(rg_ref[...])
    merged += gated(1, jnp.dot(b_out, wbr_ref[1], preferred_element_type=F32))
    c_out = dif_ref[...] * _silu_of_half(dg_ref[...])
    merged += gated(2, jnp.dot(c_out, wbr_ref[2], preferred_element_type=F32))
    heads = []
    for hd in range(HEADS):
        sl = slice(hd * HEAD_W, (hd + 1) * HEAD_W)
        qh = (_rms(mq_ref[:, sl].astype(F32)) * mqg_ref[...] * (HEAD_W ** -0.5)).astype(BF16)
        s = lax.dot_general(qh, mk_ref[:, sl], NT_DIMS, preferred_element_type=F32)
        p = jnp.exp(s - jnp.max(s, axis=-1, keepdims=True))
        o = jnp.dot(p.astype(BF16), mv_ref[:, sl], preferred_element_type=F32)
        heads.append(o / jnp.sum(p, axis=-1, keepdims=True))
    m_out = jnp.concatenate(heads, axis=-1).astype(BF16) * _silu_of_half(mg_ref[...])
    merged += gated(3, jnp.dot(m_out, wbr_ref[3], preferred_element_type=F32))
    o_ref[...] = x_ref[...] + jnp.dot(merged, wout_ref[...], preferred_element_type=F32)


def _merge(x2, y_t, sg_t, proj, ret_o, dif_o, mk, mv, wglu_t, bglu, mqg, wbr, bm, wout, seq, mem_len, tm):
    n, d = x2.shape

    def piece(p):
        return pl.BlockSpec((tm, BRANCH_W), lambda i: (i, p))

    def gate_piece(nb):
        return pl.BlockSpec((tm, D_MODEL), lambda i: (i, P_MERGE * BRANCH_W // D_MODEL + nb))

    tok = pl.BlockSpec((tm, BRANCH_W), lambda i: (i, 0))
    mem = pl.BlockSpec((mem_len, BRANCH_W), lambda i: ((i * tm) // seq, 0))
    full = lambda a: pl.BlockSpec(a.shape, lambda i: (0,) * a.ndim)
    return pl.pallas_call(
        _merge_kernel,
        grid=(n // tm,),
        in_specs=[pl.BlockSpec((tm, d), lambda i: (i, 0)),
                  pl.BlockSpec((BRANCH_W, tm), lambda i: (0, i)),
                  pl.BlockSpec((BRANCH_W, tm), lambda i: (0, i)),
                  tok, piece(P_RET_G), tok, piece(P_DIF_G), piece(P_MEM_Q), piece(P_MEM_G), mem, mem,
                  gate_piece(0), gate_piece(1), gate_piece(2), gate_piece(3),
                  full(wglu_t), full(bglu), full(mqg), full(wbr), full(bm), full(wout)],
        out_specs=pl.BlockSpec((tm, d), lambda i: (i, 0)),
        out_shape=jax.ShapeDtypeStruct((n, d), F32),
        compiler_params=_params("parallel"),
        name="merge",
    )(x2, y_t, sg_t, ret_o, proj, dif_o, proj, proj, proj, mk, mv, proj, proj, proj, proj,
      wglu_t, bglu, mqg, wbr, bm, wout)


def _tile(n, pref):
    return pref if n % pref == 0 else n


def kernel(x, mem, norm_g, w_in, ssm_lambda_re, ssm_lambda_im, ssm_log_dt, ssm_b_re, ssm_b_im, ssm_c_re,
           ssm_c_im, ssm_d, ssm_w_glu, ssm_b_glu, diff_q_norm_g, diff_k_norm_g, diff_lambda_q1,
           diff_lambda_k1, diff_lambda_q2, diff_lambda_k2, diff_head_norm_g, mem_norm_g, w_mem_kv,
           mem_q_norm_g, mem_k_norm_g, w_branch, b_merge, w_out):
    batch, seq, d = x.shape
    mem_len = mem.shape[1]
    depth = w_in.shape[0]
    n = batch * seq
    assert d == D_MODEL and seq % S5_CHUNK == 0

    diff_inv = ROPE_THETA ** (-jnp.arange(0, DIFF_DH, 2, dtype=F32) / DIFF_DH)
    ret_inv = 1.0 / (ROPE_THETA ** jnp.linspace(0.0, 1.0, HEAD_W // 2, dtype=F32))
    inv = jnp.stack([jnp.tile(diff_inv, LANES // diff_inv.shape[0]),
                     jnp.tile(ret_inv, LANES // ret_inv.shape[0])])[:, None, :]
    lane = np.arange(LANES)
    sgn = np.stack([np.where(lane % DIFF_DH < DIFF_DH // 2, -1.0, 1.0),
                    np.where(lane < HEAD_W // 2, -1.0, 1.0)])[:, None, :]
    cos_t, sin_t = _rope_tables(seq, inv, jnp.asarray(sgn, F32), _tile(seq, 1024))

    piece = np.arange(w_in.shape[2]) // BRANCH_W
    halved = (piece >= N_IN_PIECES) | np.isin(piece, IN_GATE_PIECES)
    in_col_scale = jnp.asarray(np.where(halved, GATE_FOLD, 1.0), F32)

    x2 = x.reshape(n, d)
    mem2 = mem.reshape(batch * mem_len, d)
    for l in range(depth):
        lambda_init = 0.8 - 0.6 * math.exp(-0.3 * l)
        g = norm_g[l][None, :]
        w = w_in[l] * in_col_scale[None, :]
        proj = _in_proj(x2, g, w[:, 2 * BRANCH_W:].astype(BF16), _tile(n, 2048), 1536)
        u3, sg_t = _in_proj_t(x2, g, w[:, :2 * BRANCH_W].T.astype(BF16), _tile(n, 2048))

        kern_op, p_op, q_op, dec = _s5_prep(ssm_lambda_re[l], ssm_lambda_im[l], ssm_log_dt[l], ssm_b_re[l],
                                         ssm_b_im[l], ssm_c_re[l], ssm_c_im[l])
        y_t = _s5_conv(u3, ssm_d[l], kern_op, p_op, q_op, dec, batch, seq)

        rq, rk, dq, dk = _qk_prep(proj, cos_t, sin_t, diff_q_norm_g[l], diff_k_norm_g[l], seq, _tile(seq, 1024))
        ret_o = _retention(rq, rk, proj, batch, seq, _tile(seq, 256), math.gcd(batch, 8))
        dif_o = _diff_attn(dq, dk, proj, diff_q_norm_g[l], diff_k_norm_g[l], diff_lambda_q1[l][None],
                           diff_lambda_k1[l][None], diff_lambda_q2[l][None], diff_lambda_k2[l][None],
                           diff_head_norm_g[l][None], batch, seq, _tile(seq, 2048), _tile(seq, 512), lambda_init)
        mk, mv = _mem_kv(mem2, mem_norm_g[l][None, :], w_mem_kv[l].astype(BF16), mem_k_norm_g[l][None], mem_len)
        x2 = _merge(x2, y_t, sg_t, proj, ret_o, dif_o, mk, mv, ssm_w_glu[l].T.astype(BF16),
                    ssm_b_glu[l][:, None], mem_q_norm_g[l][None], w_branch[l].astype(BF16), GATE_FOLD * b_merge[l],
                    (GATE_FOLD * w_out[l]).astype(BF16), seq, mem_len, _tile(seq, 512))
    return x2.reshape(batch, seq, d)
```
